```python
import jax, jax.numpy as jnp
from jax import lax
import numpy as np

D_MODEL = 2048
BATCH = 8
SEQ = 2048
DEPTH = 2

HEAD_DIM = 128
N_MIXERS = 4
GROUP_WIDTH = D_MODEL // N_MIXERS
GROUP_HEADS = GROUP_WIDTH // HEAD_DIM
MIX_WIDTH = N_MIXERS * GROUP_WIDTH

MLA_Q_RANK = 512
MLA_KV_RANK = 256
MLA_NOPE_DIM = 128
MLA_ROPE_DIM = 64
MLA_V_DIM = HEAD_DIM
MLA_QK_DIM = MLA_NOPE_DIM + MLA_ROPE_DIM

RET_CHUNK = 128
SB_Q_BLOCK = 128
ATTN_Q_BLOCK = 128
MOBA_BLOCK = 256
MOBA_TOPK = 3
MOBA_Q_CHUNK = 16

ROPE_THETA = 10000.0
NORM_EPS = 1e-6
NEG = -1e30

FFN_DIM = 5632
N_EXPERTS = 8
MOE_TOPK = 2
MOE_GROUP = 512
N_DENSE = (DEPTH + 1) // 2
N_MOE = DEPTH // 2

IN_SPLITS = (MLA_Q_RANK, MLA_KV_RANK, MLA_ROPE_DIM) + (GROUP_WIDTH,) * 10
IN_WIDTH = MLA_Q_RANK + MLA_KV_RANK + MLA_ROPE_DIM + 10 * GROUP_WIDTH

kernel_name = "hymba_style_mla_retnet_stickbreak_moba_moe"


def rms_norm(x, g):
    xf = x.astype(jnp.float32)
    y = xf * lax.rsqrt(jnp.mean(xf * xf, axis=-1, keepdims=True) + NORM_EPS)
    return (y * g.astype(jnp.float32)).astype(x.dtype)


def modulate(h, shift, scale):
    return h * (1.0 + scale[:, None, :]) + shift[:, None, :]


def rope_tables(positions, dim):
    inv_freq = ROPE_THETA ** (-jnp.arange(0, dim, 2, dtype=jnp.float32) / dim)
    ang = positions.astype(jnp.float32)[:, None] * inv_freq[None, :]
    return jnp.cos(ang), jnp.sin(ang)


def apply_rope(x, cos, sin):
    half = x.shape[-1] // 2
    x1 = x[..., :half].astype(jnp.float32)
    x2 = x[..., half:].astype(jnp.float32)
    out = jnp.concatenate([x1 * cos - x2 * sin, x2 * cos + x1 * sin], axis=-1)
    return out.astype(x.dtype)


def split_heads(t, n_heads):
    b, s, _ = t.shape
    return t.reshape(b, s, n_heads, -1).transpose(0, 2, 1, 3)


def merge_heads(t):
    b, h, s, d = t.shape
    return t.transpose(0, 2, 1, 3).reshape(b, s, h * d)


def causal_softmax_attention(q, k, v, scale):
    b, h, s, dk = q.shape
    dv = v.shape[-1]
    nq = s // ATTN_Q_BLOCK
    qb = q.reshape(b, h, nq, ATTN_Q_BLOCK, dk).transpose(2, 0, 1, 3, 4)
    kpos = jnp.arange(s)

    def step(args):
        q_i, i = args
        sc = jnp.einsum('bhqd,bhkd->bhqk', q_i, k, preferred_element_type=jnp.float32) * scale
        qpos = i * ATTN_Q_BLOCK + jnp.arange(ATTN_Q_BLOCK)
        sc = jnp.where(kpos[None, :] <= qpos[:, None], sc, NEG)
        p = jax.nn.softmax(sc, axis=-1)
        return jnp.einsum('bhqk,bhkd->bhqd', p.astype(v.dtype), v)

    out = lax.map(step, (qb, jnp.arange(nq)))
    return out.transpose(1, 2, 0, 3, 4).reshape(b, h, s, dv)


def mla_mixer(c_q, c_kv, k_pe, q_norm_g, kv_norm_g, w_uq, w_ukv, q_head_g, k_head_g, cos_pe, sin_pe):
    b, s, _ = c_q.shape
    q = split_heads(rms_norm(c_q, q_norm_g) @ w_uq, GROUP_HEADS)
    kv = split_heads(rms_norm(c_kv, kv_norm_g) @ w_ukv, GROUP_HEADS)
    k_nope, v = kv[..., :MLA_NOPE_DIM], kv[..., MLA_NOPE_DIM:]
    k_rope = jnp.broadcast_to(k_pe[:, None], (b, GROUP_HEADS, s, MLA_ROPE_DIM))
    k = jnp.concatenate([k_nope, k_rope], axis=-1)
    q = rms_norm(q, q_head_g)
    k = rms_norm(k, k_head_g)
    q = jnp.concatenate([q[..., :MLA_NOPE_DIM], apply_rope(q[..., MLA_NOPE_DIM:], cos_pe, sin_pe)], axis=-1)
    k = jnp.concatenate([k[..., :MLA_NOPE_DIM], apply_rope(k[..., MLA_NOPE_DIM:], cos_pe, sin_pe)], axis=-1)
    return merge_heads(causal_softmax_attention(q, k, v, MLA_QK_DIM ** -0.5))


def retention_mixer(q, k, v, g, cos, sin):
    f32 = jnp.float32
    q = apply_rope(split_heads(q, GROUP_HEADS), cos, sin).astype(f32)
    k = apply_rope(split_heads(k, GROUP_HEADS), cos, sin).astype(f32) * (HEAD_DIM ** -0.5)
    v = split_heads(v, GROUP_HEADS).astype(f32)
    b, h, s, d = q.shape
    log_gamma = jnp.log(1.0 - 2.0 ** (-5.0 - jnp.arange(h, dtype=f32)))
    n = s // RET_CHUNK
    idx = jnp.arange(RET_CHUNK, dtype=f32)
    rel = idx[:, None] - idx[None, :]
    intra_decay = jnp.where(rel >= 0, jnp.exp(jnp.maximum(rel, 0.0)[None] * log_gamma[:, None, None]), 0.0)
    query_decay = jnp.exp((idx + 1.0)[None, :] * log_gamma[:, None])
    key_decay = jnp.exp((RET_CHUNK - 1.0 - idx)[None, :] * log_gamma[:, None])
    chunk_decay = jnp.exp(RET_CHUNK * log_gamma)
    qc = q.reshape(b, h, n, RET_CHUNK, d)
    kc = k.reshape(b, h, n, RET_CHUNK, d)
    vc = v.reshape(b, h, n, RET_CHUNK, d)
    scores = jnp.einsum('bhnid,bhnjd->bhnij', qc, kc) * intra_decay[None, :, None]
    intra = jnp.einsum('bhnij,bhnjd->bhnid', scores, vc)

    def step(state, xs):
        q_n, k_n, v_n = xs
        inter_n = jnp.einsum('bhid,bhde->bhie', q_n, state) * query_decay[None, :, :, None]
        state = state * chunk_decay[None, :, None, None] + jnp.einsum(
            'bhjd,bhje->bhde', k_n * key_decay[None, :, :, None], v_n)
        return state, inter_n

    init = jnp.zeros((b, h, d, d), f32)
    _, inter = lax.scan(step, init, (qc.transpose(2, 0, 1, 3, 4), kc.transpose(2, 0, 1, 3, 4),
                                     vc.transpose(2, 0, 1, 3, 4)))
    y = (intra + inter.transpose(1, 2, 0, 3, 4)).reshape(b, h, s, d)
    mu = jnp.mean(y, axis=-1, keepdims=True)
    var = jnp.mean(jnp.square(y - mu), axis=-1, keepdims=True)
    y = merge_heads((y - mu) * lax.rsqrt(var + NORM_EPS))
    return (jax.nn.silu(g.astype(f32)) * y).astype(g.dtype)


def stick_breaking_mixer(q, k, v):
    q = split_heads(q, GROUP_HEADS)
    k = split_heads(k, GROUP_HEADS)
    v = split_heads(v, GROUP_HEADS)
    b, h, s, d = q.shape
    nq = s // SB_Q_BLOCK
    qb = q.reshape(b, h, nq, SB_Q_BLOCK, d).transpose(2, 0, 1, 3, 4)
    kpos = jnp.arange(s)
    scale = d ** -0.5

    def step(args):
        q_i, i = args
        z = jnp.einsum('bhqd,bhkd->bhqk', q_i, k, preferred_element_type=jnp.float32) * scale
        qpos = i * SB_Q_BLOCK + jnp.arange(SB_Q_BLOCK)
        strict = kpos[None, :] < qpos[:, None]
        log_1m = jnp.where(strict, jax.nn.log_sigmoid(-z), 0.0)
        suffix = lax.cumsum(log_1m, axis=3, reverse=True) - log_1m
        a = jnp.where(strict, jnp.exp(jax.nn.log_sigmoid(z) + suffix), 0.0)
        return jnp.einsum('bhqk,bhkd->bhqd', a.astype(v.dtype), v)

    out = lax.map(step, (qb, jnp.arange(nq)))
    return merge_heads(out.transpose(1, 2, 0, 3, 4).reshape(b, h, s, d))


def moba_mixer(q, k, v, q_head_g, k_head_g, cos, sin):
    q = apply_rope(rms_norm(split_heads(q, GROUP_HEADS), q_head_g), cos, sin)
    k = apply_rope(rms_norm(split_heads(k, GROUP_HEADS), k_head_g), cos, sin)
    v = split_heads(v, GROUP_HEADS)
    b, h, s, d = q.shape
    nb = -(-s // MOBA_BLOCK)
    pad = nb * MOBA_BLOCK - s
    kb = jnp.pad(k, ((0, 0), (0, 0), (0, pad), (0, 0))).reshape(b, h, nb, MOBA_BLOCK, d)
    vb = jnp.pad(v, ((0, 0), (0, 0), (0, pad), (0, 0))).reshape(b, h, nb, MOBA_BLOCK, d)
    k_mean = jnp.mean(kb.astype(jnp.float32), axis=3)
    topk = min(MOBA_TOPK, nb)
    nc = s // MOBA_Q_CHUNK
    qc = q.reshape(b, h, nc, MOBA_Q_CHUNK, d).transpose(2, 0, 1, 3, 4)
    b_idx = jnp.arange(b)[:, None, None, None]
    h_idx = jnp.arange(h)[None, :, None, None]
    blk_ids = jnp.arange(nb)
    in_blk = jnp.arange(MOBA_BLOCK)
    scale = d ** -0.5
    n_sel = topk * MOBA_BLOCK

    def step(args):
        q_i, i = args
        q0 = i * MOBA_Q_CHUNK
        cur = q0 // MOBA_BLOCK
        qpos = q0 + jnp.arange(MOBA_Q_CHUNK)
        gate = jnp.einsum('bhqd,bhnd->bhqn', q_i.astype(jnp.float32), k_mean)
        gate = jnp.where(blk_ids < cur, gate, -jnp.inf)
        _, sel = lax.top_k(gate, topk)
        sel_ok = sel < cur
        k_sel = kb[b_idx, h_idx, sel]
        v_sel = vb[b_idx, h_idx, sel]
        s_sel = jnp.einsum('bhqd,bhqnkd->bhqnk', q_i, k_sel, preferred_element_type=jnp.float32) * scale
        s_sel = jnp.where(sel_ok[..., None], s_sel, NEG).reshape(b, h, MOBA_Q_CHUNK, n_sel)
        k_own = lax.dynamic_index_in_dim(kb, cur, axis=2, keepdims=False)
        v_own = lax.dynamic_index_in_dim(vb, cur, axis=2, keepdims=False)
        s_own = jnp.einsum('bhqd,bhkd->bhqk', q_i, k_own, preferred_element_type=jnp.float32) * scale
        own_pos = cur * MOBA_BLOCK + in_blk
        s_own = jnp.where(own_pos[None, :] <= qpos[:, None], s_own, NEG)
        p = jax.nn.softmax(jnp.concatenate([s_sel, s_own], axis=-1), axis=-1)
        p_sel = p[..., :n_sel].reshape(b, h, MOBA_Q_CHUNK, topk, MOBA_BLOCK).astype(v.dtype)
        p_own = p[..., n_sel:].astype(v.dtype)
        return (jnp.einsum('bhqnk,bhqnkd->bhqd', p_sel, v_sel)
                + jnp.einsum('bhqk,bhkd->bhqd', p_own, v_own))

    out = lax.map(step, (qc, jnp.arange(nc)))
    return merge_heads(out.transpose(1, 2, 0, 3, 4).reshape(b, h, s, d))


def hybrid_mixer(h, w_in, mla_q_norm_g, mla_kv_norm_g, mla_w_uq, mla_w_ukv, mla_q_head_g, mla_k_head_g,
                 moba_q_head_g, moba_k_head_g, group_norm_g, w_out, cos_pe, sin_pe, cos_full, sin_full):
    proj = h @ w_in
    points = []
    acc = 0
    for w in IN_SPLITS[:-1]:
        acc += w
        points.append(acc)
    (c_q, c_kv, k_pe, r_q, r_k, r_v, r_g, sb_q, sb_k, sb_v,
     mb_q, mb_k, mb_v) = jnp.split(proj, points, axis=-1)
    y_mla = mla_mixer(c_q, c_kv, k_pe, mla_q_norm_g, mla_kv_norm_g, mla_w_uq, mla_w_ukv,
                      mla_q_head_g, mla_k_head_g, cos_pe, sin_pe)
    y_ret = retention_mixer(r_q, r_k, r_v, r_g, cos_full, sin_full)
    y_sb = stick_breaking_mixer(sb_q, sb_k, sb_v)
    y_moba = moba_mixer(mb_q, mb_k, mb_v, moba_q_head_g, moba_k_head_g, cos_full, sin_full)
    groups = [rms_norm(y_mla, group_norm_g[0]), rms_norm(y_ret, group_norm_g[1]),
              rms_norm(y_sb, group_norm_g[2]), rms_norm(y_moba, group_norm_g[3])]
    return jnp.concatenate(groups, axis=-1) @ w_out


def swiglu(h, w1, w3, w2):
    return (jax.nn.silu(h @ w1) * (h @ w3)) @ w2


def moe_swiglu(h, router_w, w1, w3, w2):
    b, s, d = h.shape
    t = b * s
    xf = h.reshape(t, d)
    logits = jnp.einsum('td,de->te', xf, router_w, preferred_element_type=jnp.float32)
    top_logits, top_idx = lax.top_k(logits, MOE_TOPK)
    gates = jax.nn.softmax(top_logits, axis=-1)
    n_slots = t * MOE_TOPK
    flat_e = top_idx.reshape(-1)
    flat_t = jnp.arange(n_slots) // MOE_TOPK
    flat_g = gates.reshape(-1)
    order = jnp.argsort(flat_e)
    sorted_e = flat_e[order]
    counts = jnp.bincount(flat_e, length=N_EXPERTS)
    starts = jnp.cumsum(counts) - counts
    padded = (counts + MOE_GROUP - 1) // MOE_GROUP * MOE_GROUP
    pad_ends = jnp.cumsum(padded)
    pad_starts = pad_ends - padded
    dest = pad_starts[sorted_e] + jnp.arange(n_slots) - starts[sorted_e]
    n_chunks = -(-n_slots // MOE_GROUP) + N_EXPERTS
    n_buf = n_chunks * MOE_GROUP
    buf_t = jnp.full((n_buf,), t, jnp.int32).at[dest].set(flat_t[order].astype(jnp.int32))
    buf_g = jnp.zeros((n_buf,), jnp.float32).at[dest].set(flat_g[order])
    chunk_e = jnp.minimum(jnp.searchsorted(pad_ends, jnp.arange(n_chunks) * MOE_GROUP, side='right'),
                          N_EXPERTS - 1)
    x_pad = jnp.concatenate([xf, jnp.zeros((1, d), xf.dtype)], axis=0)
    x_buf = x_pad[buf_t].reshape(n_chunks, MOE_GROUP, d)

    def expert_chunk(args):
        xc, e = args
        return swiglu(xc, w1[e], w3[e], w2[e])

    y_buf = lax.map(expert_chunk, (x_buf, chunk_e)).reshape(n_buf, d)
    out = jnp.zeros((t + 1, d), jnp.float32).at[buf_t].add(y_buf.astype(jnp.float32) * buf_g[:, None])
    return out[:t].reshape(b, s, d).astype(h.dtype)


def setup_inputs(seed: int = 0) -> dict:
    key = jax.random.key(seed)
    ks = jax.random.split(key, 26)
    f32 = jnp.float32

    def nrm(k, shape, std):
        return jax.random.normal(k, shape, f32) * std

    def gain(k, shape):
        return 1.0 + 0.02 * jax.random.normal(k, shape, f32)

    return {
        "x": nrm(ks[0], (BATCH, SEQ, D_MODEL), 1.0),
        "c": nrm(ks[1], (BATCH, D_MODEL), 1.0),
        "positions": jnp.arange(SEQ, dtype=jnp.int32),
        "ada_w": nrm(ks[2], (DEPTH, D_MODEL, 6 * D_MODEL), 0.5 * D_MODEL ** -0.5),
        "ada_b": nrm(ks[3], (DEPTH, 6 * D_MODEL), 0.02),
        "norm_mix_g": gain(ks[4], (DEPTH, D_MODEL)),
        "norm_ffn_g": gain(ks[5], (DEPTH, D_MODEL)),
        "w_in": nrm(ks[6], (DEPTH, D_MODEL, IN_WIDTH), D_MODEL ** -0.5),
        "mla_q_norm_g": gain(ks[7], (DEPTH, MLA_Q_RANK)),
        "mla_kv_norm_g": gain(ks[8], (DEPTH, MLA_KV_RANK)),
        "mla_w_uq": nrm(ks[9], (DEPTH, MLA_Q_RANK, GROUP_HEADS * MLA_QK_DIM), MLA_Q_RANK ** -0.5),
        "mla_w_ukv": nrm(ks[10], (DEPTH, MLA_KV_RANK, GROUP_HEADS * (MLA_NOPE_DIM + MLA_V_DIM)), MLA_KV_RANK ** -0.5),
        "mla_q_head_g": gain(ks[11], (DEPTH, MLA_QK_DIM)),
        "mla_k_head_g": gain(ks[12], (DEPTH, MLA_QK_DIM)),
        "moba_q_head_g": gain(ks[13], (DEPTH, HEAD_DIM)),
        "moba_k_head_g": gain(ks[14], (DEPTH, HEAD_DIM)),
        "group_norm_g": gain(ks[15], (DEPTH, N_MIXERS, GROUP_WIDTH)),
        "w_out": nrm(ks[16], (DEPTH, MIX_WIDTH, D_MODEL), MIX_WIDTH ** -0.5),
        "ffn_w1": nrm(ks[17], (N_DENSE, D_MODEL, FFN_DIM), D_MODEL ** -0.5),
        "ffn_w3": nrm(ks[18], (N_DENSE, D_MODEL, FFN_DIM), D_MODEL ** -0.5),
        "ffn_w2": nrm(ks[19], (N_DENSE, FFN_DIM, D_MODEL), FFN_DIM ** -0.5),
        "router_w": nrm(ks[20], (N_MOE, D_MODEL, N_EXPERTS), D_MODEL ** -0.5),
        "moe_w1": nrm(ks[21], (N_MOE, N_EXPERTS, D_MODEL, FFN_DIM), D_MODEL ** -0.5),
        "moe_w3": nrm(ks[22], (N_MOE, N_EXPERTS, D_MODEL, FFN_DIM), D_MODEL ** -0.5),
        "moe_w2": nrm(ks[23], (N_MOE, N_EXPERTS, FFN_DIM, D_MODEL), FFN_DIM ** -0.5),
    }


def reference(x, c, positions, ada_w, ada_b, norm_mix_g, norm_ffn_g, w_in,
              mla_q_norm_g, mla_kv_norm_g, mla_w_uq, mla_w_ukv, mla_q_head_g, mla_k_head_g,
              moba_q_head_g, moba_k_head_g, group_norm_g, w_out,
              ffn_w1, ffn_w3, ffn_w2, router_w, moe_w1, moe_w3, moe_w2):
    cos_pe, sin_pe = rope_tables(positions, MLA_ROPE_DIM)
    cos_full, sin_full = rope_tables(positions, HEAD_DIM)
    cond = jax.nn.silu(c)
    for l in range(DEPTH):
        mod = cond @ ada_w[l] + ada_b[l]
        shift_m, scale_m, gate_m, shift_f, scale_f, gate_f = jnp.split(mod, 6, axis=-1)
        h = modulate(rms_norm(x, norm_mix_g[l]), shift_m, scale_m)
        y = hybrid_mixer(h, w_in[l], mla_q_norm_g[l], mla_kv_norm_g[l], mla_w_uq[l], mla_w_ukv[l],
                         mla_q_head_g[l], mla_k_head_g[l], moba_q_head_g[l], moba_k_head_g[l],
                         group_norm_g[l], w_out[l], cos_pe, sin_pe, cos_full, sin_full)
        x = x + gate_m[:, None, :] * y
        h = modulate(rms_norm(x, norm_ffn_g[l]), shift_f, scale_f)
        j = l // 2
        if l % 2 == 0:
            y = swiglu(h, ffn_w1[j], ffn_w3[j], ffn_w2[j])
        else:
            y = moe_swiglu(h, router_w[j], moe_w1[j], moe_w3[j], moe_w2[j])
        x = x + gate_f[:, None, :] * y
    return x
```

```python
import functools

import jax
import jax.numpy as jnp
from jax import lax
from jax.experimental import pallas as pl
from jax.experimental.pallas import tpu as pltpu

F32 = jnp.float32
BF16 = jnp.bfloat16

HEAD_DIM = 128
GROUP_HEADS = 4
GROUP_WIDTH = 512
MLA_Q_RANK = 512
MLA_KV_RANK = 256
MLA_NOPE_DIM = 128
MLA_ROPE_DIM = 64
MLA_QK_DIM = MLA_NOPE_DIM + MLA_ROPE_DIM
MLA_QK_PAD = 256
RET_CHUNK = 128
MOBA_BLOCK = 256
MOBA_TOPK = 3
ROPE_THETA = 10000.0
NORM_EPS = 1e-6
NEG = -1e30
N_EXPERTS = 8
MOE_TOPK = 2

LANES = 128
ATTN_BLOCK = 256
VMEM_LIMIT = 56 * 1024 * 1024

COL_CQ = 0
COL_CKV = 512
COL_KPE = 768
COL_RET = 896
COL_SB = COL_RET + 4 * GROUP_WIDTH
COL_MOBA = COL_SB + 3 * GROUP_WIDTH
PROJ_WIDTH = 6144


def _cparams(sem):
    return pltpu.CompilerParams(dimension_semantics=sem, vmem_limit_bytes=VMEM_LIMIT)


def _dot(a, b):
    return jnp.dot(a, b, preferred_element_type=F32)


def _dot_nt(a, b):
    return lax.dot_general(a, b, (((1,), (1,)), ((), ())), preferred_element_type=F32)


def _sigmoid(x):
    return 1.0 / (1.0 + jnp.exp(-x))


def _softplus(z):
    return jnp.maximum(z, 0.0) + jnp.log(1.0 + jnp.exp(-jnp.abs(z)))


def _rms(xf, width=None):
    width = xf.shape[-1] if width is None else width
    ss = jnp.sum(xf * xf, axis=-1, keepdims=True) * (1.0 / width)
    return xf * lax.rsqrt(ss + NORM_EPS)


def _ada_kernel(c_ref, w_ref, b_ref, o_ref):
    c = c_ref[...]
    cond = c * _sigmoid(c)
    o_ref[...] = _dot(cond.astype(BF16), w_ref[...].astype(BF16)) + b_ref[...]


def ada_modulation(c, ada_w, ada_b, tn=1024):
    depth, d, n = ada_w.shape
    b = c.shape[0]
    return pl.pallas_call(
        _ada_kernel,
        grid=(depth, n // tn),
        in_specs=[
            pl.BlockSpec((b, d), lambda l, j: (0, 0)),
            pl.BlockSpec((None, d, tn), lambda l, j: (l, 0, j)),
            pl.BlockSpec((None, 1, tn), lambda l, j: (l, 0, j)),
        ],
        out_specs=pl.BlockSpec((None, b, tn), lambda l, j: (l, 0, j)),
        out_shape=jax.ShapeDtypeStruct((depth, b, n), F32),
        compiler_params=_cparams(("parallel", "parallel")),
        name="ada_modulation",
    )(c, ada_w, ada_b.reshape(depth, 1, n))


def _norm_mod_rows(x_ref, g_ref, sh_ref, sc_ref, dst_ref, rows=256):
    tm = x_ref.shape[0]
    g = g_ref[...]
    mul = 1.0 + sc_ref[...]
    sh = sh_ref[...]

    def body(r, _):
        sl = pl.ds(pl.multiple_of(r * rows, rows), rows)
        y = _rms(x_ref[sl, :]) * g
        dst_ref[sl, :] = (y * mul + sh).astype(dst_ref.dtype)
        return 0

    lax.fori_loop(0, tm // rows, body, 0)


def _proj_kernel(x_ref, g_ref, sh_ref, sc_ref, w_ref, o_ref, h_ref):
    @pl.when(pl.program_id(1) == 0)
    def _():
        _norm_mod_rows(x_ref, g_ref, sh_ref, sc_ref, h_ref)

    o_ref[...] = _dot(h_ref[...], w_ref[...]).astype(o_ref.dtype)


def norm_mod_proj(x2, g, shift, scale, w, seq, tm=1024, tn=1024):
    t, d = x2.shape
    n = w.shape[1]
    tm = min(tm, seq)
    per_b = seq // tm
    bsz = shift.shape[0]
    return pl.pallas_call(
        _proj_kernel,
        grid=(t // tm, n // tn),
        in_specs=[
            pl.BlockSpec((tm, d), lambda i, j: (i, 0)),
            pl.BlockSpec((1, d), lambda i, j: (0, 0)),
            pl.BlockSpec((None, 1, d), lambda i, j: (i // per_b, 0, 0)),
            pl.BlockSpec((None, 1, d), lambda i, j: (i // per_b, 0, 0)),
            pl.BlockSpec((d, tn), lambda i, j: (0, j)),
        ],
        out_specs=pl.BlockSpec((tm, tn), lambda i, j: (i, j)),
        out_shape=jax.ShapeDtypeStruct((t, n), BF16),
        scratch_shapes=[pltpu.VMEM((tm, d), BF16)],
        compiler_params=_cparams(("parallel", "arbitrary")),
        name="norm_mod_proj",
    )(x2, g.reshape(1, d), shift.reshape(bsz, 1, d), scale.reshape(bsz, 1, d), w)


def _rope_tables(positions, dim):
    inv_freq = ROPE_THETA ** (-jnp.arange(0, dim, 2, dtype=F32) / dim)
    ang = positions.astype(F32)[:, None] * inv_freq[None, :]
    cos, sin = jnp.cos(ang), jnp.sin(ang)
    pad = jnp.zeros((positions.shape[0], LANES - dim), F32)
    return (jnp.concatenate([cos, cos, pad], axis=-1),
            jnp.concatenate([-sin, sin, pad], axis=-1))


def _rope_full(z, cos2, sin_s):
    return z * cos2 + pltpu.roll(z, 64, 1) * sin_s


def _rope_64(z, cos2, sin_s):
    lane = lax.broadcasted_iota(jnp.int32, z.shape, 1)
    partner = jnp.where(lane < 32, pltpu.roll(z, 96, 1), pltpu.roll(z, 32, 1))
    return z * cos2 + partner * sin_s


def _mla_prep_kernel(p_ref, qg_ref, kvg_ref, wuq_ref, wukv_ref, qhg_ref, khg_ref,
                     cos_ref, sin_ref, q_ref, k_ref, v_ref):
    p = p_ref[...].astype(F32)
    c_q = p[:, COL_CQ:COL_CQ + MLA_Q_RANK]
    c_kv = p[:, COL_CKV:COL_CKV + MLA_KV_RANK]
    k_pe = p[:, COL_KPE:COL_KPE + LANES]
    q = _dot((_rms(c_q) * qg_ref[...]).astype(BF16), wuq_ref[...])
    kv = _dot((_rms(c_kv) * kvg_ref[...]).astype(BF16), wukv_ref[...])
    cos2, sin_s = cos_ref[...], sin_ref[...]
    qhg, khg = qhg_ref[...], khg_ref[...]
    scale = MLA_QK_DIM ** -0.5
    pe_ss = jnp.sum(k_pe * k_pe, axis=-1, keepdims=True)
    for h in range(GROUP_HEADS):
        qh = q[:, h * MLA_QK_PAD:(h + 1) * MLA_QK_PAD]
        qh = _rms(qh, MLA_QK_DIM) * qhg * scale
        q_ref[h, :, :LANES] = qh[:, :LANES].astype(q_ref.dtype)
        q_ref[h, :, LANES:] = _rope_64(qh[:, LANES:], cos2, sin_s).astype(q_ref.dtype)
        k_nope = kv[:, 2 * h * LANES:(2 * h + 1) * LANES]
        ss = (jnp.sum(k_nope * k_nope, axis=-1, keepdims=True) + pe_ss) * (1.0 / MLA_QK_DIM)
        r = lax.rsqrt(ss + NORM_EPS)
        k_ref[h, :, :LANES] = (k_nope * r * khg[:, :LANES]).astype(k_ref.dtype)
        k_ref[h, :, LANES:] = _rope_64(k_pe * r * khg[:, LANES:], cos2, sin_s).astype(k_ref.dtype)
        v_ref[h] = kv[:, (2 * h + 1) * LANES:(2 * h + 2) * LANES].astype(v_ref.dtype)


def mla_prep(proj3, q_norm_g, kv_norm_g, w_uq_pad, w_ukv, q_head_g_pad, k_head_g_pad,
             cos_pe, sin_pe, tm=256):
    bsz, seq, _ = proj3.shape
    h = GROUP_HEADS
    n_in = COL_KPE + LANES
    const = lambda shape: pl.BlockSpec(shape, lambda b, i: (0,) * len(shape))
    return pl.pallas_call(
        _mla_prep_kernel,
        grid=(bsz, seq // tm),
        in_specs=[
            pl.BlockSpec((None, tm, n_in), lambda b, i: (b, i, 0)),
            const((1, MLA_Q_RANK)), const((1, MLA_KV_RANK)),
            const((MLA_Q_RANK, h * MLA_QK_PAD)), const((MLA_KV_RANK, h * 2 * LANES)),
            const((1, MLA_QK_PAD)), const((1, MLA_QK_PAD)),
            pl.BlockSpec((tm, LANES), lambda b, i: (i, 0)),
            pl.BlockSpec((tm, LANES), lambda b, i: (i, 0)),
        ],
        out_specs=[
            pl.BlockSpec((None, h, tm, MLA_QK_PAD), lambda b, i: (b, 0, i, 0)),
            pl.BlockSpec((None, h, tm, MLA_QK_PAD), lambda b, i: (b, 0, i, 0)),
            pl.BlockSpec((None, h, tm, LANES), lambda b, i: (b, 0, i, 0)),
        ],
        out_shape=[
            jax.ShapeDtypeStruct((bsz, h, seq, MLA_QK_PAD), BF16),
            jax.ShapeDtypeStruct((bsz, h, seq, MLA_QK_PAD), BF16),
            jax.ShapeDtypeStruct((bsz, h, seq, LANES), BF16),
        ],
        compiler_params=_cparams(("parallel", "parallel")),
        name="mla_prep",
    )(proj3, q_norm_g.reshape(1, -1), kv_norm_g.reshape(1, -1), w_uq_pad, w_ukv,
      q_head_g_pad, k_head_g_pad, cos_pe, sin_pe)


def _attn_kernel(*refs, moba):
    if moba:
        q_ref, k_ref, v_ref, kmean_ref, o_ref = refs
    else:
        q_ref, k_ref, v_ref, o_ref = refs
    tq = q_ref.shape[0]
    tk = tq
    i = pl.program_id(2)
    q = q_ref[...]
    row = lax.broadcasted_iota(jnp.int32, (tq, tk), 0)
    col = lax.broadcasted_iota(jnp.int32, (tq, tk), 1)

    if moba:
        nb = kmean_ref.shape[0]
        qf = q.astype(F32)
        gate = [jnp.sum(qf * kmean_ref[n:n + 1, :], axis=-1, keepdims=True) for n in range(nb)]
        past = [jnp.where(n < i, 1, 0) for n in range(nb)]
        sel = []
        for n in range(nb):
            rank = jnp.zeros((tq, 1), jnp.int32)
            for o in range(nb):
                if o == n:
                    continue
                tie = 1 if o < n else 0
                ahead = jnp.where(gate[o] > gate[n], 1, jnp.where(gate[o] == gate[n], tie, 0))
                rank = rank + ahead * past[o]
            sel.append(jnp.where(rank < MOBA_TOPK, 1.0, 0.0) * past[n].astype(F32))

    def body(j, carry):
        m, l, acc = carry
        ks = pl.ds(pl.multiple_of(j * tk, tk), tk)
        s = _dot_nt(q, k_ref[ks, :])
        if moba:
            picked = jnp.zeros((tq, 1), F32)
            for n in range(nb):
                picked = picked + sel[n] * jnp.where(j == n, 1.0, 0.0)
            own = col <= row + jnp.where(j == i, 0, -2 * tk)
            mask = jnp.where(own, 1.0, picked) > 0.5
        else:
            mask = col <= row + (i - j) * tk
        s = jnp.where(mask, s, NEG)
        m_new = jnp.maximum(m, jnp.max(s, axis=-1, keepdims=True))
        alpha = jnp.exp(m - m_new)
        p = jnp.where(mask, jnp.exp(s - m_new), 0.0)
        l = alpha * l + jnp.sum(p, axis=-1, keepdims=True)
        acc = alpha * acc + _dot(p.astype(v_ref.dtype), v_ref[ks, :])
        return m_new, l, acc

    dv = v_ref.shape[1]
    init = (jnp.full((tq, 1), NEG, F32), jnp.zeros((tq, 1), F32), jnp.zeros((tq, dv), F32))
    _, l, acc = lax.fori_loop(0, i + 1, body, init)
    o_ref[...] = (acc / l).astype(o_ref.dtype)


def _attention(q_arr, q_spec, k_arr, k_spec, v_arr, v_spec, bsz, seq, kmean=None):
    h = GROUP_HEADS
    tq = ATTN_BLOCK
    in_specs = [q_spec, k_spec, v_spec]
    args = [q_arr, k_arr, v_arr]
    if kmean is not None:
        nb = kmean.shape[2]
        in_specs.append(pl.BlockSpec((None, None, nb, LANES), lambda b, hh, i: (b, hh, 0, 0)))
        args.append(kmean)
    return pl.pallas_call(
        functools.partial(_attn_kernel, moba=kmean is not None),
        grid=(bsz, h, seq // tq),
        in_specs=in_specs,
        out_specs=pl.BlockSpec((None, tq, LANES), lambda b, hh, i: (b, i, hh)),
        out_shape=jax.ShapeDtypeStruct((bsz, seq, h * LANES), BF16),
        compiler_params=_cparams(("parallel", "parallel", "arbitrary")),
        name="moba_attention" if kmean is not None else "mla_attention",
    )(*args)


def _head_specs(seq, dk):
    tq = ATTN_BLOCK
    q_spec = pl.BlockSpec((None, None, tq, dk), lambda b, hh, i: (b, hh, i, 0))
    kv_spec = lambda d: pl.BlockSpec((None, None, seq, d), lambda b, hh, i: (b, hh, 0, 0))
    return q_spec, kv_spec


def _proj_col_spec(rows, col, whole):
    cb = col // LANES
    if whole:
        return pl.BlockSpec((None, rows, LANES), lambda b, hh, i: (b, 0, cb + hh))
    return pl.BlockSpec((None, rows, LANES), lambda b, hh, i: (b, i, cb + hh))


def mla_attention(q, k, v):
    bsz, _, seq, dk = q.shape
    q_spec, kv_spec = _head_specs(seq, dk)
    return _attention(q, q_spec, k, kv_spec(dk), v, kv_spec(LANES), bsz, seq)


def _moba_prep_kernel(q_ref, k_ref, qg_ref, kg_ref, cos_ref, sin_ref, qo_ref, ko_ref, km_ref):
    seq = q_ref.shape[0]
    scale = HEAD_DIM ** -0.5
    for blk in range(seq // MOBA_BLOCK):
        sl = slice(blk * MOBA_BLOCK, (blk + 1) * MOBA_BLOCK)
        cos2, sin_s = cos_ref[sl, :], sin_ref[sl, :]
        qn = _rms(q_ref[sl, :].astype(F32)) * qg_ref[...]
        qo_ref[sl, :] = (_rope_full(qn, cos2, sin_s) * scale).astype(qo_ref.dtype)
        kn = _rope_full(_rms(k_ref[sl, :].astype(F32)) * kg_ref[...], cos2, sin_s)
        ko_ref[sl, :] = kn.astype(ko_ref.dtype)
        km_ref[blk:blk + 1, :] = jnp.mean(kn, axis=0, keepdims=True)


def moba_prep(proj3, q_head_g, k_head_g, cos_full, sin_full):
    bsz, seq, _ = proj3.shape
    h = GROUP_HEADS
    nb = seq // MOBA_BLOCK
    grid3 = lambda spec: spec
    head_out = pl.BlockSpec((None, None, seq, LANES), lambda b, hh, i: (b, hh, 0, 0))
    const = lambda shape: pl.BlockSpec(shape, lambda b, hh, i: (0,) * len(shape))
    return pl.pallas_call(
        _moba_prep_kernel,
        grid=(bsz, h, 1),
        in_specs=[
            grid3(_proj_col_spec(seq, COL_MOBA, True)),
            grid3(_proj_col_spec(seq, COL_MOBA + GROUP_WIDTH, True)),
            const((1, LANES)), const((1, LANES)),
            const((seq, LANES)), const((seq, LANES)),
        ],
        out_specs=[head_out, head_out,
                   pl.BlockSpec((None, None, nb, LANES), lambda b, hh, i: (b, hh, 0, 0))],
        out_shape=[
            jax.ShapeDtypeStruct((bsz, h, seq, LANES), BF16),
            jax.ShapeDtypeStruct((bsz, h, seq, LANES), BF16),
            jax.ShapeDtypeStruct((bsz, h, nb, LANES), F32),
        ],
        compiler_params=_cparams(("parallel", "parallel", "arbitrary")),
        name="moba_prep",
    )(proj3, proj3, q_head_g.reshape(1, -1), k_head_g.reshape(1, -1), cos_full, sin_full)


def moba_attention(q, k, kmean, proj3):
    bsz, _, seq, dk = q.shape
    q_spec, kv_spec = _head_specs(seq, dk)
    v_spec = _proj_col_spec(seq, COL_MOBA + 2 * GROUP_WIDTH, True)
    return _attention(q, q_spec, k, kv_spec(dk), proj3, v_spec, bsz, seq, kmean=kmean)


def _sb_kernel(q_ref, k_ref, v_ref, o_ref):
    tq = q_ref.shape[0]
    tk = tq
    i = pl.program_id(2)
    q = q_ref[...]
    scale = HEAD_DIM ** -0.5
    row = lax.broadcasted_iota(jnp.int32, (tq, tk), 0)
    col = lax.broadcasted_iota(jnp.int32, (tq, tk), 1)
    later = jnp.where(row > col, 1.0, 0.0).astype(BF16)

    def body(step, carry):
        tail, acc = carry
        j = i - step
        ks = pl.ds(pl.multiple_of(j * tk, tk), tk)
        z = _dot_nt(q, k_ref[ks, :]) * scale
        strict = col < row + (i - j) * tk
        log_1m = jnp.where(strict, -_softplus(z), 0.0)
        hi = log_1m.astype(BF16)
        lo = (log_1m - hi.astype(F32)).astype(BF16)
        suffix = _dot(hi, later) + _dot(lo, later)
        a = jnp.where(strict, jnp.exp(z + log_1m + suffix + tail), 0.0)
        acc = acc + _dot(a.astype(v_ref.dtype), v_ref[ks, :])
        tail = tail + jnp.sum(log_1m, axis=-1, keepdims=True)
        return tail, acc

    init = (jnp.zeros((tq, 1), F32), jnp.zeros((tq, v_ref.shape[1]), F32))
    _, acc = lax.fori_loop(0, i + 1, body, init)
    o_ref[...] = acc.astype(o_ref.dtype)


def sb_attention(proj3):
    bsz, seq, _ = proj3.shape
    tq = ATTN_BLOCK
    return pl.pallas_call(
        _sb_kernel,
        grid=(bsz, GROUP_HEADS, seq // tq),
        in_specs=[
            _proj_col_spec(tq, COL_SB, False),
            _proj_col_spec(seq, COL_SB + GROUP_WIDTH, True),
            _proj_col_spec(seq, COL_SB + 2 * GROUP_WIDTH, True),
        ],
        out_specs=pl.BlockSpec((None, tq, LANES), lambda b, hh, i: (b, i, hh)),
        out_shape=jax.ShapeDtypeStruct((bsz, seq, GROUP_WIDTH), BF16),
        compiler_params=_cparams(("parallel", "parallel", "arbitrary")),
        name="sb_attention",
    )(proj3, proj3, proj3)


def _ret_kernel(q_ref, k_ref, v_ref, g_ref, cos_ref, sin_ref, lg_ref, o_ref):
    seq = q_ref.shape[0]
    c = RET_CHUNK
    log_gamma = lg_ref[...]
    ri = lax.broadcasted_iota(jnp.int32, (c, c), 0).astype(F32)
    ci = lax.broadcasted_iota(jnp.int32, (c, c), 1).astype(F32)
    rel = ri - ci
    intra_decay = jnp.where(rel >= 0, jnp.exp(jnp.maximum(rel, 0.0) * log_gamma), 0.0)
    idx = lax.broadcasted_iota(jnp.int32, (c, 1), 0).astype(F32)
    query_decay = jnp.exp((idx + 1.0) * log_gamma)
    key_decay = jnp.exp((c - 1.0 - idx) * log_gamma)
    chunk_decay = jnp.exp(c * log_gamma)
    k_scale = HEAD_DIM ** -0.5

    def body(n, state):
        sl = pl.ds(pl.multiple_of(n * c, c), c)
        cos2, sin_s = cos_ref[sl, :], sin_ref[sl, :]
        q = _rope_full(q_ref[sl, :].astype(F32), cos2, sin_s)
        k = _rope_full(k_ref[sl, :].astype(F32), cos2, sin_s) * k_scale
        vb = v_ref[sl, :]
        qb = q.astype(BF16)
        scores = _dot_nt(qb, k.astype(BF16)) * intra_decay
        y = _dot(scores.astype(BF16), vb)
        y = y + _dot(qb, state.astype(BF16)) * query_decay
        kd_t = jnp.transpose(k * key_decay).astype(BF16)
        state = state * chunk_decay + _dot(kd_t, vb)
        mu = jnp.mean(y, axis=-1, keepdims=True)
        yc = y - mu
        var = jnp.mean(yc * yc, axis=-1, keepdims=True)
        yn = yc * lax.rsqrt(var + NORM_EPS)
        g = g_ref[sl, :].astype(F32)
        o_ref[sl, :] = (g * _sigmoid(g) * yn).astype(o_ref.dtype)
        return state

    lax.fori_loop(0, seq // c, body, jnp.zeros((HEAD_DIM, HEAD_DIM), F32))


def retention(proj3, cos_full, sin_full):
    bsz, seq, _ = proj3.shape
    h = GROUP_HEADS
    log_gamma = jnp.log(1.0 - 2.0 ** (-5.0 - jnp.arange(h, dtype=F32))).reshape(h, 1, 1)
    col = lambda k: pl.BlockSpec((None, seq, LANES),
                                 lambda b, hh: (b, 0, (COL_RET + k * GROUP_WIDTH) // LANES + hh))
    const = lambda shape: pl.BlockSpec(shape, lambda b, hh: (0,) * len(shape))
    return pl.pallas_call(
        _ret_kernel,
        grid=(bsz, h),
        in_specs=[col(0), col(1), col(2), col(3), const((seq, LANES)), const((seq, LANES)),
                  pl.BlockSpec((None, 1, 1), lambda b, hh: (hh, 0, 0))],
        out_specs=pl.BlockSpec((None, seq, LANES), lambda b, hh: (b, 0, hh)),
        out_shape=jax.ShapeDtypeStruct((bsz, seq, GROUP_WIDTH), BF16),
        compiler_params=_cparams(("parallel", "parallel")),
        name="retention",
    )(proj3, proj3, proj3, proj3, cos_full, sin_full, log_gamma)


def _out_proj_kernel(y0_ref, y1_ref, y2_ref, y3_ref, gg_ref, w_ref, x_ref, gate_ref, o_ref, h_ref):
    @pl.when(pl.program_id(1) == 0)
    def _():
        for grp, y_ref in enumerate((y0_ref, y1_ref, y2_ref, y3_ref)):
            yn = _rms(y_ref[...].astype(F32)) * gg_ref[grp:grp + 1, :]
            h_ref[:, grp * GROUP_WIDTH:(grp + 1) * GROUP_WIDTH] = yn.astype(h_ref.dtype)

    o_ref[...] = x_ref[...] + gate_ref[...] * _dot(h_ref[...], w_ref[...])


def out_proj_residual(ys, group_g, w_out, x2, gate, seq, tm=512, tn=1024):
    t, d = x2.shape
    tm = min(tm, seq)
    per_b = seq // tm
    bsz = gate.shape[0]
    y_spec = pl.BlockSpec((tm, GROUP_WIDTH), lambda i, j: (i, 0))
    return pl.pallas_call(
        _out_proj_kernel,
        grid=(t // tm, d // tn),
        in_specs=[y_spec, y_spec, y_spec, y_spec,
                  pl.BlockSpec((4, GROUP_WIDTH), lambda i, j: (0, 0)),
                  pl.BlockSpec((4 * GROUP_WIDTH, tn), lambda i, j: (0, j)),
                  pl.BlockSpec((tm, tn), lambda i, j: (i, j)),
                  pl.BlockSpec((None, 1, tn), lambda i, j: (i // per_b, 0, j))],
        out_specs=pl.BlockSpec((tm, tn), lambda i, j: (i, j)),
        out_shape=jax.ShapeDtypeStruct((t, d), F32),
        scratch_shapes=[pltpu.VMEM((tm, 4 * GROUP_WIDTH), BF16)],
        compiler_params=_cparams(("parallel", "arbitrary")),
        name="out_proj_residual",
    )(*[y.reshape(t, GROUP_WIDTH) for y in ys], group_g, w_out, x2, gate.reshape(bsz, 1, d))


def _swiglu_chunk(h, w1, w3, w2):
    a = _dot(h, w1)
    a = a * _sigmoid(a) * _dot(h, w3)
    return _dot(a.astype(BF16), w2)


def _ffn_kernel(x_ref, g_ref, sh_ref, sc_ref, gate_ref, w1_ref, w3_ref, w2_ref, o_ref,
                h_ref, acc_ref):
    f = pl.program_id(1)

    @pl.when(f == 0)
    def _():
        _norm_mod_rows(x_ref, g_ref, sh_ref, sc_ref, h_ref)

    y = _swiglu_chunk(h_ref[...], w1_ref[...], w3_ref[...], w2_ref[...])

    @pl.when(f == 0)
    def _():
        acc_ref[...] = y

    @pl.when(f > 0)
    def _():
        acc_ref[...] += y

    @pl.when(f == pl.num_programs(1) - 1)
    def _():
        o_ref[...] = x_ref[...] + gate_ref[...] * acc_ref[...]


def dense_ffn_residual(x2, g, shift, scale, gate, w1, w3, w2, seq, tm=512, tf=512):
    t, d = x2.shape
    ffn = w1.shape[1]
    tm = min(tm, seq)
    per_b = seq // tm
    bsz = gate.shape[0]
    row = lambda: pl.BlockSpec((None, 1, d), lambda i, f: (i // per_b, 0, 0))
    return pl.pallas_call(
        _ffn_kernel,
        grid=(t // tm, ffn // tf),
        in_specs=[pl.BlockSpec((tm, d), lambda i, f: (i, 0)),
                  pl.BlockSpec((1, d), lambda i, f: (0, 0)),
                  row(), row(), row(),
                  pl.BlockSpec((d, tf), lambda i, f: (0, f)),
                  pl.BlockSpec((d, tf), lambda i, f: (0, f)),
                  pl.BlockSpec((tf, d), lambda i, f: (f, 0))],
        out_specs=pl.BlockSpec((tm, d), lambda i, f: (i, 0)),
        out_shape=jax.ShapeDtypeStruct((t, d), F32),
        scratch_shapes=[pltpu.VMEM((tm, d), BF16), pltpu.VMEM((tm, d), F32)],
        compiler_params=_cparams(("parallel", "arbitrary")),
        name="dense_ffn_residual",
    )(x2, g.reshape(1, d), shift.reshape(bsz, 1, d), scale.reshape(bsz, 1, d),
      gate.reshape(bsz, 1, d), w1, w3, w2)


def _router_kernel(x_ref, g_ref, sh_ref, sc_ref, rw_ref, h_ref, idx_ref, gates_ref):
    _norm_mod_rows(x_ref, g_ref, sh_ref, sc_ref, h_ref)
    logits = jnp.dot(h_ref[...], rw_ref[...], precision=lax.Precision.HIGHEST,
                     preferred_element_type=F32)
    lane = lax.broadcasted_iota(jnp.int32, logits.shape, 1)
    lane_f = lane.astype(F32)
    logits = jnp.where(lane < N_EXPERTS, logits, -jnp.inf)
    m0 = jnp.max(logits, axis=-1, keepdims=True)
    e0 = jnp.min(jnp.where(logits == m0, lane_f, float(LANES)), axis=-1, keepdims=True)
    rest = jnp.where(lane_f == e0, -jnp.inf, logits)
    m1 = jnp.max(rest, axis=-1, keepdims=True)
    e1 = jnp.min(jnp.where(rest == m1, lane_f, float(LANES)), axis=-1, keepdims=True)
    p1 = jnp.exp(m1 - m0)
    g0 = 1.0 / (1.0 + p1)
    idx_ref[...] = jnp.where(lane == 0, e0, jnp.where(lane == 1, e1, 0.0)).astype(jnp.int32)
    gates_ref[...] = jnp.where(lane == 0, g0, jnp.where(lane == 1, p1 * g0, 0.0))


def moe_router(x2, g, shift, scale, router_w, seq, tm=256):
    t, d = x2.shape
    tm = min(tm, seq)
    per_b = seq // tm
    bsz = shift.shape[0]
    rw = jnp.zeros((d, LANES), F32).at[:, :N_EXPERTS].set(router_w)
    row = lambda: pl.BlockSpec((None, 1, d), lambda i: (i // per_b, 0, 0))
    return pl.pallas_call(
        _router_kernel,
        grid=(t // tm,),
        in_specs=[pl.BlockSpec((tm, d), lambda i: (i, 0)),
                  pl.BlockSpec((1, d), lambda i: (0, 0)),
                  row(), row(),
                  pl.BlockSpec((d, LANES), lambda i: (0, 0))],
        out_specs=[pl.BlockSpec((tm, d), lambda i: (i, 0)),
                   pl.BlockSpec((tm, LANES), lambda i: (i, 0)),
                   pl.BlockSpec((tm, LANES), lambda i: (i, 0))],
        out_shape=[jax.ShapeDtypeStruct((t, d), F32),
                   jax.ShapeDtypeStruct((t, LANES), jnp.int32),
                   jax.ShapeDtypeStruct((t, LANES), F32)],
        compiler_params=_cparams(("parallel",)),
        name="moe_router",
    )(x2, g.reshape(1, d), shift.reshape(bsz, 1, d), scale.reshape(bsz, 1, d), rw)


def _row_copy(src_ref, dst_ref, sem, src_row, dst_row):
    return pltpu.make_async_copy(src_ref.at[pl.ds(src_row, 1), :],
                                 dst_ref.at[pl.ds(dst_row, 1), :], sem)


def _start_row_gather(src_ref, dst_ref, sem, row_of, n_rows):
    def start(r, _):
        _row_copy(src_ref, dst_ref, sem, row_of(r), r).start()
        return 0

    lax.fori_loop(0, n_rows, start, 0)


def _wait_row_gather(src_ref, dst_ref, sem, n_rows):
    def wait(r, _):
        _row_copy(src_ref, dst_ref, sem, 0, r).wait()
        return 0

    lax.fori_loop(0, n_rows, wait, 0)


def _moe_ffn_kernel(tile_e_ref, tile_ok_ref, buf_t_ref, h_hbm, w1_ref, w3_ref, w2_ref, o_ref,
                    hf_ref, hb_ref, acc_ref, sem):
    i = pl.program_id(0)
    f = pl.program_id(1)
    tm = hf_ref.shape[0]
    ok = tile_ok_ref[i] > 0

    @pl.when(ok & (f == 0))
    def _():
        _start_row_gather(h_hbm, hf_ref, sem, lambda r: buf_t_ref[i * tm + r], tm)
        _wait_row_gather(h_hbm, hf_ref, sem, tm)
        hb_ref[...] = hf_ref[...].astype(hb_ref.dtype)

    @pl.when(ok)
    def _():
        y = _swiglu_chunk(hb_ref[...], w1_ref[...], w3_ref[...], w2_ref[...])

        @pl.when(f == 0)
        def _():
            acc_ref[...] = y

        @pl.when(f > 0)
        def _():
            acc_ref[...] += y

    @pl.when(f == pl.num_programs(1) - 1)
    def _():
        @pl.when(ok)
        def _():
            o_ref[...] = acc_ref[...]

        @pl.when(jnp.logical_not(ok))
        def _():
            o_ref[...] = jnp.zeros_like(o_ref)


def moe_expert_ffn(h2, tile_e, tile_ok, buf_t, w1, w3, w2, tm, tf=512):
    t, d = h2.shape
    ffn = w1.shape[2]
    nf = ffn // tf
    n_tiles = tile_e.shape[0]

    def fsel(i, f, ok):
        return jnp.where(ok[i] > 0, f, nf - 1)

    grid_spec = pltpu.PrefetchScalarGridSpec(
        num_scalar_prefetch=3,
        grid=(n_tiles, nf),
        in_specs=[pl.BlockSpec(memory_space=pl.ANY),
                  pl.BlockSpec((None, d, tf), lambda i, f, te, ok, bt: (te[i], 0, fsel(i, f, ok))),
                  pl.BlockSpec((None, d, tf), lambda i, f, te, ok, bt: (te[i], 0, fsel(i, f, ok))),
                  pl.BlockSpec((None, tf, d), lambda i, f, te, ok, bt: (te[i], fsel(i, f, ok), 0))],
        out_specs=pl.BlockSpec((tm, d), lambda i, f, te, ok, bt: (i, 0)),
        scratch_shapes=[pltpu.VMEM((tm, d), F32), pltpu.VMEM((tm, d), BF16),
                        pltpu.VMEM((tm, d), F32), pltpu.SemaphoreType.DMA(())],
    )
    return pl.pallas_call(
        _moe_ffn_kernel,
        grid_spec=grid_spec,
        out_shape=jax.ShapeDtypeStruct((n_tiles * tm, d), F32),
        compiler_params=_cparams(("arbitrary", "arbitrary")),
        name="moe_expert_ffn",
    )(tile_e, tile_ok, buf_t, h2, w1, w3, w2)


def _moe_combine_kernel(d0_ref, d1_ref, y_hbm, x_ref, gate_ref, gates_ref, o_ref,
                        y0_ref, y1_ref, sem0, sem1):
    i = pl.program_id(0)
    tm = x_ref.shape[0]
    _start_row_gather(y_hbm, y0_ref, sem0, lambda r: d0_ref[i * tm + r], tm)
    _start_row_gather(y_hbm, y1_ref, sem1, lambda r: d1_ref[i * tm + r], tm)
    _wait_row_gather(y_hbm, y0_ref, sem0, tm)
    _wait_row_gather(y_hbm, y1_ref, sem1, tm)
    gates = gates_ref[...]
    y = gates[:, 0:1] * y0_ref[...] + gates[:, 1:2] * y1_ref[...]
    o_ref[...] = x_ref[...] + gate_ref[...] * y


def moe_combine_residual(dest0, dest1, y_buf, x2, gate, gates, seq, tm=256):
    t, d = x2.shape
    tm = min(tm, seq)
    per_b = seq // tm
    bsz = gate.shape[0]
    grid_spec = pltpu.PrefetchScalarGridSpec(
        num_scalar_prefetch=2,
        grid=(t // tm,),
        in_specs=[pl.BlockSpec(memory_space=pl.ANY),
                  pl.BlockSpec((tm, d), lambda i, d0, d1: (i, 0)),
                  pl.BlockSpec((None, 1, d), lambda i, d0, d1: (i // per_b, 0, 0)),
                  pl.BlockSpec((tm, LANES), lambda i, d0, d1: (i, 0))],
        out_specs=pl.BlockSpec((tm, d), lambda i, d0, d1: (i, 0)),
        scratch_shapes=[pltpu.VMEM((tm, d), F32), pltpu.VMEM((tm, d), F32),
                        pltpu.SemaphoreType.DMA(()), pltpu.SemaphoreType.DMA(())],
    )
    return pl.pallas_call(
        _moe_combine_kernel,
        grid_spec=grid_spec,
        out_shape=jax.ShapeDtypeStruct((t, d), F32),
        compiler_params=_cparams(("arbitrary",)),
        name="moe_combine_residual",
    )(dest0, dest1, y_buf, x2, gate.reshape(bsz, 1, d), gates)


def _moe_routing_tables(idx, tm):
    t = idx.shape[0]
    n_slots = t * MOE_TOPK
    n_tiles = n_slots // tm + N_EXPERTS
    flat_e = idx[:, :MOE_TOPK].reshape(-1)
    onehot = (flat_e[:, None] == jnp.arange(N_EXPERTS)[None, :]).astype(jnp.int32)
    counts = jnp.sum(onehot, axis=0)
    rank = jnp.sum((jnp.cumsum(onehot, axis=0) - onehot) * onehot, axis=1)
    padded = (counts + tm - 1) // tm * tm
    pad_ends = jnp.cumsum(padded)
    pad_starts = pad_ends - padded
    dest = (pad_starts[flat_e] + rank).astype(jnp.int32)
    buf_t = jnp.zeros((n_tiles * tm,), jnp.int32).at[dest].set(
        (jnp.arange(n_slots) // MOE_TOPK).astype(jnp.int32))
    tile_start = jnp.arange(n_tiles) * tm
    tile_e = jnp.minimum(jnp.searchsorted(pad_ends, tile_start, side="right"),
                         N_EXPERTS - 1).astype(jnp.int32)
    tile_ok = (tile_start < pad_ends[-1]).astype(jnp.int32)
    dest2 = dest.reshape(t, MOE_TOPK)
    return tile_e, tile_ok, buf_t, dest2[:, 0], dest2[:, 1]


def moe_ffn_residual(x2, g, shift, scale, gate, router_w, w1, w3, w2, seq, tile_rows=512):
    h2, idx, gates = moe_router(x2, g, shift, scale, router_w, seq)
    tile_e, tile_ok, buf_t, dest0, dest1 = _moe_routing_tables(idx, tile_rows)
    y_buf = moe_expert_ffn(h2, tile_e, tile_ok, buf_t, w1, w3, w2, tile_rows)
    return moe_combine_residual(dest0, dest1, y_buf, x2, gate, gates, seq)


def _pad_w_in(w_in):
    d = w_in.shape[0]
    kpe_end = MLA_Q_RANK + MLA_KV_RANK + MLA_ROPE_DIM
    tail = PROJ_WIDTH - (COL_MOBA + 3 * GROUP_WIDTH)
    return jnp.concatenate([
        w_in[:, :kpe_end], jnp.zeros((d, LANES - MLA_ROPE_DIM), w_in.dtype),
        w_in[:, kpe_end:], jnp.zeros((d, tail), w_in.dtype)], axis=1).astype(BF16)


def _pad_heads_192(a):
    lead = a.shape[:-1]
    a = a.reshape(lead + (GROUP_HEADS, MLA_QK_DIM))
    a = jnp.pad(a, [(0, 0)] * len(lead) + [(0, 0), (0, MLA_QK_PAD - MLA_QK_DIM)])
    return a.reshape(lead + (GROUP_HEADS * MLA_QK_PAD,))


def kernel(x, c, positions, ada_w, ada_b, norm_mix_g, norm_ffn_g, w_in, mla_q_norm_g, mla_kv_norm_g, mla_w_uq, mla_w_ukv, mla_q_head_g, mla_k_head_g, moba_q_head_g, moba_k_head_g, group_norm_g, w_out, ffn_w1, ffn_w3, ffn_w2, router_w, moe_w1, moe_w3, moe_w2):
    bsz, seq, d = x.shape
    depth = ada_w.shape[0]
    cos_pe, sin_pe = _rope_tables(positions, MLA_ROPE_DIM)
    cos_full, sin_full = _rope_tables(positions, HEAD_DIM)
    mod = ada_modulation(c, ada_w, ada_b)
    x2 = x.reshape(bsz * seq, d)
    for l in range(depth):
        shift_m, scale_m, gate_m, shift_f, scale_f, gate_f = jnp.split(mod[l], 6, axis=-1)
        proj = norm_mod_proj(x2, norm_mix_g[l], shift_m, scale_m, _pad_w_in(w_in[l]), seq)
        proj3 = proj.reshape(bsz, seq, PROJ_WIDTH)
        pad_g = lambda g: jnp.pad(g, (0, MLA_QK_PAD - MLA_QK_DIM)).reshape(1, MLA_QK_PAD)
        q_mla, k_mla, v_mla = mla_prep(
            proj3, mla_q_norm_g[l], mla_kv_norm_g[l],
            _pad_heads_192(mla_w_uq[l]).astype(BF16), mla_w_ukv[l].astype(BF16),
            pad_g(mla_q_head_g[l]), pad_g(mla_k_head_g[l]), cos_pe, sin_pe)
        y_mla = mla_attention(q_mla, k_mla, v_mla)
        y_ret = retention(proj3, cos_full, sin_full)
        y_sb = sb_attention(proj3)
        q_mb, k_mb, kmean = moba_prep(proj3, moba_q_head_g[l], moba_k_head_g[l], cos_full, sin_full)
        y_moba = moba_attention(q_mb, k_mb, kmean, proj3)
        x2 = out_proj_residual((y_mla, y_ret, y_sb, y_moba), group_norm_g[l],
                               w_out[l].astype(BF16), x2, gate_m, seq)
        j = l // 2
        if l % 2 == 0:
            x2 = dense_ffn_residual(x2, norm_ffn_g[l], shift_f, scale_f, gate_f,
                                    ffn_w1[j].astype(BF16), ffn_w3[j].astype(BF16),
                                    ffn_w2[j].astype(BF16), seq)
        else:
            x2 = moe_ffn_residual(x2, norm_ffn_g[l], shift_f, scale_f, gate_f, router_w[j],
                                  moe_w1[j].astype(BF16), moe_w3[j].astype(BF16),
                                  moe_w2[j].astype(BF16), seq)
    return x2.reshape(bsz, seq, d)
```

```python
import functools

import jax
import jax.numpy as jnp
from jax import lax
from jax.experimental import pallas as pl
from jax.experimental.pallas import tpu as pltpu

F32 = jnp.float32
BF16 = jnp.bfloat16

HEAD_DIM = 128
GROUP_HEADS = 4
GROUP_WIDTH = 512
MLA_Q_RANK = 512
MLA_KV_RANK = 256
MLA_NOPE_DIM = 128
MLA_ROPE_DIM = 64
MLA_QK_DIM = MLA_NOPE_DIM + MLA_ROPE_DIM
MLA_QK_PAD = 256
RET_CHUNK = 128
MOBA_BLOCK = 256
MOBA_TOPK = 3
ROPE_THETA = 10000.0
NORM_EPS = 1e-6
NEG = -1e30
N_EXPERTS = 8
MOE_TOPK = 2

LANES = 128
ATTN_BLOCK = 256
VMEM_LIMIT = 56 * 1024 * 1024

COL_CQ = 0
COL_CKV = 512
COL_KPE = 768
COL_RET = 896
COL_SB = COL_RET + 4 * GROUP_WIDTH
COL_MOBA = COL_SB + 3 * GROUP_WIDTH
PROJ_WIDTH = 6144


def _cparams(sem):
    return pltpu.CompilerParams(dimension_semantics=sem, vmem_limit_bytes=VMEM_LIMIT)


def _dot(a, b):
    return jnp.dot(a, b, preferred_element_type=F32)


def _dot_nt(a, b):
    return lax.dot_general(a, b, (((1,), (1,)), ((), ())), preferred_element_type=F32)


def _sigmoid(x):
    return 1.0 / (1.0 + jnp.exp(-x))


def _softplus(z):
    return jnp.maximum(z, 0.0) + jnp.log(1.0 + jnp.exp(-jnp.abs(z)))


def _rms(xf, width=None):
    width = xf.shape[-1] if width is None else width
    ss = jnp.sum(xf * xf, axis=-1, keepdims=True) * (1.0 / width)
    return xf * lax.rsqrt(ss + NORM_EPS)


def _ada_kernel(c_ref, w_ref, b_ref, o_ref):
    c = c_ref[...]
    cond = c * _sigmoid(c)
    o_ref[...] = _dot(cond.astype(BF16), w_ref[...].astype(BF16)) + b_ref[...]


def ada_modulation(c, ada_w, ada_b, tn=1024):
    depth, d, n = ada_w.shape
    b = c.shape[0]
    return pl.pallas_call(
        _ada_kernel,
        grid=(depth, n // tn),
        in_specs=[
            pl.BlockSpec((b, d), lambda l, j: (0, 0)),
            pl.BlockSpec((None, d, tn), lambda l, j: (l, 0, j)),
            pl.BlockSpec((None, 1, tn), lambda l, j: (l, 0, j)),
        ],
        out_specs=pl.BlockSpec((None, b, tn), lambda l, j: (l, 0, j)),
        out_shape=jax.ShapeDtypeStruct((depth, b, n), F32),
        compiler_params=_cparams(("parallel", "parallel")),
        name="ada_modulation",
    )(c, ada_w, ada_b.reshape(depth, 1, n))


def _norm_mod_rows(x_ref, g_ref, sh_ref, sc_ref, dst_ref, rows=256):
    tm = x_ref.shape[0]
    g = g_ref[...]
    mul = 1.0 + sc_ref[...]
    sh = sh_ref[...]

    def body(r, _):
        sl = pl.ds(pl.multiple_of(r * rows, rows), rows)
        y = _rms(x_ref[sl, :]) * g
        dst_ref[sl, :] = (y * mul + sh).astype(dst_ref.dtype)
        return 0

    lax.fori_loop(0, tm // rows, body, 0)


def _proj_kernel(x_ref, g_ref, sh_ref, sc_ref, w_ref, o_ref, h_ref):
    @pl.when(pl.program_id(1) == 0)
    def _():
        _norm_mod_rows(x_ref, g_ref, sh_ref, sc_ref, h_ref)

    o_ref[...] = _dot(h_ref[...], w_ref[...]).astype(o_ref.dtype)


def norm_mod_proj(x2, g, shift, scale, w, seq, tm=1024, tn=1024):
    t, d = x2.shape
    n = w.shape[1]
    tm = min(tm, seq)
    per_b = seq // tm
    bsz = shift.shape[0]
    return pl.pallas_call(
        _proj_kernel,
        grid=(t // tm, n // tn),
        in_specs=[
            pl.BlockSpec((tm, d), lambda i, j: (i, 0)),
            pl.BlockSpec((1, d), lambda i, j: (0, 0)),
            pl.BlockSpec((None, 1, d), lambda i, j: (i // per_b, 0, 0)),
            pl.BlockSpec((None, 1, d), lambda i, j: (i // per_b, 0, 0)),
            pl.BlockSpec((d, tn), lambda i, j: (0, j)),
        ],
        out_specs=pl.BlockSpec((tm, tn), lambda i, j: (i, j)),
        out_shape=jax.ShapeDtypeStruct((t, n), BF16),
        scratch_shapes=[pltpu.VMEM((tm, d), BF16)],
        compiler_params=_cparams(("parallel", "arbitrary")),
        name="norm_mod_proj",
    )(x2, g.reshape(1, d), shift.reshape(bsz, 1, d), scale.reshape(bsz, 1, d), w)


def _rope_tables(positions, dim):
    inv_freq = ROPE_THETA ** (-jnp.arange(0, dim, 2, dtype=F32) / dim)
    ang = positions.astype(F32)[:, None] * inv_freq[None, :]
    cos, sin = jnp.cos(ang), jnp.sin(ang)
    pad = jnp.zeros((positions.shape[0], LANES - dim), F32)
    return (jnp.concatenate([cos, cos, pad], axis=-1),
            jnp.concatenate([-sin, sin, pad], axis=-1))


def _rope_full(z, cos2, sin_s):
    return z * cos2 + pltpu.roll(z, 64, 1) * sin_s


def _rope_64(z, cos2, sin_s):
    lane = lax.broadcasted_iota(jnp.int32, z.shape, 1)
    partner = jnp.where(lane < 32, pltpu.roll(z, 96, 1), pltpu.roll(z, 32, 1))
    return z * cos2 + partner * sin_s


def _mla_prep_kernel(p_ref, qg_ref, kvg_ref, wuq_ref, wukv_ref, qhg_ref, khg_ref,
                     cos_ref, sin_ref, q_ref, k_ref, v_ref):
    p = p_ref[...].astype(F32)
    c_q = p[:, COL_CQ:COL_CQ + MLA_Q_RANK]
    c_kv = p[:, COL_CKV:COL_CKV + MLA_KV_RANK]
    k_pe = p[:, COL_KPE:COL_KPE + LANES]
    q = _dot((_rms(c_q) * qg_ref[...]).astype(BF16), wuq_ref[...])
    kv = _dot((_rms(c_kv) * kvg_ref[...]).astype(BF16), wukv_ref[...])
    cos2, sin_s = cos_ref[...], sin_ref[...]
    qhg, khg = qhg_ref[...], khg_ref[...]
    scale = MLA_QK_DIM ** -0.5
    pe_ss = jnp.sum(k_pe * k_pe, axis=-1, keepdims=True)
    for h in range(GROUP_HEADS):
        qh = q[:, h * MLA_QK_PAD:(h + 1) * MLA_QK_PAD]
        qh = _rms(qh, MLA_QK_DIM) * qhg * scale
        q_ref[h, :, :LANES] = qh[:, :LANES].astype(q_ref.dtype)
        q_ref[h, :, LANES:] = _rope_64(qh[:, LANES:], cos2, sin_s).astype(q_ref.dtype)
        k_nope = kv[:, 2 * h * LANES:(2 * h + 1) * LANES]
        ss = (jnp.sum(k_nope * k_nope, axis=-1, keepdims=True) + pe_ss) * (1.0 / MLA_QK_DIM)
        r = lax.rsqrt(ss + NORM_EPS)
        k_ref[h, :, :LANES] = (k_nope * r * khg[:, :LANES]).astype(k_ref.dtype)
        k_ref[h, :, LANES:] = _rope_64(k_pe * r * khg[:, LANES:], cos2, sin_s).astype(k_ref.dtype)
        v_ref[h] = kv[:, (2 * h + 1) * LANES:(2 * h + 2) * LANES].astype(v_ref.dtype)


def mla_prep(proj3, q_norm_g, kv_norm_g, w_uq_pad, w_ukv, q_head_g_pad, k_head_g_pad,
             cos_pe, sin_pe, tm=256):
    bsz, seq, _ = proj3.shape
    h = GROUP_HEADS
    n_in = COL_KPE + LANES
    const = lambda shape: pl.BlockSpec(shape, lambda b, i: (0,) * len(shape))
    return pl.pallas_call(
        _mla_prep_kernel,
        grid=(bsz, seq // tm),
        in_specs=[
            pl.BlockSpec((None, tm, n_in), lambda b, i: (b, i, 0)),
            const((1, MLA_Q_RANK)), const((1, MLA_KV_RANK)),
            const((MLA_Q_RANK, h * MLA_QK_PAD)), const((MLA_KV_RANK, h * 2 * LANES)),
            const((1, MLA_QK_PAD)), const((1, MLA_QK_PAD)),
            pl.BlockSpec((tm, LANES), lambda b, i: (i, 0)),
            pl.BlockSpec((tm, LANES), lambda b, i: (i, 0)),
        ],
        out_specs=[
            pl.BlockSpec((None, h, tm, MLA_QK_PAD), lambda b, i: (b, 0, i, 0)),
            pl.BlockSpec((None, h, tm, MLA_QK_PAD), lambda b, i: (b, 0, i, 0)),
            pl.BlockSpec((None, h, tm, LANES), lambda b, i: (b, 0, i, 0)),
        ],
        out_shape=[
            jax.ShapeDtypeStruct((bsz, h, seq, MLA_QK_PAD), BF16),
            jax.ShapeDtypeStruct((bsz, h, seq, MLA_QK_PAD), BF16),
            jax.ShapeDtypeStruct((bsz, h, seq, LANES), BF16),
        ],
        compiler_params=_cparams(("parallel", "parallel")),
        name="mla_prep",
    )(proj3, q_norm_g.reshape(1, -1), kv_norm_g.reshape(1, -1), w_uq_pad, w_ukv,
      q_head_g_pad, k_head_g_pad, cos_pe, sin_pe)


def _moba_block_choice(qf, kmean_ref, n_past):
    tq = qf.shape[0]
    if n_past <= MOBA_TOPK:
        return [jnp.ones((tq, 1), F32)] * n_past
    gate = [jnp.sum(qf * kmean_ref[n:n + 1, :], axis=-1, keepdims=True) for n in range(n_past)]
    sel = []
    for n in range(n_past):
        rank = jnp.zeros((tq, 1), jnp.int32)
        for o in range(n_past):
            if o != n:
                tie = 1 if o < n else 0
                rank = rank + jnp.where(gate[o] > gate[n], 1,
                                        jnp.where(gate[o] == gate[n], tie, 0))
        sel.append(jnp.where(rank < MOBA_TOPK, 1.0, 0.0))
    return sel


def _attn_kernel(*refs, moba):
    if moba:
        q_ref, k_ref, v_ref, kmean_ref, o_ref = refs
    else:
        q_ref, k_ref, v_ref, o_ref = refs
    seq = q_ref.shape[0]
    t = ATTN_BLOCK
    row = lax.broadcasted_iota(jnp.int32, (t, t), 0)
    col = lax.broadcasted_iota(jnp.int32, (t, t), 1)
    causal = col <= row
    for i in range(seq // t):
        own = slice(i * t, (i + 1) * t)
        q = q_ref[own, :]
        s_own = jnp.where(causal, _dot_nt(q, k_ref[own, :]), NEG)
        m = jnp.max(s_own, axis=-1, keepdims=True)
        s_past = []
        if i > 0:
            s_all = _dot_nt(q, k_ref[:i * t, :])
            if moba:
                sel = _moba_block_choice(q.astype(F32), kmean_ref, i)
                s_past = [jnp.where(sel[n] > 0.5, s_all[:, n * t:(n + 1) * t], NEG)
                          for n in range(i)]
            else:
                s_past = [s_all]
            for s in s_past:
                m = jnp.maximum(m, jnp.max(s, axis=-1, keepdims=True))
        p = jnp.exp(s_own - m)
        l = jnp.sum(p, axis=-1, keepdims=True)
        acc = _dot(p.astype(v_ref.dtype), v_ref[own, :])
        start = 0
        for s in s_past:
            p = jnp.exp(s - m)
            l = l + jnp.sum(p, axis=-1, keepdims=True)
            acc = acc + _dot(p.astype(v_ref.dtype), v_ref[start:start + s.shape[1], :])
            start += s.shape[1]
        o_ref[own, :] = (acc / l).astype(o_ref.dtype)


def _head_spec(seq, d):
    return pl.BlockSpec((None, None, seq, d), lambda b, hh: (b, hh, 0, 0))


def _proj_col_spec(seq, col):
    cb = col // LANES
    return pl.BlockSpec((None, seq, LANES), lambda b, hh: (b, 0, cb + hh))


def _attention(q, k, v, v_spec, kmean=None):
    bsz, h, seq, dk = q.shape
    in_specs = [_head_spec(seq, dk), _head_spec(seq, dk), v_spec]
    args = [q, k, v]
    if kmean is not None:
        in_specs.append(_head_spec(kmean.shape[2], LANES))
        args.append(kmean)
    return pl.pallas_call(
        functools.partial(_attn_kernel, moba=kmean is not None),
        grid=(bsz, h),
        in_specs=in_specs,
        out_specs=pl.BlockSpec((None, seq, LANES), lambda b, hh: (b, 0, hh)),
        out_shape=jax.ShapeDtypeStruct((bsz, seq, h * LANES), BF16),
        compiler_params=_cparams(("parallel", "parallel")),
        name="moba_attention" if kmean is not None else "mla_attention",
    )(*args)


def mla_attention(q, k, v):
    return _attention(q, k, v, _head_spec(q.shape[2], LANES))


def _moba_prep_kernel(q_ref, k_ref, qg_ref, kg_ref, cos_ref, sin_ref, qo_ref, ko_ref, km_ref):
    seq = q_ref.shape[0]
    scale = HEAD_DIM ** -0.5
    for blk in range(seq // MOBA_BLOCK):
        sl = slice(blk * MOBA_BLOCK, (blk + 1) * MOBA_BLOCK)
        cos2, sin_s = cos_ref[sl, :], sin_ref[sl, :]
        qn = _rms(q_ref[sl, :].astype(F32)) * qg_ref[...]
        qo_ref[sl, :] = (_rope_full(qn, cos2, sin_s) * scale).astype(qo_ref.dtype)
        kn = _rope_full(_rms(k_ref[sl, :].astype(F32)) * kg_ref[...], cos2, sin_s)
        ko_ref[sl, :] = kn.astype(ko_ref.dtype)
        km_ref[blk:blk + 1, :] = jnp.mean(kn, axis=0, keepdims=True)


def moba_prep(proj3, q_head_g, k_head_g, cos_full, sin_full):
    bsz, seq, _ = proj3.shape
    h = GROUP_HEADS
    nb = seq // MOBA_BLOCK
    const = lambda shape: pl.BlockSpec(shape, lambda b, hh: (0,) * len(shape))
    return pl.pallas_call(
        _moba_prep_kernel,
        grid=(bsz, h),
        in_specs=[
            _proj_col_spec(seq, COL_MOBA),
            _proj_col_spec(seq, COL_MOBA + GROUP_WIDTH),
            const((1, LANES)), const((1, LANES)),
            const((seq, LANES)), const((seq, LANES)),
        ],
        out_specs=[_head_spec(seq, LANES), _head_spec(seq, LANES), _head_spec(nb, LANES)],
        out_shape=[
            jax.ShapeDtypeStruct((bsz, h, seq, LANES), BF16),
            jax.ShapeDtypeStruct((bsz, h, seq, LANES), BF16),
            jax.ShapeDtypeStruct((bsz, h, nb, LANES), F32),
        ],
        compiler_params=_cparams(("parallel", "parallel")),
        name="moba_prep",
    )(proj3, proj3, q_head_g.reshape(1, -1), k_head_g.reshape(1, -1), cos_full, sin_full)


def moba_attention(q, k, kmean, proj3):
    v_spec = _proj_col_spec(q.shape[2], COL_MOBA + 2 * GROUP_WIDTH)
    return _attention(q, k, proj3, v_spec, kmean=kmean)


def _sb_kernel(q_ref, k_ref, v_ref, o_ref):
    seq = q_ref.shape[0]
    t = ATTN_BLOCK
    scale = HEAD_DIM ** -0.5
    row = lax.broadcasted_iota(jnp.int32, (t, t), 0)
    col = lax.broadcasted_iota(jnp.int32, (t, t), 1)
    strict = col < row
    later = jnp.where(row > col, 1.0, 0.0).astype(BF16)
    for i in range(seq // t):
        q = q_ref[i * t:(i + 1) * t, :]
        z_all = _dot_nt(q, k_ref[:(i + 1) * t, :]) * scale
        tail = jnp.zeros((t, 1), F32)
        acc = jnp.zeros((t, v_ref.shape[1]), F32)
        for n in range(i, -1, -1):
            z = z_all[:, n * t:(n + 1) * t]
            log_1m = -_softplus(z)
            if n == i:
                log_1m = jnp.where(strict, log_1m, 0.0)
            hi = log_1m.astype(BF16)
            lo = (log_1m - hi.astype(F32)).astype(BF16)
            suffix = _dot(hi, later) + _dot(lo, later)
            a = jnp.exp(z + log_1m + suffix + tail)
            if n == i:
                a = jnp.where(strict, a, 0.0)
            acc = acc + _dot(a.astype(v_ref.dtype), v_ref[n * t:(n + 1) * t, :])
            tail = tail + jnp.sum(log_1m, axis=-1, keepdims=True)
        o_ref[i * t:(i + 1) * t, :] = acc.astype(o_ref.dtype)


def sb_attention(proj3):
    bsz, seq, _ = proj3.shape
    return pl.pallas_call(
        _sb_kernel,
        grid=(bsz, GROUP_HEADS),
        in_specs=[
            _proj_col_spec(seq, COL_SB),
            _proj_col_spec(seq, COL_SB + GROUP_WIDTH),
            _proj_col_spec(seq, COL_SB + 2 * GROUP_WIDTH),
        ],
        out_specs=pl.BlockSpec((None, seq, LANES), lambda b, hh: (b, 0, hh)),
        out_shape=jax.ShapeDtypeStruct((bsz, seq, GROUP_WIDTH), BF16),
        compiler_params=_cparams(("parallel", "parallel")),
        name="sb_attention",
    )(proj3, proj3, proj3)


def _ret_kernel(q_ref, k_ref, v_ref, g_ref, cos_ref, sin_ref, lg_ref, o_ref):
    seq = q_ref.shape[0]
    c = RET_CHUNK
    log_gamma = lg_ref[...]
    ri = lax.broadcasted_iota(jnp.int32, (c, c), 0).astype(F32)
    ci = lax.broadcasted_iota(jnp.int32, (c, c), 1).astype(F32)
    rel = ri - ci
    intra_decay = jnp.where(rel >= 0, jnp.exp(jnp.maximum(rel, 0.0) * log_gamma), 0.0)
    idx = lax.broadcasted_iota(jnp.int32, (c, 1), 0).astype(F32)
    query_decay = jnp.exp((idx + 1.0) * log_gamma)
    key_decay = jnp.exp((c - 1.0 - idx) * log_gamma)
    chunk_decay = jnp.exp(c * log_gamma)
    k_scale = HEAD_DIM ** -0.5

    state = jnp.zeros((HEAD_DIM, HEAD_DIM), F32)
    for n in range(seq // c):
        sl = slice(n * c, (n + 1) * c)
        cos2, sin_s = cos_ref[sl, :], sin_ref[sl, :]
        q = _rope_full(q_ref[sl, :].astype(F32), cos2, sin_s)
        k = _rope_full(k_ref[sl, :].astype(F32), cos2, sin_s) * k_scale
        vb = v_ref[sl, :]
        qb = q.astype(BF16)
        scores = _dot_nt(qb, k.astype(BF16)) * intra_decay
        y = _dot(scores.astype(BF16), vb)
        y = y + _dot(qb, state.astype(BF16)) * query_decay
        kd_t = jnp.transpose(k * key_decay).astype(BF16)
        state = state * chunk_decay + _dot(kd_t, vb)
        mu = jnp.mean(y, axis=-1, keepdims=True)
        yc = y - mu
        var = jnp.mean(yc * yc, axis=-1, keepdims=True)
        yn = yc * lax.rsqrt(var + NORM_EPS)
        g = g_ref[sl, :].astype(F32)
        o_ref[sl, :] = (g * _sigmoid(g) * yn).astype(o_ref.dtype)


def retention(proj3, cos_full, sin_full):
    bsz, seq, _ = proj3.shape
    h = GROUP_HEADS
    log_gamma = jnp.log(1.0 - 2.0 ** (-5.0 - jnp.arange(h, dtype=F32))).reshape(h, 1, 1)
    col = lambda k: pl.BlockSpec((None, seq, LANES),
                                 lambda b, hh: (b, 0, (COL_RET + k * GROUP_WIDTH) // LANES + hh))
    const = lambda shape: pl.BlockSpec(shape, lambda b, hh: (0,) * len(shape))
    return pl.pallas_call(
        _ret_kernel,
        grid=(bsz, h),
        in_specs=[col(0), col(1), col(2), col(3), const((seq, LANES)), const((seq, LANES)),
                  pl.BlockSpec((None, 1, 1), lambda b, hh: (hh, 0, 0))],
        out_specs=pl.BlockSpec((None, seq, LANES), lambda b, hh: (b, 0, hh)),
        out_shape=jax.ShapeDtypeStruct((bsz, seq, GROUP_WIDTH), BF16),
        compiler_params=_cparams(("parallel", "parallel")),
        name="retention",
    )(proj3, proj3, proj3, proj3, cos_full, sin_full, log_gamma)


def _out_proj_kernel(y0_ref, y1_ref, y2_ref, y3_ref, gg_ref, w_ref, x_ref, gate_ref, o_ref, h_ref):
    @pl.when(pl.program_id(1) == 0)
    def _():
        for grp, y_ref in enumerate((y0_ref, y1_ref, y2_ref, y3_ref)):
            yn = _rms(y_ref[...].astype(F32)) * gg_ref[grp:grp + 1, :]
            h_ref[:, grp * GROUP_WIDTH:(grp + 1) * GROUP_WIDTH] = yn.astype(h_ref.dtype)

    o_ref[...] = x_ref[...] + gate_ref[...] * _dot(h_ref[...], w_ref[...])


def out_proj_residual(ys, group_g, w_out, x2, gate, seq, tm=512, tn=1024):
    t, d = x2.shape
    tm = min(tm, seq)
    per_b = seq // tm
    bsz = gate.shape[0]
    y_spec = pl.BlockSpec((tm, GROUP_WIDTH), lambda i, j: (i, 0))
    return pl.pallas_call(
        _out_proj_kernel,
        grid=(t // tm, d // tn),
        in_specs=[y_spec, y_spec, y_spec, y_spec,
                  pl.BlockSpec((4, GROUP_WIDTH), lambda i, j: (0, 0)),
                  pl.BlockSpec((4 * GROUP_WIDTH, tn), lambda i, j: (0, j)),
                  pl.BlockSpec((tm, tn), lambda i, j: (i, j)),
                  pl.BlockSpec((None, 1, tn), lambda i, j: (i // per_b, 0, j))],
        out_specs=pl.BlockSpec((tm, tn), lambda i, j: (i, j)),
        out_shape=jax.ShapeDtypeStruct((t, d), F32),
        scratch_shapes=[pltpu.VMEM((tm, 4 * GROUP_WIDTH), BF16)],
        compiler_params=_cparams(("parallel", "arbitrary")),
        name="out_proj_residual",
    )(*[y.reshape(t, GROUP_WIDTH) for y in ys], group_g, w_out, x2, gate.reshape(bsz, 1, d))


def _swiglu_chunk(h, w1, w3, w2):
    a = _dot(h, w1)
    a = a * _sigmoid(a) * _dot(h, w3)
    return _dot(a.astype(BF16), w2)


def _ffn_kernel(x_ref, g_ref, sh_ref, sc_ref, gate_ref, w1_ref, w3_ref, w2_ref, o_ref, h_ref):
    f = pl.program_id(1)

    @pl.when(f == 0)
    def _():
        _norm_mod_rows(x_ref, g_ref, sh_ref, sc_ref, h_ref)

    y = _swiglu_chunk(h_ref[...], w1_ref[...], w3_ref[...], w2_ref[...])

    @pl.when(f == 0)
    def _():
        o_ref[...] = y

    @pl.when(f > 0)
    def _():
        o_ref[...] += y

    @pl.when(f == pl.num_programs(1) - 1)
    def _():
        o_ref[...] = x_ref[...] + gate_ref[...] * o_ref[...]


def dense_ffn_residual(x2, g, shift, scale, gate, w1, w3, w2, seq, tm=1024, tf=512):
    t, d = x2.shape
    ffn = w1.shape[1]
    tm = min(tm, seq)
    per_b = seq // tm
    bsz = gate.shape[0]
    row = lambda: pl.BlockSpec((None, 1, d), lambda i, f: (i // per_b, 0, 0))
    return pl.pallas_call(
        _ffn_kernel,
        grid=(t // tm, ffn // tf),
        in_specs=[pl.BlockSpec((tm, d), lambda i, f: (i, 0), pipeline_mode=pl.Buffered(1)),
                  pl.BlockSpec((1, d), lambda i, f: (0, 0)),
                  row(), row(), row(),
                  pl.BlockSpec((d, tf), lambda i, f: (0, f)),
                  pl.BlockSpec((d, tf), lambda i, f: (0, f)),
                  pl.BlockSpec((tf, d), lambda i, f: (f, 0))],
        out_specs=pl.BlockSpec((tm, d), lambda i, f: (i, 0)),
        out_shape=jax.ShapeDtypeStruct((t, d), F32),
        scratch_shapes=[pltpu.VMEM((tm, d), BF16)],
        compiler_params=_cparams(("parallel", "arbitrary")),
        name="dense_ffn_residual",
    )(x2, g.reshape(1, d), shift.reshape(bsz, 1, d), scale.reshape(bsz, 1, d),
      gate.reshape(bsz, 1, d), w1, w3, w2)


def _router_kernel(x_ref, g_ref, sh_ref, sc_ref, rw_ref, h_ref, idx_ref, gates_ref):
    _norm_mod_rows(x_ref, g_ref, sh_ref, sc_ref, h_ref)
    logits = jnp.dot(h_ref[...], rw_ref[...], precision=lax.Precision.HIGHEST,
                     preferred_element_type=F32)
    lane = lax.broadcasted_iota(jnp.int32, logits.shape, 1)
    lane_f = lane.astype(F32)
    logits = jnp.where(lane < N_EXPERTS, logits, -jnp.inf)
    m0 = jnp.max(logits, axis=-1, keepdims=True)
    e0 = jnp.min(jnp.where(logits == m0, lane_f, float(LANES)), axis=-1, keepdims=True)
    rest = jnp.where(lane_f == e0, -jnp.inf, logits)
    m1 = jnp.max(rest, axis=-1, keepdims=True)
    e1 = jnp.min(jnp.where(rest == m1, lane_f, float(LANES)), axis=-1, keepdims=True)
    p1 = jnp.exp(m1 - m0)
    g0 = 1.0 / (1.0 + p1)
    idx_ref[...] = jnp.where(lane == 0, e0, jnp.where(lane == 1, e1, 0.0)).astype(jnp.int32)
    gates_ref[...] = jnp.where(lane == 0, g0, jnp.where(lane == 1, p1 * g0, 0.0))


def moe_router(x2, g, shift, scale, router_w, seq, tm=256):
    t, d = x2.shape
    tm = min(tm, seq)
    per_b = seq // tm
    bsz = shift.shape[0]
    rw = jnp.zeros((d, LANES), F32).at[:, :N_EXPERTS].set(router_w)
    row = lambda: pl.BlockSpec((None, 1, d), lambda i: (i // per_b, 0, 0))
    return pl.pallas_call(
        _router_kernel,
        grid=(t // tm,),
        in_specs=[pl.BlockSpec((tm, d), lambda i: (i, 0)),
                  pl.BlockSpec((1, d), lambda i: (0, 0)),
                  row(), row(),
                  pl.BlockSpec((d, LANES), lambda i: (0, 0))],
        out_specs=[pl.BlockSpec((tm, d), lambda i: (i, 0)),
                   pl.BlockSpec((tm, LANES), lambda i: (i, 0)),
                   pl.BlockSpec((tm, LANES), lambda i: (i, 0))],
        out_shape=[jax.ShapeDtypeStruct((t, d), F32),
                   jax.ShapeDtypeStruct((t, LANES), jnp.int32),
                   jax.ShapeDtypeStruct((t, LANES), F32)],
        compiler_params=_cparams(("parallel",)),
        name="moe_router",
    )(x2, g.reshape(1, d), shift.reshape(bsz, 1, d), scale.reshape(bsz, 1, d), rw)


def _row_copy(src_ref, dst_ref, sem, src_row, dst_row):
    return pltpu.make_async_copy(src_ref.at[pl.ds(src_row, 1), :],
                                 dst_ref.at[pl.ds(dst_row, 1), :], sem)


def _start_row_gather(src_ref, dst_ref, sem, row_of, n_rows):
    def start(r, _):
        _row_copy(src_ref, dst_ref, sem, row_of(r), r).start()
        return 0

    lax.fori_loop(0, n_rows, start, 0)


def _wait_row_gather(src_ref, dst_ref, sem, n_rows):
    def wait(r, _):
        _row_copy(src_ref, dst_ref, sem, 0, r).wait()
        return 0

    lax.fori_loop(0, n_rows, wait, 0)


def _moe_ffn_kernel(tile_e_ref, tile_ok_ref, buf_t_ref, h_hbm, w1_ref, w3_ref, w2_ref, o_ref,
                    hf_ref, hb_ref, sems):
    i = pl.program_id(0)
    f = pl.program_id(1)
    n_tiles = pl.num_programs(0)
    tm = hb_ref.shape[0]
    ok = tile_ok_ref[i] > 0
    slot = i % 2

    def start_gather(tile, into):
        _start_row_gather(h_hbm, hf_ref.at[into], sems.at[into],
                          lambda r: buf_t_ref[tile * tm + r], tm)

    @pl.when(ok & (f == 0))
    def _():
        @pl.when(i == 0)
        def _():
            start_gather(0, 0)

        _wait_row_gather(h_hbm, hf_ref.at[slot], sems.at[slot], tm)
        hb_ref[...] = hf_ref[slot].astype(hb_ref.dtype)

    nxt = jnp.minimum(i + 1, n_tiles - 1)

    @pl.when((f == 1) & (i + 1 < n_tiles) & (tile_ok_ref[nxt] > 0))
    def _():
        start_gather(i + 1, 1 - slot)

    @pl.when(ok)
    def _():
        y = _swiglu_chunk(hb_ref[...], w1_ref[...], w3_ref[...], w2_ref[...])

        @pl.when(f == 0)
        def _():
            o_ref[...] = y

        @pl.when(f > 0)
        def _():
            o_ref[...] += y

    @pl.when(jnp.logical_not(ok) & (f == pl.num_programs(1) - 1))
    def _():
        o_ref[...] = jnp.zeros_like(o_ref)


def moe_expert_ffn(h2, tile_e, tile_ok, buf_t, w1, w3, w2, tm, tf=512):
    t, d = h2.shape
    ffn = w1.shape[2]
    nf = ffn // tf
    n_tiles = tile_e.shape[0]

    def fsel(i, f, ok):
        return jnp.where(ok[i] > 0, f, nf - 1)

    grid_spec = pltpu.PrefetchScalarGridSpec(
        num_scalar_prefetch=3,
        grid=(n_tiles, nf),
        in_specs=[pl.BlockSpec(memory_space=pl.ANY),
                  pl.BlockSpec((None, d, tf), lambda i, f, te, ok, bt: (te[i], 0, fsel(i, f, ok))),
                  pl.BlockSpec((None, d, tf), lambda i, f, te, ok, bt: (te[i], 0, fsel(i, f, ok))),
                  pl.BlockSpec((None, tf, d), lambda i, f, te, ok, bt: (te[i], fsel(i, f, ok), 0))],
        out_specs=pl.BlockSpec((tm, d), lambda i, f, te, ok, bt: (i, 0)),
        scratch_shapes=[pltpu.VMEM((2, tm, d), F32), pltpu.VMEM((tm, d), BF16),
                        pltpu.SemaphoreType.DMA((2,))],
    )
    return pl.pallas_call(
        _moe_ffn_kernel,
        grid_spec=grid_spec,
        out_shape=jax.ShapeDtypeStruct((n_tiles * tm, d), F32),
        compiler_params=_cparams(("arbitrary", "arbitrary")),
        name="moe_expert_ffn",
    )(tile_e, tile_ok, buf_t, h2, w1, w3, w2)


def _moe_combine_kernel(d0_ref, d1_ref, y_hbm, x_ref, gate_ref, gates_ref, o_ref,
                        y0_ref, y1_ref, sems0, sems1):
    i = pl.program_id(0)
    tm = x_ref.shape[0]
    slot = i % 2

    def start_gather(tile, into):
        _start_row_gather(y_hbm, y0_ref.at[into], sems0.at[into], lambda r: d0_ref[tile * tm + r], tm)
        _start_row_gather(y_hbm, y1_ref.at[into], sems1.at[into], lambda r: d1_ref[tile * tm + r], tm)

    @pl.when(i == 0)
    def _():
        start_gather(0, 0)

    @pl.when(i + 1 < pl.num_programs(0))
    def _():
        start_gather(i + 1, 1 - slot)

    _wait_row_gather(y_hbm, y0_ref.at[slot], sems0.at[slot], tm)
    _wait_row_gather(y_hbm, y1_ref.at[slot], sems1.at[slot], tm)
    gates = gates_ref[...]
    y = gates[:, 0:1] * y0_ref[slot] + gates[:, 1:2] * y1_ref[slot]
    o_ref[...] = x_ref[...] + gate_ref[...] * y


def moe_combine_residual(dest0, dest1, y_buf, x2, gate, gates, seq, tm=256):
    t, d = x2.shape
    tm = min(tm, seq)
    per_b = seq // tm
    bsz = gate.shape[0]
    grid_spec = pltpu.PrefetchScalarGridSpec(
        num_scalar_prefetch=2,
        grid=(t // tm,),
        in_specs=[pl.BlockSpec(memory_space=pl.ANY),
                  pl.BlockSpec((tm, d), lambda i, d0, d1: (i, 0)),
                  pl.BlockSpec((None, 1, d), lambda i, d0, d1: (i // per_b, 0, 0)),
                  pl.BlockSpec((tm, LANES), lambda i, d0, d1: (i, 0))],
        out_specs=pl.BlockSpec((tm, d), lambda i, d0, d1: (i, 0)),
        scratch_shapes=[pltpu.VMEM((2, tm, d), F32), pltpu.VMEM((2, tm, d), F32),
                        pltpu.SemaphoreType.DMA((2,)), pltpu.SemaphoreType.DMA((2,))],
    )
    return pl.pallas_call(
        _moe_combine_kernel,
        grid_spec=grid_spec,
        out_shape=jax.ShapeDtypeStruct((t, d), F32),
        compiler_params=_cparams(("arbitrary",)),
        name="moe_combine_residual",
    )(dest0, dest1, y_buf, x2, gate.reshape(bsz, 1, d), gates)


def _moe_routing_tables(idx, tm):
    t = idx.shape[0]
    n_slots = t * MOE_TOPK
    n_tiles = n_slots // tm + N_EXPERTS
    flat_e = idx[:, :MOE_TOPK].reshape(-1)
    onehot = (flat_e[:, None] == jnp.arange(N_EXPERTS)[None, :]).astype(jnp.int32)
    counts = jnp.sum(onehot, axis=0)
    rank = jnp.sum((jnp.cumsum(onehot, axis=0) - onehot) * onehot, axis=1)
    padded = (counts + tm - 1) // tm * tm
    pad_ends = jnp.cumsum(padded)
    pad_starts = pad_ends - padded
    dest = (pad_starts[flat_e] + rank).astype(jnp.int32)
    buf_t = jnp.zeros((n_tiles * tm,), jnp.int32).at[dest].set(
        (jnp.arange(n_slots) // MOE_TOPK).astype(jnp.int32))
    tile_start = jnp.arange(n_tiles) * tm
    tile_e = jnp.minimum(jnp.searchsorted(pad_ends, tile_start, side="right"),
                         N_EXPERTS - 1).astype(jnp.int32)
    tile_ok = (tile_start < pad_ends[-1]).astype(jnp.int32)
    dest2 = dest.reshape(t, MOE_TOPK)
    return tile_e, tile_ok, buf_t, dest2[:, 0], dest2[:, 1]


def moe_ffn_residual(x2, g, shift, scale, gate, router_w, w1, w3, w2, seq, tile_rows=512):
    h2, idx, gates = moe_router(x2, g, shift, scale, router_w, seq)
    tile_e, tile_ok, buf_t, dest0, dest1 = _moe_routing_tables(idx, tile_rows)
    y_buf = moe_expert_ffn(h2, tile_e, tile_ok, buf_t, w1, w3, w2, tile_rows)
    return moe_combine_residual(dest0, dest1, y_buf, x2, gate, gates, seq)


def _pad_w_in(w_in):
    d = w_in.shape[0]
    kpe_end = MLA_Q_RANK + MLA_KV_RANK + MLA_ROPE_DIM
    tail = PROJ_WIDTH - (COL_MOBA + 3 * GROUP_WIDTH)
    return jnp.concatenate([
        w_in[:, :kpe_end], jnp.zeros((d, LANES - MLA_ROPE_DIM), w_in.dtype),
        w_in[:, kpe_end:], jnp.zeros((d, tail), w_in.dtype)], axis=1).astype(BF16)


def _pad_heads_192(a):
    lead = a.shape[:-1]
    a = a.reshape(lead + (GROUP_HEADS, MLA_QK_DIM))
    a = jnp.pad(a, [(0, 0)] * len(lead) + [(0, 0), (0, MLA_QK_PAD - MLA_QK_DIM)])
    return a.reshape(lead + (GROUP_HEADS * MLA_QK_PAD,))


def kernel(x, c, positions, ada_w, ada_b, norm_mix_g, norm_ffn_g, w_in, mla_q_norm_g, mla_kv_norm_g, mla_w_uq, mla_w_ukv, mla_q_head_g, mla_k_head_g, moba_q_head_g, moba_k_head_g, group_norm_g, w_out, ffn_w1, ffn_w3, ffn_w2, router_w, moe_w1, moe_w3, moe_w2):
    bsz, seq, d = x.shape
    depth = ada_w.shape[0]
    cos_pe, sin_pe = _rope_tables(positions, MLA_ROPE_DIM)
    cos_full, sin_full = _rope_tables(positions, HEAD_DIM)
    mod = ada_modulation(c, ada_w, ada_b)
    x2 = x.reshape(bsz * seq, d)
    for l in range(depth):
        shift_m, scale_m, gate_m, shift_f, scale_f, gate_f = jnp.split(mod[l], 6, axis=-1)
        proj = norm_mod_proj(x2, norm_mix_g[l], shift_m, scale_m, _pad_w_in(w_in[l]), seq)
        proj3 = proj.reshape(bsz, seq, PROJ_WIDTH)
        pad_g = lambda g: jnp.pad(g, (0, MLA_QK_PAD - MLA_QK_DIM)).reshape(1, MLA_QK_PAD)
        q_mla, k_mla, v_mla = mla_prep(
            proj3, mla_q_norm_g[l], mla_kv_norm_g[l],
            _pad_heads_192(mla_w_uq[l]).astype(BF16), mla_w_ukv[l].astype(BF16),
            pad_g(mla_q_head_g[l]), pad_g(mla_k_head_g[l]), cos_pe, sin_pe)
        y_mla = mla_attention(q_mla, k_mla, v_mla)
        y_ret = retention(proj3, cos_full, sin_full)
        y_sb = sb_attention(proj3)
        q_mb, k_mb, kmean = moba_prep(proj3, moba_q_head_g[l], moba_k_head_g[l], cos_full, sin_full)
        y_moba = moba_attention(q_mb, k_mb, kmean, proj3)
        x2 = out_proj_residual((y_mla, y_ret, y_sb, y_moba), group_norm_g[l],
                               w_out[l].astype(BF16), x2, gate_m, seq)
        j = l // 2
        if l % 2 == 0:
            x2 = dense_ffn_residual(x2, norm_ffn_g[l], shift_f, scale_f, gate_f,
                                    ffn_w1[j].astype(BF16), ffn_w3[j].astype(BF16),
                                    ffn_w2[j].astype(BF16), seq)
        else:
            x2 = moe_ffn_residual(x2, norm_ffn_g[l], shift_f, scale_f, gate_f, router_w[j],
                                  moe_w1[j].astype(BF16), moe_w3[j].astype(BF16),
                                  moe_w2[j].astype(BF16), seq)
    return x2.reshape(bsz, seq, d)
```

```python
import functools

import jax
import jax.numpy as jnp
from jax import lax
from jax.experimental import pallas as pl
from jax.experimental.pallas import tpu as pltpu

F32 = jnp.float32
BF16 = jnp.bfloat16

HEAD_DIM = 128
GROUP_HEADS = 4
GROUP_WIDTH = 512
MLA_Q_RANK = 512
MLA_KV_RANK = 256
MLA_NOPE_DIM = 128
MLA_ROPE_DIM = 64
MLA_QK_DIM = MLA_NOPE_DIM + MLA_ROPE_DIM
MLA_QK_PAD = 256
RET_CHUNK = 128
MOBA_BLOCK = 256
MOBA_TOPK = 3
ROPE_THETA = 10000.0
NORM_EPS = 1e-6
NEG = -1e30
N_EXPERTS = 8
MOE_TOPK = 2

LANES = 128
ATTN_BLOCK = 256
VMEM_LIMIT = 56 * 1024 * 1024
VMEM_LIMIT_MOE = 60 * 1024 * 1024

COL_CQ = 0
COL_CKV = 512
COL_KPE = 768
COL_RET = 896
COL_SB = COL_RET + 4 * GROUP_WIDTH
COL_MOBA = COL_SB + 3 * GROUP_WIDTH
PROJ_WIDTH = 6144


def _cparams(sem, vmem=VMEM_LIMIT):
    return pltpu.CompilerParams(dimension_semantics=sem, vmem_limit_bytes=vmem)


def _dot(a, b):
    return jnp.dot(a, b, preferred_element_type=F32)


def _dot_nt(a, b):
    return lax.dot_general(a, b, (((1,), (1,)), ((), ())), preferred_element_type=F32)


def _sigmoid(x):
    return 1.0 / (1.0 + jnp.exp(-x))


def _softplus(z):
    return jnp.maximum(z, 0.0) + jnp.log(1.0 + jnp.exp(-jnp.abs(z)))


def _rms(xf, width=None):
    width = xf.shape[-1] if width is None else width
    ss = jnp.sum(xf * xf, axis=-1, keepdims=True) * (1.0 / width)
    return xf * lax.rsqrt(ss + NORM_EPS)


def _ada_kernel(c_ref, w_ref, b_ref, o_ref):
    c = c_ref[...]
    cond = c * _sigmoid(c)
    o_ref[...] = _dot(cond.astype(BF16), w_ref[...].astype(BF16)) + b_ref[...]


def ada_modulation(c, ada_w, ada_b, tn=1024):
    depth, d, n = ada_w.shape
    b = c.shape[0]
    return pl.pallas_call(
        _ada_kernel,
        grid=(depth, n // tn),
        in_specs=[
            pl.BlockSpec((b, d), lambda l, j: (0, 0)),
            pl.BlockSpec((None, d, tn), lambda l, j: (l, 0, j)),
            pl.BlockSpec((None, 1, tn), lambda l, j: (l, 0, j)),
        ],
        out_specs=pl.BlockSpec((None, b, tn), lambda l, j: (l, 0, j)),
        out_shape=jax.ShapeDtypeStruct((depth, b, n), F32),
        compiler_params=_cparams(("parallel", "parallel")),
        name="ada_modulation",
    )(c, ada_w, ada_b.reshape(depth, 1, n))


def _norm_mod_rows(x_ref, g_ref, sh_ref, sc_ref, dst_ref, rows=256):
    tm = x_ref.shape[0]
    g = g_ref[...]
    mul = 1.0 + sc_ref[...]
    sh = sh_ref[...]

    def body(r, _):
        sl = pl.ds(pl.multiple_of(r * rows, rows), rows)
        y = _rms(x_ref[sl, :]) * g
        dst_ref[sl, :] = (y * mul + sh).astype(dst_ref.dtype)
        return 0

    lax.fori_loop(0, tm // rows, body, 0)


def _proj_kernel(x_ref, g_ref, sh_ref, sc_ref, w_ref, o_ref, h_ref):
    @pl.when(pl.program_id(1) == 0)
    def _():
        _norm_mod_rows(x_ref, g_ref, sh_ref, sc_ref, h_ref)

    o_ref[...] = _dot(h_ref[...], w_ref[...]).astype(o_ref.dtype)


def norm_mod_proj(x2, g, shift, scale, w, seq, tm=1024, tn=1024):
    t, d = x2.shape
    n = w.shape[1]
    tm = min(tm, seq)
    per_b = seq // tm
    bsz = shift.shape[0]
    return pl.pallas_call(
        _proj_kernel,
        grid=(t // tm, n // tn),
        in_specs=[
            pl.BlockSpec((tm, d), lambda i, j: (i, 0)),
            pl.BlockSpec((1, d), lambda i, j: (0, 0)),
            pl.BlockSpec((None, 1, d), lambda i, j: (i // per_b, 0, 0)),
            pl.BlockSpec((None, 1, d), lambda i, j: (i // per_b, 0, 0)),
            pl.BlockSpec((d, tn), lambda i, j: (0, j)),
        ],
        out_specs=pl.BlockSpec((tm, tn), lambda i, j: (i, j)),
        out_shape=jax.ShapeDtypeStruct((t, n), BF16),
        scratch_shapes=[pltpu.VMEM((tm, d), BF16)],
        compiler_params=_cparams(("parallel", "arbitrary")),
        name="norm_mod_proj",
    )(x2, g.reshape(1, d), shift.reshape(bsz, 1, d), scale.reshape(bsz, 1, d), w)


def _rope_tables(positions, dim):
    inv_freq = ROPE_THETA ** (-jnp.arange(0, dim, 2, dtype=F32) / dim)
    ang = positions.astype(F32)[:, None] * inv_freq[None, :]
    cos, sin = jnp.cos(ang), jnp.sin(ang)
    pad = jnp.zeros((positions.shape[0], LANES - dim), F32)
    return (jnp.concatenate([cos, cos, pad], axis=-1),
            jnp.concatenate([-sin, sin, pad], axis=-1))


def _rope_full(z, cos2, sin_s):
    return z * cos2 + pltpu.roll(z, 64, 1) * sin_s


def _rope_64(z, cos2, sin_s):
    lane = lax.broadcasted_iota(jnp.int32, z.shape, 1)
    partner = jnp.where(lane < 32, pltpu.roll(z, 96, 1), pltpu.roll(z, 32, 1))
    return z * cos2 + partner * sin_s


def _mla_prep_kernel(p_ref, qg_ref, kvg_ref, wuq_ref, wukv_ref, qhg_ref, khg_ref,
                     cos_ref, sin_ref, q_ref, k_ref, v_ref):
    p = p_ref[...].astype(F32)
    c_q = p[:, COL_CQ:COL_CQ + MLA_Q_RANK]
    c_kv = p[:, COL_CKV:COL_CKV + MLA_KV_RANK]
    k_pe = p[:, COL_KPE:COL_KPE + LANES]
    q = _dot((_rms(c_q) * qg_ref[...]).astype(BF16), wuq_ref[...])
    kv = _dot((_rms(c_kv) * kvg_ref[...]).astype(BF16), wukv_ref[...])
    cos2, sin_s = cos_ref[...], sin_ref[...]
    qhg, khg = qhg_ref[...], khg_ref[...]
    scale = MLA_QK_DIM ** -0.5
    pe_ss = jnp.sum(k_pe * k_pe, axis=-1, keepdims=True)
    for h in range(GROUP_HEADS):
        qh = q[:, h * MLA_QK_PAD:(h + 1) * MLA_QK_PAD]
        qh = _rms(qh, MLA_QK_DIM) * qhg * scale
        q_ref[h, :, :LANES] = qh[:, :LANES].astype(q_ref.dtype)
        q_ref[h, :, LANES:] = _rope_64(qh[:, LANES:], cos2, sin_s).astype(q_ref.dtype)
        k_nope = kv[:, 2 * h * LANES:(2 * h + 1) * LANES]
        ss = (jnp.sum(k_nope * k_nope, axis=-1, keepdims=True) + pe_ss) * (1.0 / MLA_QK_DIM)
        r = lax.rsqrt(ss + NORM_EPS)
        k_ref[h, :, :LANES] = (k_nope * r * khg[:, :LANES]).astype(k_ref.dtype)
        k_ref[h, :, LANES:] = _rope_64(k_pe * r * khg[:, LANES:], cos2, sin_s).astype(k_ref.dtype)
        v_ref[h] = kv[:, (2 * h + 1) * LANES:(2 * h + 2) * LANES].astype(v_ref.dtype)


def mla_prep(proj3, q_norm_g, kv_norm_g, w_uq_pad, w_ukv, q_head_g_pad, k_head_g_pad,
             cos_pe, sin_pe, tm=256):
    bsz, seq, _ = proj3.shape
    h = GROUP_HEADS
    n_in = COL_KPE + LANES
    const = lambda shape: pl.BlockSpec(shape, lambda b, i: (0,) * len(shape))
    return pl.pallas_call(
        _mla_prep_kernel,
        grid=(bsz, seq // tm),
        in_specs=[
            pl.BlockSpec((None, tm, n_in), lambda b, i: (b, i, 0)),
            const((1, MLA_Q_RANK)), const((1, MLA_KV_RANK)),
            const((MLA_Q_RANK, h * MLA_QK_PAD)), const((MLA_KV_RANK, h * 2 * LANES)),
            const((1, MLA_QK_PAD)), const((1, MLA_QK_PAD)),
            pl.BlockSpec((tm, LANES), lambda b, i: (i, 0)),
            pl.BlockSpec((tm, LANES), lambda b, i: (i, 0)),
        ],
        out_specs=[
            pl.BlockSpec((None, h, tm, MLA_QK_PAD), lambda b, i: (b, 0, i, 0)),
            pl.BlockSpec((None, h, tm, MLA_QK_PAD), lambda b, i: (b, 0, i, 0)),
            pl.BlockSpec((None, h, tm, LANES), lambda b, i: (b, 0, i, 0)),
        ],
        out_shape=[
            jax.ShapeDtypeStruct((bsz, h, seq, MLA_QK_PAD), BF16),
            jax.ShapeDtypeStruct((bsz, h, seq, MLA_QK_PAD), BF16),
            jax.ShapeDtypeStruct((bsz, h, seq, LANES), BF16),
        ],
        compiler_params=_cparams(("parallel", "parallel")),
        name="mla_prep",
    )(proj3, q_norm_g.reshape(1, -1), kv_norm_g.reshape(1, -1), w_uq_pad, w_ukv,
      q_head_g_pad, k_head_g_pad, cos_pe, sin_pe)


def _moba_block_choice(qf, kmean_ref, n_past):
    tq = qf.shape[0]
    if n_past <= MOBA_TOPK:
        return [jnp.ones((tq, 1), F32)] * n_past
    gate = [jnp.sum(qf * kmean_ref[n:n + 1, :], axis=-1, keepdims=True) for n in range(n_past)]
    sel = []
    for n in range(n_past):
        rank = jnp.zeros((tq, 1), jnp.int32)
        for o in range(n_past):
            if o != n:
                tie = 1 if o < n else 0
                rank = rank + jnp.where(gate[o] > gate[n], 1,
                                        jnp.where(gate[o] == gate[n], tie, 0))
        sel.append(jnp.where(rank < MOBA_TOPK, 1.0, 0.0))
    return sel


def _attn_kernel(*refs, moba):
    if moba:
        q_ref, k_ref, v_ref, kmean_ref, o_ref = refs
    else:
        q_ref, k_ref, v_ref, o_ref = refs
    seq = q_ref.shape[0]
    t = ATTN_BLOCK
    row = lax.broadcasted_iota(jnp.int32, (t, t), 0)
    col = lax.broadcasted_iota(jnp.int32, (t, t), 1)
    causal = col <= row
    for i in range(seq // t):
        own = slice(i * t, (i + 1) * t)
        q = q_ref[own, :]
        s_own = jnp.where(causal, _dot_nt(q, k_ref[own, :]), NEG)
        m = jnp.max(s_own, axis=-1, keepdims=True)
        s_past = []
        if i > 0:
            s_all = _dot_nt(q, k_ref[:i * t, :])
            if moba:
                sel = _moba_block_choice(q.astype(F32), kmean_ref, i)
                s_past = [jnp.where(sel[n] > 0.5, s_all[:, n * t:(n + 1) * t], NEG)
                          for n in range(i)]
            else:
                s_past = [s_all]
            for s in s_past:
                m = jnp.maximum(m, jnp.max(s, axis=-1, keepdims=True))
        p = jnp.exp(s_own - m)
        l = jnp.sum(p, axis=-1, keepdims=True)
        acc = _dot(p.astype(v_ref.dtype), v_ref[own, :])
        start = 0
        for s in s_past:
            p = jnp.exp(s - m)
            l = l + jnp.sum(p, axis=-1, keepdims=True)
            acc = acc + _dot(p.astype(v_ref.dtype), v_ref[start:start + s.shape[1], :])
            start += s.shape[1]
        o_ref[own, :] = (acc / l).astype(o_ref.dtype)


def _head_spec(seq, d):
    return pl.BlockSpec((None, None, seq, d), lambda b, hh: (b, hh, 0, 0))


def _proj_col_spec(seq, col):
    cb = col // LANES
    return pl.BlockSpec((None, seq, LANES), lambda b, hh: (b, 0, cb + hh))


def _attention(q, k, v, v_spec, kmean=None):
    bsz, h, seq, dk = q.shape
    in_specs = [_head_spec(seq, dk), _head_spec(seq, dk), v_spec]
    args = [q, k, v]
    if kmean is not None:
        in_specs.append(_head_spec(kmean.shape[2], LANES))
        args.append(kmean)
    return pl.pallas_call(
        functools.partial(_attn_kernel, moba=kmean is not None),
        grid=(bsz, h),
        in_specs=in_specs,
        out_specs=pl.BlockSpec((None, seq, LANES), lambda b, hh: (b, 0, hh)),
        out_shape=jax.ShapeDtypeStruct((bsz, seq, h * LANES), BF16),
        compiler_params=_cparams(("parallel", "parallel")),
        name="moba_attention" if kmean is not None else "mla_attention",
    )(*args)


def mla_attention(q, k, v):
    return _attention(q, k, v, _head_spec(q.shape[2], LANES))


def _moba_prep_kernel(q_ref, k_ref, qg_ref, kg_ref, cos_ref, sin_ref, qo_ref, ko_ref, km_ref):
    seq = q_ref.shape[0]
    scale = HEAD_DIM ** -0.5
    for blk in range(seq // MOBA_BLOCK):
        sl = slice(blk * MOBA_BLOCK, (blk + 1) * MOBA_BLOCK)
        cos2, sin_s = cos_ref[sl, :], sin_ref[sl, :]
        qn = _rms(q_ref[sl, :].astype(F32)) * qg_ref[...]
        qo_ref[sl, :] = (_rope_full(qn, cos2, sin_s) * scale).astype(qo_ref.dtype)
        kn = _rope_full(_rms(k_ref[sl, :].astype(F32)) * kg_ref[...], cos2, sin_s)
        ko_ref[sl, :] = kn.astype(ko_ref.dtype)
        km_ref[blk:blk + 1, :] = jnp.mean(kn, axis=0, keepdims=True)


def moba_prep(proj3, q_head_g, k_head_g, cos_full, sin_full):
    bsz, seq, _ = proj3.shape
    h = GROUP_HEADS
    nb = seq // MOBA_BLOCK
    const = lambda shape: pl.BlockSpec(shape, lambda b, hh: (0,) * len(shape))
    return pl.pallas_call(
        _moba_prep_kernel,
        grid=(bsz, h),
        in_specs=[
            _proj_col_spec(seq, COL_MOBA),
            _proj_col_spec(seq, COL_MOBA + GROUP_WIDTH),
            const((1, LANES)), const((1, LANES)),
            const((seq, LANES)), const((seq, LANES)),
        ],
        out_specs=[_head_spec(seq, LANES), _head_spec(seq, LANES), _head_spec(nb, LANES)],
        out_shape=[
            jax.ShapeDtypeStruct((bsz, h, seq, LANES), BF16),
            jax.ShapeDtypeStruct((bsz, h, seq, LANES), BF16),
            jax.ShapeDtypeStruct((bsz, h, nb, LANES), F32),
        ],
        compiler_params=_cparams(("parallel", "parallel")),
        name="moba_prep",
    )(proj3, proj3, q_head_g.reshape(1, -1), k_head_g.reshape(1, -1), cos_full, sin_full)


def moba_attention(q, k, kmean, proj3):
    v_spec = _proj_col_spec(q.shape[2], COL_MOBA + 2 * GROUP_WIDTH)
    return _attention(q, k, proj3, v_spec, kmean=kmean)


def _sb_kernel(q_ref, k_ref, v_ref, o_ref):
    seq = q_ref.shape[0]
    t = ATTN_BLOCK
    scale = HEAD_DIM ** -0.5
    row = lax.broadcasted_iota(jnp.int32, (t, t), 0)
    col = lax.broadcasted_iota(jnp.int32, (t, t), 1)
    strict = col < row
    later = jnp.where(row > col, 1.0, 0.0).astype(BF16)
    for i in range(seq // t):
        q = q_ref[i * t:(i + 1) * t, :]
        z_all = _dot_nt(q, k_ref[:(i + 1) * t, :]) * scale
        tail = jnp.zeros((t, 1), F32)
        acc = jnp.zeros((t, v_ref.shape[1]), F32)
        for n in range(i, -1, -1):
            z = z_all[:, n * t:(n + 1) * t]
            log_1m = -_softplus(z)
            if n == i:
                log_1m = jnp.where(strict, log_1m, 0.0)
            hi = log_1m.astype(BF16)
            lo = (log_1m - hi.astype(F32)).astype(BF16)
            suffix = _dot(hi, later) + _dot(lo, later)
            a = jnp.exp(z + log_1m + suffix + tail)
            if n == i:
                a = jnp.where(strict, a, 0.0)
            acc = acc + _dot(a.astype(v_ref.dtype), v_ref[n * t:(n + 1) * t, :])
            tail = tail + jnp.sum(log_1m, axis=-1, keepdims=True)
        o_ref[i * t:(i + 1) * t, :] = acc.astype(o_ref.dtype)


def sb_attention(proj3):
    bsz, seq, _ = proj3.shape
    return pl.pallas_call(
        _sb_kernel,
        grid=(bsz, GROUP_HEADS),
        in_specs=[
            _proj_col_spec(seq, COL_SB),
            _proj_col_spec(seq, COL_SB + GROUP_WIDTH),
            _proj_col_spec(seq, COL_SB + 2 * GROUP_WIDTH),
        ],
        out_specs=pl.BlockSpec((None, seq, LANES), lambda b, hh: (b, 0, hh)),
        out_shape=jax.ShapeDtypeStruct((bsz, seq, GROUP_WIDTH), BF16),
        compiler_params=_cparams(("parallel", "parallel")),
        name="sb_attention",
    )(proj3, proj3, proj3)


def _ret_kernel(q_ref, k_ref, v_ref, g_ref, cos_ref, sin_ref, lg_ref, o_ref):
    seq = q_ref.shape[0]
    c = RET_CHUNK
    log_gamma = lg_ref[...]
    ri = lax.broadcasted_iota(jnp.int32, (c, c), 0).astype(F32)
    ci = lax.broadcasted_iota(jnp.int32, (c, c), 1).astype(F32)
    rel = ri - ci
    intra_decay = jnp.where(rel >= 0, jnp.exp(jnp.maximum(rel, 0.0) * log_gamma), 0.0)
    idx = lax.broadcasted_iota(jnp.int32, (c, 1), 0).astype(F32)
    query_decay = jnp.exp((idx + 1.0) * log_gamma)
    key_decay = jnp.exp((c - 1.0 - idx) * log_gamma)
    chunk_decay = jnp.exp(c * log_gamma)
    k_scale = HEAD_DIM ** -0.5

    state = jnp.zeros((HEAD_DIM, HEAD_DIM), F32)
    for n in range(seq // c):
        sl = slice(n * c, (n + 1) * c)
        cos2, sin_s = cos_ref[sl, :], sin_ref[sl, :]
        q = _rope_full(q_ref[sl, :].astype(F32), cos2, sin_s)
        k = _rope_full(k_ref[sl, :].astype(F32), cos2, sin_s) * k_scale
        vb = v_ref[sl, :]
        qb = q.astype(BF16)
        scores = _dot_nt(qb, k.astype(BF16)) * intra_decay
        y = _dot(scores.astype(BF16), vb)
        y = y + _dot(qb, state.astype(BF16)) * query_decay
        kd_t = jnp.transpose(k * key_decay).astype(BF16)
        state = state * chunk_decay + _dot(kd_t, vb)
        mu = jnp.mean(y, axis=-1, keepdims=True)
        yc = y - mu
        var = jnp.mean(yc * yc, axis=-1, keepdims=True)
        yn = yc * lax.rsqrt(var + NORM_EPS)
        g = g_ref[sl, :].astype(F32)
        o_ref[sl, :] = (g * _sigmoid(g) * yn).astype(o_ref.dtype)


def retention(proj3, cos_full, sin_full):
    bsz, seq, _ = proj3.shape
    h = GROUP_HEADS
    log_gamma = jnp.log(1.0 - 2.0 ** (-5.0 - jnp.arange(h, dtype=F32))).reshape(h, 1, 1)
    col = lambda k: pl.BlockSpec((None, seq, LANES),
                                 lambda b, hh: (b, 0, (COL_RET + k * GROUP_WIDTH) // LANES + hh))
    const = lambda shape: pl.BlockSpec(shape, lambda b, hh: (0,) * len(shape))
    return pl.pallas_call(
        _ret_kernel,
        grid=(bsz, h),
        in_specs=[col(0), col(1), col(2), col(3), const((seq, LANES)), const((seq, LANES)),
                  pl.BlockSpec((None, 1, 1), lambda b, hh: (hh, 0, 0))],
        out_specs=pl.BlockSpec((None, seq, LANES), lambda b, hh: (b, 0, hh)),
        out_shape=jax.ShapeDtypeStruct((bsz, seq, GROUP_WIDTH), BF16),
        compiler_params=_cparams(("parallel", "parallel")),
        name="retention",
    )(proj3, proj3, proj3, proj3, cos_full, sin_full, log_gamma)


def _out_proj_kernel(y0_ref, y1_ref, y2_ref, y3_ref, gg_ref, w_ref, x_ref, gate_ref, o_ref, h_ref):
    @pl.when(pl.program_id(1) == 0)
    def _():
        for grp, y_ref in enumerate((y0_ref, y1_ref, y2_ref, y3_ref)):
            yn = _rms(y_ref[...].astype(F32)) * gg_ref[grp:grp + 1, :]
            h_ref[:, grp * GROUP_WIDTH:(grp + 1) * GROUP_WIDTH] = yn.astype(h_ref.dtype)

    o_ref[...] = x_ref[...] + gate_ref[...] * _dot(h_ref[...], w_ref[...])


def out_proj_residual(ys, group_g, w_out, x2, gate, seq, tm=1024, tn=1024):
    t, d = x2.shape
    tm = min(tm, seq)
    per_b = seq // tm
    bsz = gate.shape[0]
    y_spec = pl.BlockSpec((tm, GROUP_WIDTH), lambda i, j: (i, 0))
    return pl.pallas_call(
        _out_proj_kernel,
        grid=(t // tm, d // tn),
        in_specs=[y_spec, y_spec, y_spec, y_spec,
                  pl.BlockSpec((4, GROUP_WIDTH), lambda i, j: (0, 0)),
                  pl.BlockSpec((4 * GROUP_WIDTH, tn), lambda i, j: (0, j)),
                  pl.BlockSpec((tm, tn), lambda i, j: (i, j)),
                  pl.BlockSpec((None, 1, tn), lambda i, j: (i // per_b, 0, j))],
        out_specs=pl.BlockSpec((tm, tn), lambda i, j: (i, j)),
        out_shape=jax.ShapeDtypeStruct((t, d), F32),
        scratch_shapes=[pltpu.VMEM((tm, 4 * GROUP_WIDTH), BF16)],
        compiler_params=_cparams(("parallel", "arbitrary")),
        name="out_proj_residual",
    )(*[y.reshape(t, GROUP_WIDTH) for y in ys], group_g, w_out, x2, gate.reshape(bsz, 1, d))


def _swiglu_act(h, w1_ref, w3_ref):
    a = _dot(h, w1_ref[...])
    return (a * _sigmoid(a) * _dot(h, w3_ref[...])).astype(BF16)


def _ffn_kernel(x_ref, g_ref, sh_ref, sc_ref, gate_ref, w1_ref, w3_ref, w2_ref, o_ref, h_ref):
    f = pl.program_id(1)

    @pl.when(f == 0)
    def _():
        _norm_mod_rows(x_ref, g_ref, sh_ref, sc_ref, h_ref)
        o_ref[...] = jnp.zeros_like(o_ref)

    o_ref[...] += _dot(_swiglu_act(h_ref[...], w1_ref, w3_ref), w2_ref[...])

    @pl.when(f == pl.num_programs(1) - 1)
    def _():
        o_ref[...] = x_ref[...] + gate_ref[...] * o_ref[...]


def dense_ffn_residual(x2, g, shift, scale, gate, w1, w3, w2, seq, tm=1024, tf=512):
    t, d = x2.shape
    ffn = w1.shape[1]
    tm = min(tm, seq)
    per_b = seq // tm
    bsz = gate.shape[0]
    row = lambda: pl.BlockSpec((None, 1, d), lambda i, f: (i // per_b, 0, 0))
    return pl.pallas_call(
        _ffn_kernel,
        grid=(t // tm, ffn // tf),
        in_specs=[pl.BlockSpec((tm, d), lambda i, f: (i, 0), pipeline_mode=pl.Buffered(1)),
                  pl.BlockSpec((1, d), lambda i, f: (0, 0)),
                  row(), row(), row(),
                  pl.BlockSpec((d, tf), lambda i, f: (0, f)),
                  pl.BlockSpec((d, tf), lambda i, f: (0, f)),
                  pl.BlockSpec((tf, d), lambda i, f: (f, 0))],
        out_specs=pl.BlockSpec((tm, d), lambda i, f: (i, 0)),
        out_shape=jax.ShapeDtypeStruct((t, d), F32),
        scratch_shapes=[pltpu.VMEM((tm, d), BF16)],
        compiler_params=_cparams(("parallel", "arbitrary")),
        name="dense_ffn_residual",
    )(x2, g.reshape(1, d), shift.reshape(bsz, 1, d), scale.reshape(bsz, 1, d),
      gate.reshape(bsz, 1, d), w1, w3, w2)


def _router_kernel(x_ref, g_ref, sh_ref, sc_ref, rw_ref, h_ref, idx_ref, gates_ref):
    _norm_mod_rows(x_ref, g_ref, sh_ref, sc_ref, h_ref)
    logits = jnp.dot(h_ref[...], rw_ref[...], precision=lax.Precision.HIGHEST,
                     preferred_element_type=F32)
    lane = lax.broadcasted_iota(jnp.int32, logits.shape, 1)
    lane_f = lane.astype(F32)
    logits = jnp.where(lane < N_EXPERTS, logits, -jnp.inf)
    m0 = jnp.max(logits, axis=-1, keepdims=True)
    e0 = jnp.min(jnp.where(logits == m0, lane_f, float(LANES)), axis=-1, keepdims=True)
    rest = jnp.where(lane_f == e0, -jnp.inf, logits)
    m1 = jnp.max(rest, axis=-1, keepdims=True)
    e1 = jnp.min(jnp.where(rest == m1, lane_f, float(LANES)), axis=-1, keepdims=True)
    p1 = jnp.exp(m1 - m0)
    g0 = 1.0 / (1.0 + p1)
    idx_ref[...] = jnp.where(lane == 0, e0, jnp.where(lane == 1, e1, 0.0)).astype(jnp.int32)
    gates_ref[...] = jnp.where(lane == 0, g0, jnp.where(lane == 1, p1 * g0, 0.0))


def moe_router(x2, g, shift, scale, router_w, seq, tm=256):
    t, d = x2.shape
    tm = min(tm, seq)
    per_b = seq // tm
    bsz = shift.shape[0]
    rw = jnp.zeros((d, LANES), F32).at[:, :N_EXPERTS].set(router_w)
    row = lambda: pl.BlockSpec((None, 1, d), lambda i: (i // per_b, 0, 0))
    return pl.pallas_call(
        _router_kernel,
        grid=(t // tm,),
        in_specs=[pl.BlockSpec((tm, d), lambda i: (i, 0)),
                  pl.BlockSpec((1, d), lambda i: (0, 0)),
                  row(), row(),
                  pl.BlockSpec((d, LANES), lambda i: (0, 0))],
        out_specs=[pl.BlockSpec((tm, d), lambda i: (i, 0)),
                   pl.BlockSpec((tm, LANES), lambda i: (i, 0)),
                   pl.BlockSpec((tm, LANES), lambda i: (i, 0))],
        out_shape=[jax.ShapeDtypeStruct((t, d), F32),
                   jax.ShapeDtypeStruct((t, LANES), jnp.int32),
                   jax.ShapeDtypeStruct((t, LANES), F32)],
        compiler_params=_cparams(("parallel",)),
        name="moe_router",
    )(x2, g.reshape(1, d), shift.reshape(bsz, 1, d), scale.reshape(bsz, 1, d), rw)


GATHER_UNROLL = 8


def _row_copy(src_ref, dst_ref, sem, src_row, dst_row):
    return pltpu.make_async_copy(src_ref.at[pl.ds(src_row, 1), :],
                                 dst_ref.at[pl.ds(dst_row, 1), :], sem)


def _start_row_gather(src_ref, dst_ref, sem, row_of, n_rows):
    def start(r, _):
        _row_copy(src_ref, dst_ref, sem, row_of(r), r).start()
        return 0

    lax.fori_loop(0, n_rows, start, 0, unroll=GATHER_UNROLL)


def _wait_row_gather(dst_ref, sem):
    pltpu.make_async_copy(dst_ref, dst_ref, sem).wait()


def _moe_ffn_kernel(tile_e_ref, tile_ok_ref, buf_t_ref, h_hbm, w1_ref, w3_ref, w2_ref, o_ref,
                    hf_ref, hb_ref, sem):
    i = pl.program_id(0)
    f = pl.program_id(1)
    n_tiles = pl.num_programs(0)
    tm = hb_ref.shape[0]
    ok = tile_ok_ref[i] > 0

    def start_gather(tile):
        _start_row_gather(h_hbm, hf_ref, sem, lambda r: buf_t_ref[tile * tm + r], tm)

    @pl.when(ok & (f == 0))
    def _():
        @pl.when(i == 0)
        def _():
            start_gather(0)

        _wait_row_gather(hf_ref, sem)
        hb_ref[...] = hf_ref[...].astype(hb_ref.dtype)
        o_ref[...] = jnp.zeros_like(o_ref)

        nxt = jnp.minimum(i + 1, n_tiles - 1)

        @pl.when((i + 1 < n_tiles) & (tile_ok_ref[nxt] > 0))
        def _():
            start_gather(i + 1)

    @pl.when(ok)
    def _():
        o_ref[...] += _dot(_swiglu_act(hb_ref[...], w1_ref, w3_ref), w2_ref[...])

    @pl.when(jnp.logical_not(ok) & (f == pl.num_programs(1) - 1))
    def _():
        o_ref[...] = jnp.zeros_like(o_ref)


def moe_expert_ffn(h2, tile_e, tile_ok, buf_t, w1, w3, w2, tm, tf=1408):
    t, d = h2.shape
    ffn = w1.shape[2]
    nf = ffn // tf
    n_tiles = tile_e.shape[0]

    def fsel(i, f, ok):
        return jnp.where(ok[i] > 0, f, nf - 1)

    grid_spec = pltpu.PrefetchScalarGridSpec(
        num_scalar_prefetch=3,
        grid=(n_tiles, nf),
        in_specs=[pl.BlockSpec(memory_space=pl.ANY),
                  pl.BlockSpec((None, d, tf), lambda i, f, te, ok, bt: (te[i], 0, fsel(i, f, ok))),
                  pl.BlockSpec((None, d, tf), lambda i, f, te, ok, bt: (te[i], 0, fsel(i, f, ok))),
                  pl.BlockSpec((None, tf, d), lambda i, f, te, ok, bt: (te[i], fsel(i, f, ok), 0))],
        out_specs=pl.BlockSpec((tm, d), lambda i, f, te, ok, bt: (i, 0)),
        scratch_shapes=[pltpu.VMEM((tm, d), F32), pltpu.VMEM((tm, d), BF16),
                        pltpu.SemaphoreType.DMA(())],
    )
    return pl.pallas_call(
        _moe_ffn_kernel,
        grid_spec=grid_spec,
        out_shape=jax.ShapeDtypeStruct((n_tiles * tm, d), F32),
        compiler_params=_cparams(("arbitrary", "arbitrary"), VMEM_LIMIT_MOE),
        name="moe_expert_ffn",
    )(tile_e, tile_ok, buf_t, h2, w1, w3, w2)


def _moe_combine_kernel(d0_ref, d1_ref, y_hbm, x_ref, gate_ref, gates_ref, o_ref,
                        y0_ref, y1_ref, sems0, sems1):
    i = pl.program_id(0)
    tm = x_ref.shape[0]
    slot = i % 2

    def start_gather(tile, into):
        def start(r, _):
            _row_copy(y_hbm, y0_ref.at[into], sems0.at[into], d0_ref[tile * tm + r], r).start()
            _row_copy(y_hbm, y1_ref.at[into], sems1.at[into], d1_ref[tile * tm + r], r).start()
            return 0

        lax.fori_loop(0, tm, start, 0, unroll=GATHER_UNROLL)

    @pl.when(i == 0)
    def _():
        start_gather(0, 0)

    @pl.when(i + 1 < pl.num_programs(0))
    def _():
        start_gather(i + 1, 1 - slot)

    _wait_row_gather(y0_ref.at[slot], sems0.at[slot])
    _wait_row_gather(y1_ref.at[slot], sems1.at[slot])
    gates = gates_ref[...]
    y = gates[:, 0:1] * y0_ref[slot] + gates[:, 1:2] * y1_ref[slot]
    o_ref[...] = x_ref[...] + gate_ref[...] * y


def moe_combine_residual(dest0, dest1, y_buf, x2, gate, gates, seq, tm=256):
    t, d = x2.shape
    tm = min(tm, seq)
    per_b = seq // tm
    bsz = gate.shape[0]
    grid_spec = pltpu.PrefetchScalarGridSpec(
        num_scalar_prefetch=2,
        grid=(t // tm,),
        in_specs=[pl.BlockSpec(memory_space=pl.ANY),
                  pl.BlockSpec((tm, d), lambda i, d0, d1: (i, 0)),
                  pl.BlockSpec((None, 1, d), lambda i, d0, d1: (i // per_b, 0, 0)),
                  pl.BlockSpec((tm, LANES), lambda i, d0, d1: (i, 0))],
        out_specs=pl.BlockSpec((tm, d), lambda i, d0, d1: (i, 0)),
        scratch_shapes=[pltpu.VMEM((2, tm, d), F32), pltpu.VMEM((2, tm, d), F32),
                        pltpu.SemaphoreType.DMA((2,)), pltpu.SemaphoreType.DMA((2,))],
    )
    return pl.pallas_call(
        _moe_combine_kernel,
        grid_spec=grid_spec,
        out_shape=jax.ShapeDtypeStruct((t, d), F32),
        compiler_params=_cparams(("arbitrary",)),
        name="moe_combine_residual",
    )(dest0, dest1, y_buf, x2, gate.reshape(bsz, 1, d), gates)


def _moe_routing_tables(idx, tm):
    t = idx.shape[0]
    n_slots = t * MOE_TOPK
    n_tiles = n_slots // tm + N_EXPERTS
    flat_e = idx[:, :MOE_TOPK].reshape(-1)
    onehot = (flat_e[:, None] == jnp.arange(N_EXPERTS)[None, :]).astype(jnp.int32)
    counts = jnp.sum(onehot, axis=0)
    rank = jnp.sum((jnp.cumsum(onehot, axis=0) - onehot) * onehot, axis=1)
    padded = (counts + tm - 1) // tm * tm
    pad_ends = jnp.cumsum(padded)
    pad_starts = pad_ends - padded
    dest = (pad_starts[flat_e] + rank).astype(jnp.int32)
    buf_t = jnp.zeros((n_tiles * tm,), jnp.int32).at[dest].set(
        (jnp.arange(n_slots) // MOE_TOPK).astype(jnp.int32))
    tile_start = jnp.arange(n_tiles) * tm
    tile_e = jnp.minimum(jnp.searchsorted(pad_ends, tile_start, side="right"),
                         N_EXPERTS - 1).astype(jnp.int32)
    tile_ok = (tile_start < pad_ends[-1]).astype(jnp.int32)
    dest2 = dest.reshape(t, MOE_TOPK)
    return tile_e, tile_ok, buf_t, dest2[:, 0], dest2[:, 1]


def moe_ffn_residual(x2, g, shift, scale, gate, router_w, w1, w3, w2, seq, tile_rows=512):
    h2, idx, gates = moe_router(x2, g, shift, scale, router_w, seq)
    tile_e, tile_ok, buf_t, dest0, dest1 = _moe_routing_tables(idx, tile_rows)
    y_buf = moe_expert_ffn(h2, tile_e, tile_ok, buf_t, w1, w3, w2, tile_rows)
    return moe_combine_residual(dest0, dest1, y_buf, x2, gate, gates, seq)


def _pad_w_in(w_in):
    d = w_in.shape[0]
    kpe_end = MLA_Q_RANK + MLA_KV_RANK + MLA_ROPE_DIM
    tail = PROJ_WIDTH - (COL_MOBA + 3 * GROUP_WIDTH)
    return jnp.concatenate([
        w_in[:, :kpe_end], jnp.zeros((d, LANES - MLA_ROPE_DIM), w_in.dtype),
        w_in[:, kpe_end:], jnp.zeros((d, tail), w_in.dtype)], axis=1).astype(BF16)


def _pad_heads_192(a):
    lead = a.shape[:-1]
    a = a.reshape(lead + (GROUP_HEADS, MLA_QK_DIM))
    a = jnp.pad(a, [(0, 0)] * len(lead) + [(0, 0), (0, MLA_QK_PAD - MLA_QK_DIM)])
    return a.reshape(lead + (GROUP_HEADS * MLA_QK_PAD,))


def kernel(x, c, positions, ada_w, ada_b, norm_mix_g, norm_ffn_g, w_in, mla_q_norm_g, mla_kv_norm_g, mla_w_uq, mla_w_ukv, mla_q_head_g, mla_k_head_g, moba_q_head_g, moba_k_head_g, group_norm_g, w_out, ffn_w1, ffn_w3, ffn_w2, router_w, moe_w1, moe_w3, moe_w2):
    bsz, seq, d = x.shape
    depth = ada_w.shape[0]
    cos_pe, sin_pe = _rope_tables(positions, MLA_ROPE_DIM)
    cos_full, sin_full = _rope_tables(positions, HEAD_DIM)
    mod = ada_modulation(c, ada_w, ada_b)
    x2 = x.reshape(bsz * seq, d)
    for l in range(depth):
        shift_m, scale_m, gate_m, shift_f, scale_f, gate_f = jnp.split(mod[l], 6, axis=-1)
        proj = norm_mod_proj(x2, norm_mix_g[l], shift_m, scale_m, _pad_w_in(w_in[l]), seq)
        proj3 = proj.reshape(bsz, seq, PROJ_WIDTH)
        pad_g = lambda g: jnp.pad(g, (0, MLA_QK_PAD - MLA_QK_DIM)).reshape(1, MLA_QK_PAD)
        q_mla, k_mla, v_mla = mla_prep(
            proj3, mla_q_norm_g[l], mla_kv_norm_g[l],
            _pad_heads_192(mla_w_uq[l]).astype(BF16), mla_w_ukv[l].astype(BF16),
            pad_g(mla_q_head_g[l]), pad_g(mla_k_head_g[l]), cos_pe, sin_pe)
        y_mla = mla_attention(q_mla, k_mla, v_mla)
        y_ret = retention(proj3, cos_full, sin_full)
        y_sb = sb_attention(proj3)
        q_mb, k_mb, kmean = moba_prep(proj3, moba_q_head_g[l], moba_k_head_g[l], cos_full, sin_full)
        y_moba = moba_attention(q_mb, k_mb, kmean, proj3)
        x2 = out_proj_residual((y_mla, y_ret, y_sb, y_moba), group_norm_g[l],
                               w_out[l].astype(BF16), x2, gate_m, seq)
        j = l // 2
        if l % 2 == 0:
            x2 = dense_ffn_residual(x2, norm_ffn_g[l], shift_f, scale_f, gate_f,
                                    ffn_w1[j].astype(BF16), ffn_w3[j].astype(BF16),
                                    ffn_w2[j].astype(BF16), seq)
        else:
            x2 = moe_ffn_residual(x2, norm_ffn_g[l], shift_f, scale_f, gate_f, router_w[j],
                                  moe_w1[j].astype(BF16), moe_w3[j].astype(BF16),
                                  moe_w2[j].astype(BF16), seq)
    return x2.reshape(bsz, seq, d)
```

```python
import functools

import jax
import jax.numpy as jnp
from jax import lax
from jax.experimental import pallas as pl
from jax.experimental.pallas import tpu as pltpu

F32 = jnp.float32
BF16 = jnp.bfloat16

HEAD_DIM = 128
GROUP_HEADS = 4
GROUP_WIDTH = 512
MLA_Q_RANK = 512
MLA_KV_RANK = 256
MLA_NOPE_DIM = 128
MLA_ROPE_DIM = 64
MLA_QK_DIM = MLA_NOPE_DIM + MLA_ROPE_DIM
MLA_QK_PAD = 256
RET_CHUNK = 128
MOBA_BLOCK = 256
MOBA_TOPK = 3
ROPE_THETA = 10000.0
NORM_EPS = 1e-6
NEG = -1e30
N_EXPERTS = 8
MOE_TOPK = 2

LANES = 128
ATTN_BLOCK = 256
VMEM_LIMIT = 56 * 1024 * 1024
VMEM_LIMIT_MOE = 60 * 1024 * 1024

COL_CQ = 0
COL_CKV = 512
COL_KPE = 768
COL_RET = 896
COL_SB = COL_RET + 4 * GROUP_WIDTH
COL_MOBA = COL_SB + 3 * GROUP_WIDTH
PROJ_WIDTH = 6144


def _cparams(sem, vmem=VMEM_LIMIT):
    return pltpu.CompilerParams(dimension_semantics=sem, vmem_limit_bytes=vmem)


def _dot(a, b):
    return jnp.dot(a, b, preferred_element_type=F32)


def _dot_nt(a, b):
    return lax.dot_general(a, b, (((1,), (1,)), ((), ())), preferred_element_type=F32)


def _sigmoid(x):
    return 1.0 / (1.0 + jnp.exp(-x))


def _softplus(z):
    return jnp.maximum(z, 0.0) + jnp.log(1.0 + jnp.exp(-jnp.abs(z)))


def _rms(xf, width=None):
    width = xf.shape[-1] if width is None else width
    ss = jnp.sum(xf * xf, axis=-1, keepdims=True) * (1.0 / width)
    return xf * lax.rsqrt(ss + NORM_EPS)


def _ada_kernel(c_ref, w_ref, b_ref, o_ref):
    c = c_ref[...]
    cond = c * _sigmoid(c)
    o_ref[...] = _dot(cond.astype(BF16), w_ref[...].astype(BF16)) + b_ref[...]


def ada_modulation(c, ada_w, ada_b, tn=1024):
    depth, d, n = ada_w.shape
    b = c.shape[0]
    return pl.pallas_call(
        _ada_kernel,
        grid=(depth, n // tn),
        in_specs=[
            pl.BlockSpec((b, d), lambda l, j: (0, 0)),
            pl.BlockSpec((None, d, tn), lambda l, j: (l, 0, j)),
            pl.BlockSpec((None, 1, tn), lambda l, j: (l, 0, j)),
        ],
        out_specs=pl.BlockSpec((None, b, tn), lambda l, j: (l, 0, j)),
        out_shape=jax.ShapeDtypeStruct((depth, b, n), F32),
        compiler_params=_cparams(("parallel", "parallel")),
        name="ada_modulation",
    )(c, ada_w, ada_b.reshape(depth, 1, n))


def _norm_mod_rows(x_ref, g_ref, sh_ref, sc_ref, dst_ref, rows=32):
    tm = x_ref.shape[0]
    mul = g_ref[...] * (1.0 + sc_ref[...])
    sh = sh_ref[...]

    def body(r, _):
        sl = pl.ds(pl.multiple_of(r * rows, rows), rows)
        dst_ref[sl, :] = (_rms(x_ref[sl, :]) * mul + sh).astype(dst_ref.dtype)
        return 0

    lax.fori_loop(0, tm // rows, body, 0, unroll=2)


def _proj_kernel(x_ref, g_ref, sh_ref, sc_ref, w_ref, o_ref, h_ref):
    @pl.when(pl.program_id(1) == 0)
    def _():
        _norm_mod_rows(x_ref, g_ref, sh_ref, sc_ref, h_ref)

    o_ref[...] = _dot(h_ref[...], w_ref[...]).astype(o_ref.dtype)


def norm_mod_proj(x2, g, shift, scale, w, seq, tm=1024, tn=2048):
    t, d = x2.shape
    n = w.shape[1]
    tm = min(tm, seq)
    per_b = seq // tm
    bsz = shift.shape[0]
    return pl.pallas_call(
        _proj_kernel,
        grid=(t // tm, n // tn),
        in_specs=[
            pl.BlockSpec((tm, d), lambda i, j: (i, 0)),
            pl.BlockSpec((1, d), lambda i, j: (0, 0)),
            pl.BlockSpec((None, 1, d), lambda i, j: (i // per_b, 0, 0)),
            pl.BlockSpec((None, 1, d), lambda i, j: (i // per_b, 0, 0)),
            pl.BlockSpec((d, tn), lambda i, j: (0, j)),
        ],
        out_specs=pl.BlockSpec((tm, tn), lambda i, j: (i, j)),
        out_shape=jax.ShapeDtypeStruct((t, n), BF16),
        scratch_shapes=[pltpu.VMEM((tm, d), BF16)],
        compiler_params=_cparams(("parallel", "arbitrary")),
        name="norm_mod_proj",
    )(x2, g.reshape(1, d), shift.reshape(bsz, 1, d), scale.reshape(bsz, 1, d), w)


def _rope_tables(positions, dim):
    inv_freq = ROPE_THETA ** (-jnp.arange(0, dim, 2, dtype=F32) / dim)
    ang = positions.astype(F32)[:, None] * inv_freq[None, :]
    cos, sin = jnp.cos(ang), jnp.sin(ang)
    pad = jnp.zeros((positions.shape[0], LANES - dim), F32)
    return (jnp.concatenate([cos, cos, pad], axis=-1),
            jnp.concatenate([-sin, sin, pad], axis=-1))


def _rope_full(z, cos2, sin_s):
    return z * cos2 + pltpu.roll(z, 64, 1) * sin_s


def _rope_64(z, cos2, sin_s):
    lane = lax.broadcasted_iota(jnp.int32, z.shape, 1)
    partner = jnp.where(lane < 32, pltpu.roll(z, 96, 1), pltpu.roll(z, 32, 1))
    return z * cos2 + partner * sin_s


def _mla_prep_kernel(p_ref, qg_ref, kvg_ref, wuq_ref, wukv_ref, qhg_ref, khg_ref,
                     cos_ref, sin_ref, q_ref, k_ref, v_ref):
    p = p_ref[...].astype(F32)
    c_q = p[:, COL_CQ:COL_CQ + MLA_Q_RANK]
    c_kv = p[:, COL_CKV:COL_CKV + MLA_KV_RANK]
    k_pe = p[:, COL_KPE:COL_KPE + LANES]
    q = _dot((_rms(c_q) * qg_ref[...]).astype(BF16), wuq_ref[...])
    kv = _dot((_rms(c_kv) * kvg_ref[...]).astype(BF16), wukv_ref[...])
    cos2, sin_s = cos_ref[...], sin_ref[...]
    qhg, khg = qhg_ref[...], khg_ref[...]
    scale = MLA_QK_DIM ** -0.5
    pe_ss = jnp.sum(k_pe * k_pe, axis=-1, keepdims=True)
    for h in range(GROUP_HEADS):
        qh = q[:, h * MLA_QK_PAD:(h + 1) * MLA_QK_PAD]
        qh = _rms(qh, MLA_QK_DIM) * qhg * scale
        q_ref[h, :, :LANES] = qh[:, :LANES].astype(q_ref.dtype)
        q_ref[h, :, LANES:] = _rope_64(qh[:, LANES:], cos2, sin_s).astype(q_ref.dtype)
        k_nope = kv[:, 2 * h * LANES:(2 * h + 1) * LANES]
        ss = (jnp.sum(k_nope * k_nope, axis=-1, keepdims=True) + pe_ss) * (1.0 / MLA_QK_DIM)
        r = lax.rsqrt(ss + NORM_EPS)
        k_ref[h, :, :LANES] = (k_nope * r * khg[:, :LANES]).astype(k_ref.dtype)
        k_ref[h, :, LANES:] = _rope_64(k_pe * r * khg[:, LANES:], cos2, sin_s).astype(k_ref.dtype)
        v_ref[h] = kv[:, (2 * h + 1) * LANES:(2 * h + 2) * LANES].astype(v_ref.dtype)


def mla_prep(proj3, q_norm_g, kv_norm_g, w_uq_pad, w_ukv, q_head_g_pad, k_head_g_pad,
             cos_pe, sin_pe, tm=512):
    bsz, seq, _ = proj3.shape
    h = GROUP_HEADS
    n_in = COL_KPE + LANES
    const = lambda shape: pl.BlockSpec(shape, lambda b, i: (0,) * len(shape))
    return pl.pallas_call(
        _mla_prep_kernel,
        grid=(bsz, seq // tm),
        in_specs=[
            pl.BlockSpec((None, tm, n_in), lambda b, i: (b, i, 0)),
            const((1, MLA_Q_RANK)), const((1, MLA_KV_RANK)),
            const((MLA_Q_RANK, h * MLA_QK_PAD)), const((MLA_KV_RANK, h * 2 * LANES)),
            const((1, MLA_QK_PAD)), const((1, MLA_QK_PAD)),
            pl.BlockSpec((tm, LANES), lambda b, i: (i, 0)),
            pl.BlockSpec((tm, LANES), lambda b, i: (i, 0)),
        ],
        out_specs=[
            pl.BlockSpec((None, h, tm, MLA_QK_PAD), lambda b, i: (b, 0, i, 0)),
            pl.BlockSpec((None, h, tm, MLA_QK_PAD), lambda b, i: (b, 0, i, 0)),
            pl.BlockSpec((None, h, tm, LANES), lambda b, i: (b, 0, i, 0)),
        ],
        out_shape=[
            jax.ShapeDtypeStruct((bsz, h, seq, MLA_QK_PAD), BF16),
            jax.ShapeDtypeStruct((bsz, h, seq, MLA_QK_PAD), BF16),
            jax.ShapeDtypeStruct((bsz, h, seq, LANES), BF16),
        ],
        compiler_params=_cparams(("parallel", "parallel")),
        name="mla_prep",
    )(proj3, q_norm_g.reshape(1, -1), kv_norm_g.reshape(1, -1), w_uq_pad, w_ukv,
      q_head_g_pad, k_head_g_pad, cos_pe, sin_pe)


def _moba_block_choice(qf, kmean_ref, n_past):
    tq = qf.shape[0]
    if n_past <= MOBA_TOPK:
        return [jnp.ones((tq, 1), F32)] * n_past
    gate = [jnp.sum(qf * kmean_ref[n:n + 1, :], axis=-1, keepdims=True) for n in range(n_past)]
    sel = []
    for n in range(n_past):
        rank = jnp.zeros((tq, 1), jnp.int32)
        for o in range(n_past):
            if o != n:
                tie = 1 if o < n else 0
                rank = rank + jnp.where(gate[o] > gate[n], 1,
                                        jnp.where(gate[o] == gate[n], tie, 0))
        sel.append(jnp.where(rank < MOBA_TOPK, 1.0, 0.0))
    return sel


def _attn_kernel(*refs, moba):
    if moba:
        q_ref, k_ref, v_ref, kmean_ref, o_ref = refs
    else:
        q_ref, k_ref, v_ref, o_ref = refs
    seq = q_ref.shape[0]
    t = ATTN_BLOCK
    row = lax.broadcasted_iota(jnp.int32, (t, t), 0)
    col = lax.broadcasted_iota(jnp.int32, (t, t), 1)
    causal = col <= row
    for i in range(seq // t):
        own = slice(i * t, (i + 1) * t)
        q = q_ref[own, :]
        s_own = jnp.where(causal, _dot_nt(q, k_ref[own, :]), NEG)
        m = jnp.max(s_own, axis=-1, keepdims=True)
        s_past = []
        if i > 0:
            s_all = _dot_nt(q, k_ref[:i * t, :])
            if moba:
                sel = _moba_block_choice(q.astype(F32), kmean_ref, i)
                s_past = [jnp.where(sel[n] > 0.5, s_all[:, n * t:(n + 1) * t], NEG)
                          for n in range(i)]
            else:
                s_past = [s_all]
            for s in s_past:
                m = jnp.maximum(m, jnp.max(s, axis=-1, keepdims=True))
        p = jnp.exp(s_own - m)
        l = jnp.sum(p, axis=-1, keepdims=True)
        acc = _dot(p.astype(v_ref.dtype), v_ref[own, :])
        start = 0
        for s in s_past:
            p = jnp.exp(s - m)
            l = l + jnp.sum(p, axis=-1, keepdims=True)
            acc = acc + _dot(p.astype(v_ref.dtype), v_ref[start:start + s.shape[1], :])
            start += s.shape[1]
        o_ref[own, :] = (acc / l).astype(o_ref.dtype)


def _head_spec(seq, d):
    return pl.BlockSpec((None, None, seq, d), lambda b, hh: (b, hh, 0, 0))


def _proj_col_spec(seq, col):
    cb = col // LANES
    return pl.BlockSpec((None, seq, LANES), lambda b, hh: (b, 0, cb + hh))


def _attention(q, k, v, v_spec, kmean=None):
    bsz, h, seq, dk = q.shape
    in_specs = [_head_spec(seq, dk), _head_spec(seq, dk), v_spec]
    args = [q, k, v]
    if kmean is not None:
        in_specs.append(_head_spec(kmean.shape[2], LANES))
        args.append(kmean)
    return pl.pallas_call(
        functools.partial(_attn_kernel, moba=kmean is not None),
        grid=(bsz, h),
        in_specs=in_specs,
        out_specs=pl.BlockSpec((None, seq, LANES), lambda b, hh: (b, 0, hh)),
        out_shape=jax.ShapeDtypeStruct((bsz, seq, h * LANES), BF16),
        compiler_params=_cparams(("parallel", "parallel")),
        name="moba_attention" if kmean is not None else "mla_attention",
    )(*args)


def mla_attention(q, k, v):
    return _attention(q, k, v, _head_spec(q.shape[2], LANES))


def _moba_prep_kernel(q_ref, k_ref, qg_ref, kg_ref, cos_ref, sin_ref, qo_ref, ko_ref, km_ref):
    seq = q_ref.shape[0]
    scale = HEAD_DIM ** -0.5
    for blk in range(seq // MOBA_BLOCK):
        sl = slice(blk * MOBA_BLOCK, (blk + 1) * MOBA_BLOCK)
        cos2, sin_s = cos_ref[sl, :], sin_ref[sl, :]
        qn = _rms(q_ref[sl, :].astype(F32)) * qg_ref[...]
        qo_ref[sl, :] = (_rope_full(qn, cos2, sin_s) * scale).astype(qo_ref.dtype)
        kn = _rope_full(_rms(k_ref[sl, :].astype(F32)) * kg_ref[...], cos2, sin_s)
        ko_ref[sl, :] = kn.astype(ko_ref.dtype)
        km_ref[blk:blk + 1, :] = jnp.mean(kn, axis=0, keepdims=True)


def moba_prep(proj3, q_head_g, k_head_g, cos_full, sin_full):
    bsz, seq, _ = proj3.shape
    h = GROUP_HEADS
    nb = seq // MOBA_BLOCK
    const = lambda shape: pl.BlockSpec(shape, lambda b, hh: (0,) * len(shape))
    return pl.pallas_call(
        _moba_prep_kernel,
        grid=(bsz, h),
        in_specs=[
            _proj_col_spec(seq, COL_MOBA),
            _proj_col_spec(seq, COL_MOBA + GROUP_WIDTH),
            const((1, LANES)), const((1, LANES)),
            const((seq, LANES)), const((seq, LANES)),
        ],
        out_specs=[_head_spec(seq, LANES), _head_spec(seq, LANES), _head_spec(nb, LANES)],
        out_shape=[
            jax.ShapeDtypeStruct((bsz, h, seq, LANES), BF16),
            jax.ShapeDtypeStruct((bsz, h, seq, LANES), BF16),
            jax.ShapeDtypeStruct((bsz, h, nb, LANES), F32),
        ],
        compiler_params=_cparams(("parallel", "parallel")),
        name="moba_prep",
    )(proj3, proj3, q_head_g.reshape(1, -1), k_head_g.reshape(1, -1), cos_full, sin_full)


def moba_attention(q, k, kmean, proj3):
    v_spec = _proj_col_spec(q.shape[2], COL_MOBA + 2 * GROUP_WIDTH)
    return _attention(q, k, proj3, v_spec, kmean=kmean)


SB_DEAD_TAIL = -104.0


def _sb_tile(q, k_blk, v_blk, tail, later, strict):
    z = _dot_nt(q, k_blk) * (HEAD_DIM ** -0.5)
    log_1m = -_softplus(z)
    if strict is not None:
        log_1m = jnp.where(strict, log_1m, 0.0)
    hi = log_1m.astype(BF16)
    lo = (log_1m - hi.astype(F32)).astype(BF16)
    suffix = _dot(hi, later) + _dot(lo, later)
    a = jnp.exp(z + log_1m + suffix + tail)
    if strict is not None:
        a = jnp.where(strict, a, 0.0)
    return _dot(a.astype(v_blk.dtype), v_blk), jnp.sum(log_1m, axis=-1, keepdims=True)


def _sb_kernel(q_ref, k_ref, v_ref, o_ref, acc_ref, tail_ref):
    seq = q_ref.shape[0]
    t = ATTN_BLOCK
    row = lax.broadcasted_iota(jnp.int32, (t, t), 0)
    col = lax.broadcasted_iota(jnp.int32, (t, t), 1)
    strict = col < row
    later = jnp.where(row > col, 1.0, 0.0).astype(BF16)
    blk = lambda n: slice(n * t, (n + 1) * t)

    for i in range(seq // t):
        q = q_ref[blk(i), :]
        acc, tail = _sb_tile(q, k_ref[blk(i), :], v_ref[blk(i), :], 0.0, later, strict)
        if i > 0:
            y, s = _sb_tile(q, k_ref[blk(i - 1), :], v_ref[blk(i - 1), :], tail, later, None)
            acc, tail = acc + y, tail + s
        acc_ref[blk(i), :] = acc
        tail_ref[blk(i), :] = tail

    for i in range(2, seq // t):
        alive = jnp.max(tail_ref[blk(i), :], axis=0, keepdims=True)[0, 0] > SB_DEAD_TAIL

        @pl.when(alive)
        def _():
            q = q_ref[blk(i), :]
            acc, tail = acc_ref[blk(i), :], tail_ref[blk(i), :]
            for n in range(i - 2, -1, -1):
                y, s = _sb_tile(q, k_ref[blk(n), :], v_ref[blk(n), :], tail, later, None)
                acc, tail = acc + y, tail + s
            acc_ref[blk(i), :] = acc

    o_ref[...] = acc_ref[...].astype(o_ref.dtype)


def sb_attention(proj3):
    bsz, seq, _ = proj3.shape
    return pl.pallas_call(
        _sb_kernel,
        grid=(bsz, GROUP_HEADS),
        in_specs=[
            _proj_col_spec(seq, COL_SB),
            _proj_col_spec(seq, COL_SB + GROUP_WIDTH),
            _proj_col_spec(seq, COL_SB + 2 * GROUP_WIDTH),
        ],
        out_specs=pl.BlockSpec((None, seq, LANES), lambda b, hh: (b, 0, hh)),
        out_shape=jax.ShapeDtypeStruct((bsz, seq, GROUP_WIDTH), BF16),
        scratch_shapes=[pltpu.VMEM((seq, LANES), F32), pltpu.VMEM((seq, 1), F32)],
        compiler_params=_cparams(("parallel", "parallel")),
        name="sb_attention",
    )(proj3, proj3, proj3)


def _ret_kernel(q_ref, k_ref, v_ref, g_ref, cos_ref, sin_ref, lg_ref, o_ref):
    seq = q_ref.shape[0]
    c = RET_CHUNK
    log_gamma = lg_ref[...]
    ri = lax.broadcasted_iota(jnp.int32, (c, c), 0).astype(F32)
    ci = lax.broadcasted_iota(jnp.int32, (c, c), 1).astype(F32)
    rel = ri - ci
    intra_decay = jnp.where(rel >= 0, jnp.exp(jnp.maximum(rel, 0.0) * log_gamma), 0.0)
    idx = lax.broadcasted_iota(jnp.int32, (c, 1), 0).astype(F32)
    query_decay = jnp.exp((idx + 1.0) * log_gamma)
    key_decay = jnp.exp((c - 1.0 - idx) * log_gamma)
    chunk_decay = jnp.exp(c * log_gamma)
    k_scale = HEAD_DIM ** -0.5

    state = jnp.zeros((HEAD_DIM, HEAD_DIM), F32)
    for n in range(seq // c):
        sl = slice(n * c, (n + 1) * c)
        cos2, sin_s = cos_ref[sl, :], sin_ref[sl, :]
        q = _rope_full(q_ref[sl, :].astype(F32), cos2, sin_s)
        k = _rope_full(k_ref[sl, :].astype(F32), cos2, sin_s) * k_scale
        vb = v_ref[sl, :]
        qb = q.astype(BF16)
        scores = _dot_nt(qb, k.astype(BF16)) * intra_decay
        y = _dot(scores.astype(BF16), vb)
        y = y + _dot(qb, state.astype(BF16)) * query_decay
        kd_t = jnp.transpose(k * key_decay).astype(BF16)
        state = state * chunk_decay + _dot(kd_t, vb)
        mu = jnp.mean(y, axis=-1, keepdims=True)
        yc = y - mu
        var = jnp.mean(yc * yc, axis=-1, keepdims=True)
        yn = yc * lax.rsqrt(var + NORM_EPS)
        g = g_ref[sl, :].astype(F32)
        o_ref[sl, :] = (g * _sigmoid(g) * yn).astype(o_ref.dtype)


def retention(proj3, cos_full, sin_full):
    bsz, seq, _ = proj3.shape
    h = GROUP_HEADS
    log_gamma = jnp.log(1.0 - 2.0 ** (-5.0 - jnp.arange(h, dtype=F32))).reshape(h, 1, 1)
    col = lambda k: pl.BlockSpec((None, seq, LANES),
                                 lambda b, hh: (b, 0, (COL_RET + k * GROUP_WIDTH) // LANES + hh))
    const = lambda shape: pl.BlockSpec(shape, lambda b, hh: (0,) * len(shape))
    return pl.pallas_call(
        _ret_kernel,
        grid=(bsz, h),
        in_specs=[col(0), col(1), col(2), col(3), const((seq, LANES)), const((seq, LANES)),
                  pl.BlockSpec((None, 1, 1), lambda b, hh: (hh, 0, 0))],
        out_specs=pl.BlockSpec((None, seq, LANES), lambda b, hh: (b, 0, hh)),
        out_shape=jax.ShapeDtypeStruct((bsz, seq, GROUP_WIDTH), BF16),
        compiler_params=_cparams(("parallel", "parallel")),
        name="retention",
    )(proj3, proj3, proj3, proj3, cos_full, sin_full, log_gamma)


def _out_proj_kernel(y0_ref, y1_ref, y2_ref, y3_ref, gg_ref, w_ref, x_ref, gate_ref, o_ref, h_ref):
    @pl.when(pl.program_id(1) == 0)
    def _():
        for grp, y_ref in enumerate((y0_ref, y1_ref, y2_ref, y3_ref)):
            yn = _rms(y_ref[...].astype(F32)) * gg_ref[grp:grp + 1, :]
            h_ref[:, grp * GROUP_WIDTH:(grp + 1) * GROUP_WIDTH] = yn.astype(h_ref.dtype)

    o_ref[...] = x_ref[...] + gate_ref[...] * _dot(h_ref[...], w_ref[...])


def out_proj_residual(ys, group_g, w_out, x2, gate, seq, tm=1024, tn=1024):
    t, d = x2.shape
    tm = min(tm, seq)
    per_b = seq // tm
    bsz = gate.shape[0]
    y_spec = pl.BlockSpec((tm, GROUP_WIDTH), lambda i, j: (i, 0))
    return pl.pallas_call(
        _out_proj_kernel,
        grid=(t // tm, d // tn),
        in_specs=[y_spec, y_spec, y_spec, y_spec,
                  pl.BlockSpec((4, GROUP_WIDTH), lambda i, j: (0, 0)),
                  pl.BlockSpec((4 * GROUP_WIDTH, tn), lambda i, j: (0, j)),
                  pl.BlockSpec((tm, tn), lambda i, j: (i, j)),
                  pl.BlockSpec((None, 1, tn), lambda i, j: (i // per_b, 0, j))],
        out_specs=pl.BlockSpec((tm, tn), lambda i, j: (i, j)),
        out_shape=jax.ShapeDtypeStruct((t, d), F32),
        scratch_shapes=[pltpu.VMEM((tm, 4 * GROUP_WIDTH), BF16)],
        compiler_params=_cparams(("parallel", "arbitrary")),
        name="out_proj_residual",
    )(*[y.reshape(t, GROUP_WIDTH) for y in ys], group_g, w_out, x2, gate.reshape(bsz, 1, d))


def _swiglu_act(h, w1_ref, w3_ref):
    a = _dot(h, w1_ref[...])
    return (a * _sigmoid(a) * _dot(h, w3_ref[...])).astype(BF16)


def _ffn_kernel(x_ref, g_ref, sh_ref, sc_ref, gate_ref, w1_ref, w3_ref, w2_ref, o_ref, h_ref):
    f = pl.program_id(1)

    @pl.when(f == 0)
    def _():
        _norm_mod_rows(x_ref, g_ref, sh_ref, sc_ref, h_ref)
        o_ref[...] = jnp.zeros_like(o_ref)

    o_ref[...] += _dot(_swiglu_act(h_ref[...], w1_ref, w3_ref), w2_ref[...])

    @pl.when(f == pl.num_programs(1) - 1)
    def _():
        o_ref[...] = x_ref[...] + gate_ref[...] * o_ref[...]


def dense_ffn_residual(x2, g, shift, scale, gate, w1, w3, w2, seq, tm=1024, tf=512):
    t, d = x2.shape
    ffn = w1.shape[1]
    tm = min(tm, seq)
    per_b = seq // tm
    bsz = gate.shape[0]
    row = lambda: pl.BlockSpec((None, 1, d), lambda i, f: (i // per_b, 0, 0))
    return pl.pallas_call(
        _ffn_kernel,
        grid=(t // tm, ffn // tf),
        in_specs=[pl.BlockSpec((tm, d), lambda i, f: (i, 0), pipeline_mode=pl.Buffered(1)),
                  pl.BlockSpec((1, d), lambda i, f: (0, 0)),
                  row(), row(), row(),
                  pl.BlockSpec((d, tf), lambda i, f: (0, f)),
                  pl.BlockSpec((d, tf), lambda i, f: (0, f)),
                  pl.BlockSpec((tf, d), lambda i, f: (f, 0))],
        out_specs=pl.BlockSpec((tm, d), lambda i, f: (i, 0)),
        out_shape=jax.ShapeDtypeStruct((t, d), F32),
        scratch_shapes=[pltpu.VMEM((tm, d), BF16)],
        compiler_params=_cparams(("parallel", "arbitrary")),
        name="dense_ffn_residual",
    )(x2, g.reshape(1, d), shift.reshape(bsz, 1, d), scale.reshape(bsz, 1, d),
      gate.reshape(bsz, 1, d), w1, w3, w2)


def _router_kernel(x_ref, g_ref, sh_ref, sc_ref, rw_ref, h_ref, idx_ref, gates_ref):
    _norm_mod_rows(x_ref, g_ref, sh_ref, sc_ref, h_ref)
    logits = jnp.dot(h_ref[...], rw_ref[...], precision=lax.Precision.HIGHEST,
                     preferred_element_type=F32)
    lane = lax.broadcasted_iota(jnp.int32, logits.shape, 1)
    lane_f = lane.astype(F32)
    logits = jnp.where(lane < N_EXPERTS, logits, -jnp.inf)
    m0 = jnp.max(logits, axis=-1, keepdims=True)
    e0 = jnp.min(jnp.where(logits == m0, lane_f, float(LANES)), axis=-1, keepdims=True)
    rest = jnp.where(lane_f == e0, -jnp.inf, logits)
    m1 = jnp.max(rest, axis=-1, keepdims=True)
    e1 = jnp.min(jnp.where(rest == m1, lane_f, float(LANES)), axis=-1, keepdims=True)
    p1 = jnp.exp(m1 - m0)
    g0 = 1.0 / (1.0 + p1)
    idx_ref[...] = jnp.where(lane == 0, e0, jnp.where(lane == 1, e1, 0.0)).astype(jnp.int32)
    gates_ref[...] = jnp.where(lane == 0, g0, jnp.where(lane == 1, p1 * g0, 0.0))


def moe_router(x2, g, shift, scale, router_w, seq, tm=256):
    t, d = x2.shape
    tm = min(tm, seq)
    per_b = seq // tm
    bsz = shift.shape[0]
    rw = jnp.zeros((d, LANES), F32).at[:, :N_EXPERTS].set(router_w)
    row = lambda: pl.BlockSpec((None, 1, d), lambda i: (i // per_b, 0, 0))
    return pl.pallas_call(
        _router_kernel,
        grid=(t // tm,),
        in_specs=[pl.BlockSpec((tm, d), lambda i: (i, 0)),
                  pl.BlockSpec((1, d), lambda i: (0, 0)),
                  row(), row(),
                  pl.BlockSpec((d, LANES), lambda i: (0, 0))],
        out_specs=[pl.BlockSpec((tm, d), lambda i: (i, 0)),
                   pl.BlockSpec((tm, LANES), lambda i: (i, 0)),
                   pl.BlockSpec((tm, LANES), lambda i: (i, 0))],
        out_shape=[jax.ShapeDtypeStruct((t, d), F32),
                   jax.ShapeDtypeStruct((t, LANES), jnp.int32),
                   jax.ShapeDtypeStruct((t, LANES), F32)],
        compiler_params=_cparams(("parallel",)),
        name="moe_router",
    )(x2, g.reshape(1, d), shift.reshape(bsz, 1, d), scale.reshape(bsz, 1, d), rw)


GATHER_UNROLL = 8


def _row_copy(src_ref, dst_ref, sem, src_row, dst_row):
    return pltpu.make_async_copy(src_ref.at[pl.ds(src_row, 1), :],
                                 dst_ref.at[pl.ds(dst_row, 1), :], sem)


def _start_row_gather(src_ref, dst_ref, sem, row_of, n_rows):
    def start(r, _):
        _row_copy(src_ref, dst_ref, sem, row_of(r), r).start()
        return 0

    lax.fori_loop(0, n_rows, start, 0, unroll=GATHER_UNROLL)


def _wait_row_gather(dst_ref, sem):
    pltpu.make_async_copy(dst_ref, dst_ref, sem).wait()


def _moe_ffn_kernel(tile_e_ref, tile_ok_ref, buf_t_ref, h_hbm, w1_ref, w3_ref, w2_ref, o_ref,
                    hf_ref, hb_ref, sem):
    i = pl.program_id(0)
    f = pl.program_id(1)
    n_tiles = pl.num_programs(0)
    tm = hb_ref.shape[0]
    ok = tile_ok_ref[i] > 0

    def start_gather(tile):
        _start_row_gather(h_hbm, hf_ref, sem, lambda r: buf_t_ref[tile * tm + r], tm)

    @pl.when(ok & (f == 0))
    def _():
        @pl.when(i == 0)
        def _():
            start_gather(0)

        _wait_row_gather(hf_ref, sem)
        hb_ref[...] = hf_ref[...].astype(hb_ref.dtype)
        o_ref[...] = jnp.zeros_like(o_ref)

        nxt = jnp.minimum(i + 1, n_tiles - 1)

        @pl.when((i + 1 < n_tiles) & (tile_ok_ref[nxt] > 0))
        def _():
            start_gather(i + 1)

    @pl.when(ok)
    def _():
        o_ref[...] += _dot(_swiglu_act(hb_ref[...], w1_ref, w3_ref), w2_ref[...])

    @pl.when(jnp.logical_not(ok) & (f == pl.num_programs(1) - 1))
    def _():
        o_ref[...] = jnp.zeros_like(o_ref)


def moe_expert_ffn(h2, tile_e, tile_ok, buf_t, w1, w3, w2, tm, tf=1408):
    t, d = h2.shape
    ffn = w1.shape[2]
    nf = ffn // tf
    n_tiles = tile_e.shape[0]

    def fsel(i, f, ok):
        return jnp.where(ok[i] > 0, f, nf - 1)

    grid_spec = pltpu.PrefetchScalarGridSpec(
        num_scalar_prefetch=3,
        grid=(n_tiles, nf),
        in_specs=[pl.BlockSpec(memory_space=pl.ANY),
                  pl.BlockSpec((None, d, tf), lambda i, f, te, ok, bt: (te[i], 0, fsel(i, f, ok))),
                  pl.BlockSpec((None, d, tf), lambda i, f, te, ok, bt: (te[i], 0, fsel(i, f, ok))),
                  pl.BlockSpec((None, tf, d), lambda i, f, te, ok, bt: (te[i], fsel(i, f, ok), 0))],
        out_specs=pl.BlockSpec((tm, d), lambda i, f, te, ok, bt: (i, 0)),
        scratch_shapes=[pltpu.VMEM((tm, d), F32), pltpu.VMEM((tm, d), BF16),
                        pltpu.SemaphoreType.DMA(())],
    )
    return pl.pallas_call(
        _moe_ffn_kernel,
        grid_spec=grid_spec,
        out_shape=jax.ShapeDtypeStruct((n_tiles * tm, d), F32),
        compiler_params=_cparams(("arbitrary", "arbitrary"), VMEM_LIMIT_MOE),
        name="moe_expert_ffn",
    )(tile_e, tile_ok, buf_t, h2, w1, w3, w2)


def _moe_combine_kernel(d0_ref, d1_ref, y_hbm, x_ref, gate_ref, gates_ref, o_ref,
                        y0_ref, y1_ref, sems0, sems1):
    i = pl.program_id(0)
    tm = x_ref.shape[0]
    slot = i % 2

    def start_gather(tile, into):
        def start(r, _):
            _row_copy(y_hbm, y0_ref.at[into], sems0.at[into], d0_ref[tile * tm + r], r).start()
            _row_copy(y_hbm, y1_ref.at[into], sems1.at[into], d1_ref[tile * tm + r], r).start()
            return 0

        lax.fori_loop(0, tm, start, 0, unroll=GATHER_UNROLL)

    @pl.when(i == 0)
    def _():
        start_gather(0, 0)

    @pl.when(i + 1 < pl.num_programs(0))
    def _():
        start_gather(i + 1, 1 - slot)

    _wait_row_gather(y0_ref.at[slot], sems0.at[slot])
    _wait_row_gather(y1_ref.at[slot], sems1.at[slot])
    gates = gates_ref[...]
    y = gates[:, 0:1] * y0_ref[slot] + gates[:, 1:2] * y1_ref[slot]
    o_ref[...] = x_ref[...] + gate_ref[...] * y


def moe_combine_residual(dest0, dest1, y_buf, x2, gate, gates, seq, tm=256):
    t, d = x2.shape
    tm = min(tm, seq)
    per_b = seq // tm
    bsz = gate.shape[0]
    grid_spec = pltpu.PrefetchScalarGridSpec(
        num_scalar_prefetch=2,
        grid=(t // tm,),
        in_specs=[pl.BlockSpec(memory_space=pl.ANY),
                  pl.BlockSpec((tm, d), lambda i, d0, d1: (i, 0)),
                  pl.BlockSpec((None, 1, d), lambda i, d0, d1: (i // per_b, 0, 0)),
                  pl.BlockSpec((tm, LANES), lambda i, d0, d1: (i, 0))],
        out_specs=pl.BlockSpec((tm, d), lambda i, d0, d1: (i, 0)),
        scratch_shapes=[pltpu.VMEM((2, tm, d), F32), pltpu.VMEM((2, tm, d), F32),
                        pltpu.SemaphoreType.DMA((2,)), pltpu.SemaphoreType.DMA((2,))],
    )
    return pl.pallas_call(
        _moe_combine_kernel,
        grid_spec=grid_spec,
        out_shape=jax.ShapeDtypeStruct((t, d), F32),
        compiler_params=_cparams(("arbitrary",)),
        name="moe_combine_residual",
    )(dest0, dest1, y_buf, x2, gate.reshape(bsz, 1, d), gates)


def _moe_routing_tables(idx, tm):
    t = idx.shape[0]
    n_slots = t * MOE_TOPK
    n_tiles = n_slots // tm + N_EXPERTS
    flat_e = idx[:, :MOE_TOPK].reshape(-1)
    onehot = (flat_e[:, None] == jnp.arange(N_EXPERTS)[None, :]).astype(jnp.int32)
    counts = jnp.sum(onehot, axis=0)
    rank = jnp.sum((jnp.cumsum(onehot, axis=0) - onehot) * onehot, axis=1)
    padded = (counts + tm - 1) // tm * tm
    pad_ends = jnp.cumsum(padded)
    pad_starts = pad_ends - padded
    dest = (pad_starts[flat_e] + rank).astype(jnp.int32)
    buf_t = jnp.zeros((n_tiles * tm,), jnp.int32).at[dest].set(
        (jnp.arange(n_slots) // MOE_TOPK).astype(jnp.int32))
    tile_start = jnp.arange(n_tiles) * tm
    tile_e = jnp.minimum(jnp.searchsorted(pad_ends, tile_start, side="right"),
                         N_EXPERTS - 1).astype(jnp.int32)
    tile_ok = (tile_start < pad_ends[-1]).astype(jnp.int32)
    dest2 = dest.reshape(t, MOE_TOPK)
    return tile_e, tile_ok, buf_t, dest2[:, 0], dest2[:, 1]


def moe_ffn_residual(x2, g, shift, scale, gate, router_w, w1, w3, w2, seq, tile_rows=512):
    h2, idx, gates = moe_router(x2, g, shift, scale, router_w, seq)
    tile_e, tile_ok, buf_t, dest0, dest1 = _moe_routing_tables(idx, tile_rows)
    y_buf = moe_expert_ffn(h2, tile_e, tile_ok, buf_t, w1, w3, w2, tile_rows)
    return moe_combine_residual(dest0, dest1, y_buf, x2, gate, gates, seq)


def _pad_w_in(w_in):
    d = w_in.shape[0]
    kpe_end = MLA_Q_RANK + MLA_KV_RANK + MLA_ROPE_DIM
    tail = PROJ_WIDTH - (COL_MOBA + 3 * GROUP_WIDTH)
    return jnp.concatenate([
        w_in[:, :kpe_end], jnp.zeros((d, LANES - MLA_ROPE_DIM), w_in.dtype),
        w_in[:, kpe_end:], jnp.zeros((d, tail), w_in.dtype)], axis=1).astype(BF16)


def _pad_heads_192(a):
    lead = a.shape[:-1]
    a = a.reshape(lead + (GROUP_HEADS, MLA_QK_DIM))
    a = jnp.pad(a, [(0, 0)] * len(lead) + [(0, 0), (0, MLA_QK_PAD - MLA_QK_DIM)])
    return a.reshape(lead + (GROUP_HEADS * MLA_QK_PAD,))


def kernel(x, c, positions, ada_w, ada_b, norm_mix_g, norm_ffn_g, w_in, mla_q_norm_g, mla_kv_norm_g, mla_w_uq, mla_w_ukv, mla_q_head_g, mla_k_head_g, moba_q_head_g, moba_k_head_g, group_norm_g, w_out, ffn_w1, ffn_w3, ffn_w2, router_w, moe_w1, moe_w3, moe_w2):
    bsz, seq, d = x.shape
    depth = ada_w.shape[0]
    cos_pe, sin_pe = _rope_tables(positions, MLA_ROPE_DIM)
    cos_full, sin_full = _rope_tables(positions, HEAD_DIM)
    mod = ada_modulation(c, ada_w, ada_b)
    x2 = x.reshape(bsz * seq, d)
    for l in range(depth):
        shift_m, scale_m, gate_m, shift_f, scale_f, gate_f = jnp.split(mod[l], 6, axis=-1)
        proj = norm_mod_proj(x2, norm_mix_g[l], shift_m, scale_m, _pad_w_in(w_in[l]), seq)
        proj3 = proj.reshape(bsz, seq, PROJ_WIDTH)
        pad_g = lambda g: jnp.pad(g, (0, MLA_QK_PAD - MLA_QK_DIM)).reshape(1, MLA_QK_PAD)
        q_mla, k_mla, v_mla = mla_prep(
            proj3, mla_q_norm_g[l], mla_kv_norm_g[l],
            _pad_heads_192(mla_w_uq[l]).astype(BF16), mla_w_ukv[l].astype(BF16),
            pad_g(mla_q_head_g[l]), pad_g(mla_k_head_g[l]), cos_pe, sin_pe)
        y_mla = mla_attention(q_mla, k_mla, v_mla)
        y_ret = retention(proj3, cos_full, sin_full)
        y_sb = sb_attention(proj3)
        q_mb, k_mb, kmean = moba_prep(proj3, moba_q_head_g[l], moba_k_head_g[l], cos_full, sin_full)
        y_moba = moba_attention(q_mb, k_mb, kmean, proj3)
        x2 = out_proj_residual((y_mla, y_ret, y_sb, y_moba), group_norm_g[l],
                               w_out[l].astype(BF16), x2, gate_m, seq)
        j = l // 2
        if l % 2 == 0:
            x2 = dense_ffn_residual(x2, norm_ffn_g[l], shift_f, scale_f, gate_f,
                                    ffn_w1[j].astype(BF16), ffn_w3[j].astype(BF16),
                                    ffn_w2[j].astype(BF16), seq)
        else:
            x2 = moe_ffn_residual(x2, norm_ffn_g[l], shift_f, scale_f, gate_f, router_w[j],
                                  moe_w1[j].astype(BF16), moe_w3[j].astype(BF16),
                                  moe_w2[j].astype(BF16), seq)
    return x2.reshape(bsz, seq, d)
```

```python
import functools

import jax
import jax.numpy as jnp
from jax import lax
from jax.experimental import pallas as pl
from jax.experimental.pallas import tpu as pltpu

F32 = jnp.float32
BF16 = jnp.bfloat16

HEAD_DIM = 128
GROUP_HEADS = 4
GROUP_WIDTH = 512
MLA_Q_RANK = 512
MLA_KV_RANK = 256
MLA_NOPE_DIM = 128
MLA_ROPE_DIM = 64
MLA_QK_DIM = MLA_NOPE_DIM + MLA_ROPE_DIM
MLA_QK_PAD = 256
RET_CHUNK = 128
MOBA_BLOCK = 256
MOBA_TOPK = 3
ROPE_THETA = 10000.0
NORM_EPS = 1e-6
NEG = -1e30
N_EXPERTS = 8
MOE_TOPK = 2

LANES = 128
ATTN_BLOCK = 256
VMEM_LIMIT = 56 * 1024 * 1024
VMEM_LIMIT_MOE = 60 * 1024 * 1024

COL_CQ = 0
COL_CKV = 512
COL_KPE = 768
COL_RET = 896
COL_SB = COL_RET + 4 * GROUP_WIDTH
COL_MOBA = COL_SB + 3 * GROUP_WIDTH
PROJ_WIDTH = 6144


def _cparams(sem, vmem=VMEM_LIMIT):
    return pltpu.CompilerParams(dimension_semantics=sem, vmem_limit_bytes=vmem)


def _dot(a, b):
    return jnp.dot(a, b, preferred_element_type=F32)


def _dot_nt(a, b):
    return lax.dot_general(a, b, (((1,), (1,)), ((), ())), preferred_element_type=F32)


def _sigmoid(x):
    return 1.0 / (1.0 + jnp.exp(-x))


def _softplus(z):
    return jnp.maximum(z, 0.0) + jnp.log(1.0 + jnp.exp(-jnp.abs(z)))


def _rms(xf, width=None):
    width = xf.shape[-1] if width is None else width
    ss = jnp.sum(xf * xf, axis=-1, keepdims=True) * (1.0 / width)
    return xf * lax.rsqrt(ss + NORM_EPS)


def _ada_kernel(c_ref, w_ref, b_ref, o_ref):
    c = c_ref[...]
    cond = c * _sigmoid(c)
    o_ref[...] = _dot(cond.astype(BF16), w_ref[...].astype(BF16)) + b_ref[...]


def ada_modulation(c, ada_w, ada_b, tn=1024):
    depth, d, n = ada_w.shape
    b = c.shape[0]
    return pl.pallas_call(
        _ada_kernel,
        grid=(depth, n // tn),
        in_specs=[
            pl.BlockSpec((b, d), lambda l, j: (0, 0)),
            pl.BlockSpec((None, d, tn), lambda l, j: (l, 0, j)),
            pl.BlockSpec((None, 1, tn), lambda l, j: (l, 0, j)),
        ],
        out_specs=pl.BlockSpec((None, b, tn), lambda l, j: (l, 0, j)),
        out_shape=jax.ShapeDtypeStruct((depth, b, n), F32),
        compiler_params=_cparams(("parallel", "parallel")),
        name="ada_modulation",
    )(c, ada_w, ada_b.reshape(depth, 1, n))


def _norm_mod_rows(x_ref, g_ref, sh_ref, sc_ref, dst_ref, rows=32):
    tm = x_ref.shape[0]
    mul = g_ref[...] * (1.0 + sc_ref[...])
    sh = sh_ref[...]

    def body(r, _):
        sl = pl.ds(pl.multiple_of(r * rows, rows), rows)
        dst_ref[sl, :] = (_rms(x_ref[sl, :]) * mul + sh).astype(dst_ref.dtype)
        return 0

    lax.fori_loop(0, tm // rows, body, 0, unroll=2)


def _proj_kernel(x_ref, g_ref, sh_ref, sc_ref, w_ref, o_ref, h_ref):
    @pl.when(pl.program_id(1) == 0)
    def _():
        _norm_mod_rows(x_ref, g_ref, sh_ref, sc_ref, h_ref)

    o_ref[...] = _dot(h_ref[...], w_ref[...]).astype(o_ref.dtype)


def norm_mod_proj(x2, g, shift, scale, w, seq, tm=1024, tn=2048):
    t, d = x2.shape
    n = w.shape[1]
    tm = min(tm, seq)
    per_b = seq // tm
    bsz = shift.shape[0]
    return pl.pallas_call(
        _proj_kernel,
        grid=(t // tm, n // tn),
        in_specs=[
            pl.BlockSpec((tm, d), lambda i, j: (i, 0)),
            pl.BlockSpec((1, d), lambda i, j: (0, 0)),
            pl.BlockSpec((None, 1, d), lambda i, j: (i // per_b, 0, 0)),
            pl.BlockSpec((None, 1, d), lambda i, j: (i // per_b, 0, 0)),
            pl.BlockSpec((d, tn), lambda i, j: (0, j)),
        ],
        out_specs=pl.BlockSpec((tm, tn), lambda i, j: (i, j)),
        out_shape=jax.ShapeDtypeStruct((t, n), BF16),
        scratch_shapes=[pltpu.VMEM((tm, d), BF16)],
        compiler_params=_cparams(("parallel", "arbitrary")),
        name="norm_mod_proj",
    )(x2, g.reshape(1, d), shift.reshape(bsz, 1, d), scale.reshape(bsz, 1, d), w)


def _rope_tables(positions, dim):
    inv_freq = ROPE_THETA ** (-jnp.arange(0, dim, 2, dtype=F32) / dim)
    ang = positions.astype(F32)[:, None] * inv_freq[None, :]
    cos, sin = jnp.cos(ang), jnp.sin(ang)
    pad = jnp.zeros((positions.shape[0], LANES - dim), F32)
    return (jnp.concatenate([cos, cos, pad], axis=-1),
            jnp.concatenate([-sin, sin, pad], axis=-1))


def _rope_full(z, cos2, sin_s):
    return z * cos2 + pltpu.roll(z, 64, 1) * sin_s


def _rope_64(z, cos2, sin_s):
    lane = lax.broadcasted_iota(jnp.int32, z.shape, 1)
    partner = jnp.where(lane < 32, pltpu.roll(z, 96, 1), pltpu.roll(z, 32, 1))
    return z * cos2 + partner * sin_s


def _mla_prep_kernel(p_ref, qg_ref, kvg_ref, wuq_ref, wukv_ref, qhg_ref, khg_ref,
                     cos_ref, sin_ref, q_ref, k_ref, v_ref):
    p = p_ref[...].astype(F32)
    c_q = p[:, COL_CQ:COL_CQ + MLA_Q_RANK]
    c_kv = p[:, COL_CKV:COL_CKV + MLA_KV_RANK]
    k_pe = p[:, COL_KPE:COL_KPE + LANES]
    q = _dot((_rms(c_q) * qg_ref[...]).astype(BF16), wuq_ref[...])
    kv = _dot((_rms(c_kv) * kvg_ref[...]).astype(BF16), wukv_ref[...])
    cos2, sin_s = cos_ref[...], sin_ref[...]
    qhg, khg = qhg_ref[...], khg_ref[...]
    scale = MLA_QK_DIM ** -0.5
    pe_ss = jnp.sum(k_pe * k_pe, axis=-1, keepdims=True)
    for h in range(GROUP_HEADS):
        qh = q[:, h * MLA_QK_PAD:(h + 1) * MLA_QK_PAD]
        qh = _rms(qh, MLA_QK_DIM) * qhg * scale
        q_ref[h, :, :LANES] = qh[:, :LANES].astype(q_ref.dtype)
        q_ref[h, :, LANES:] = _rope_64(qh[:, LANES:], cos2, sin_s).astype(q_ref.dtype)
        k_nope = kv[:, 2 * h * LANES:(2 * h + 1) * LANES]
        ss = (jnp.sum(k_nope * k_nope, axis=-1, keepdims=True) + pe_ss) * (1.0 / MLA_QK_DIM)
        r = lax.rsqrt(ss + NORM_EPS)
        k_ref[h, :, :LANES] = (k_nope * r * khg[:, :LANES]).astype(k_ref.dtype)
        k_ref[h, :, LANES:] = _rope_64(k_pe * r * khg[:, LANES:], cos2, sin_s).astype(k_ref.dtype)
        v_ref[h] = kv[:, (2 * h + 1) * LANES:(2 * h + 2) * LANES].astype(v_ref.dtype)


def mla_prep(proj3, q_norm_g, kv_norm_g, w_uq_pad, w_ukv, q_head_g_pad, k_head_g_pad,
             cos_pe, sin_pe, tm=512):
    bsz, seq, _ = proj3.shape
    h = GROUP_HEADS
    n_in = COL_KPE + LANES
    const = lambda shape: pl.BlockSpec(shape, lambda b, i: (0,) * len(shape))
    return pl.pallas_call(
        _mla_prep_kernel,
        grid=(bsz, seq // tm),
        in_specs=[
            pl.BlockSpec((None, tm, n_in), lambda b, i: (b, i, 0)),
            const((1, MLA_Q_RANK)), const((1, MLA_KV_RANK)),
            const((MLA_Q_RANK, h * MLA_QK_PAD)), const((MLA_KV_RANK, h * 2 * LANES)),
            const((1, MLA_QK_PAD)), const((1, MLA_QK_PAD)),
            pl.BlockSpec((tm, LANES), lambda b, i: (i, 0)),
            pl.BlockSpec((tm, LANES), lambda b, i: (i, 0)),
        ],
        out_specs=[
            pl.BlockSpec((None, h, tm, MLA_QK_PAD), lambda b, i: (b, 0, i, 0)),
            pl.BlockSpec((None, h, tm, MLA_QK_PAD), lambda b, i: (b, 0, i, 0)),
            pl.BlockSpec((None, h, tm, LANES), lambda b, i: (b, 0, i, 0)),
        ],
        out_shape=[
            jax.ShapeDtypeStruct((bsz, h, seq, MLA_QK_PAD), BF16),
            jax.ShapeDtypeStruct((bsz, h, seq, MLA_QK_PAD), BF16),
            jax.ShapeDtypeStruct((bsz, h, seq, LANES), BF16),
        ],
        compiler_params=_cparams(("parallel", "parallel")),
        name="mla_prep",
    )(proj3, q_norm_g.reshape(1, -1), kv_norm_g.reshape(1, -1), w_uq_pad, w_ukv,
      q_head_g_pad, k_head_g_pad, cos_pe, sin_pe)


def _moba_gates(q_ref, kmean_ref):
    km = kmean_ref[...]
    nb = km.shape[0]
    hi = km.astype(BF16).astype(F32)
    mid = (km - hi).astype(BF16).astype(F32)
    lo = (km - hi - mid).astype(BF16).astype(F32)
    pieces = jnp.concatenate([hi, mid, lo, jnp.zeros_like(km)], axis=0).astype(BF16)
    r = _dot_nt(pieces, q_ref[...])
    return r[:nb] + r[nb:2 * nb] + r[2 * nb:3 * nb]


def _moba_block_choice(gate, n_past):
    blk = lax.broadcasted_iota(jnp.int32, gate.shape, 0)
    rank = jnp.zeros(gate.shape, jnp.int32)
    for o in range(n_past):
        g_o = gate[o:o + 1, :]
        tie = jnp.where(o < blk, 1, 0)
        rank = rank + jnp.where(g_o > gate, 1, jnp.where(g_o == gate, tie, 0))
    return jnp.where(rank < MOBA_TOPK, 1.0, 0.0)


def _attn_kernel(*refs, moba):
    if moba:
        q_ref, k_ref, v_ref, kmean_ref, o_ref = refs
        gates = _moba_gates(q_ref, kmean_ref)
    else:
        q_ref, k_ref, v_ref, o_ref = refs
    seq = q_ref.shape[0]
    t = ATTN_BLOCK
    v_t = jnp.transpose(v_ref[...].astype(F32)).astype(BF16)
    key = lax.broadcasted_iota(jnp.int32, (t, t), 0)
    qry = lax.broadcasted_iota(jnp.int32, (t, t), 1)
    causal = key <= qry
    for i in range(seq // t):
        own = slice(i * t, (i + 1) * t)
        q = q_ref[own, :]
        s_own = jnp.where(causal, _dot_nt(k_ref[own, :], q), NEG)
        m = jnp.max(s_own, axis=0, keepdims=True)
        s_past = []
        if i > 0:
            s_all = _dot_nt(k_ref[:i * t, :], q)
            if moba and i > MOBA_TOPK:
                sel = _moba_block_choice(gates[:, own], i)
                s_past = [jnp.where(sel[n:n + 1, :] > 0.5, s_all[n * t:(n + 1) * t, :], NEG)
                          for n in range(i)]
            else:
                s_past = [s_all]
            for s in s_past:
                m = jnp.maximum(m, jnp.max(s, axis=0, keepdims=True))
        p = jnp.exp(s_own - m)
        l = jnp.sum(p, axis=0, keepdims=True)
        acc_t = _dot(v_t[:, own], p.astype(BF16))
        start = 0
        for s in s_past:
            p = jnp.exp(s - m)
            l = l + jnp.sum(p, axis=0, keepdims=True)
            acc_t = acc_t + _dot(v_t[:, start:start + s.shape[0]], p.astype(BF16))
            start += s.shape[0]
        o_ref[own, :] = jnp.transpose(acc_t / l).astype(o_ref.dtype)


def _head_spec(seq, d):
    return pl.BlockSpec((None, None, seq, d), lambda b, hh: (b, hh, 0, 0))


def _proj_col_spec(seq, col):
    cb = col // LANES
    return pl.BlockSpec((None, seq, LANES), lambda b, hh: (b, 0, cb + hh))


def _attention(q, k, v, v_spec, kmean=None):
    bsz, h, seq, dk = q.shape
    in_specs = [_head_spec(seq, dk), _head_spec(seq, dk), v_spec]
    args = [q, k, v]
    if kmean is not None:
        in_specs.append(_head_spec(kmean.shape[2], LANES))
        args.append(kmean)
    return pl.pallas_call(
        functools.partial(_attn_kernel, moba=kmean is not None),
        grid=(bsz, h),
        in_specs=in_specs,
        out_specs=pl.BlockSpec((None, seq, LANES), lambda b, hh: (b, 0, hh)),
        out_shape=jax.ShapeDtypeStruct((bsz, seq, h * LANES), BF16),
        compiler_params=_cparams(("parallel", "parallel")),
        name="moba_attention" if kmean is not None else "mla_attention",
    )(*args)


def mla_attention(q, k, v):
    return _attention(q, k, v, _head_spec(q.shape[2], LANES))


def _moba_prep_kernel(q_ref, k_ref, qg_ref, kg_ref, cos_ref, sin_ref, qo_ref, ko_ref, km_ref):
    seq = q_ref.shape[0]
    scale = HEAD_DIM ** -0.5
    for blk in range(seq // MOBA_BLOCK):
        sl = slice(blk * MOBA_BLOCK, (blk + 1) * MOBA_BLOCK)
        cos2, sin_s = cos_ref[sl, :], sin_ref[sl, :]
        qn = _rms(q_ref[sl, :].astype(F32)) * qg_ref[...]
        qo_ref[sl, :] = (_rope_full(qn, cos2, sin_s) * scale).astype(qo_ref.dtype)
        kn = _rope_full(_rms(k_ref[sl, :].astype(F32)) * kg_ref[...], cos2, sin_s)
        ko_ref[sl, :] = kn.astype(ko_ref.dtype)
        km_ref[blk:blk + 1, :] = jnp.mean(kn, axis=0, keepdims=True)


def moba_prep(proj3, q_head_g, k_head_g, cos_full, sin_full):
    bsz, seq, _ = proj3.shape
    h = GROUP_HEADS
    nb = seq // MOBA_BLOCK
    const = lambda shape: pl.BlockSpec(shape, lambda b, hh: (0,) * len(shape))
    return pl.pallas_call(
        _moba_prep_kernel,
        grid=(bsz, h),
        in_specs=[
            _proj_col_spec(seq, COL_MOBA),
            _proj_col_spec(seq, COL_MOBA + GROUP_WIDTH),
            const((1, LANES)), const((1, LANES)),
            const((seq, LANES)), const((seq, LANES)),
        ],
        out_specs=[_head_spec(seq, LANES), _head_spec(seq, LANES), _head_spec(nb, LANES)],
        out_shape=[
            jax.ShapeDtypeStruct((bsz, h, seq, LANES), BF16),
            jax.ShapeDtypeStruct((bsz, h, seq, LANES), BF16),
            jax.ShapeDtypeStruct((bsz, h, nb, LANES), F32),
        ],
        compiler_params=_cparams(("parallel", "parallel")),
        name="moba_prep",
    )(proj3, proj3, q_head_g.reshape(1, -1), k_head_g.reshape(1, -1), cos_full, sin_full)


def moba_attention(q, k, kmean, proj3):
    v_spec = _proj_col_spec(q.shape[2], COL_MOBA + 2 * GROUP_WIDTH)
    return _attention(q, k, proj3, v_spec, kmean=kmean)


SB_DEAD_TAIL = -104.0


def _sb_tile(q, k_blk, v_blk, tail, later, strict):
    z = _dot_nt(q, k_blk) * (HEAD_DIM ** -0.5)
    log_1m = -_softplus(z)
    if strict is not None:
        log_1m = jnp.where(strict, log_1m, 0.0)
    hi = log_1m.astype(BF16)
    lo = (log_1m - hi.astype(F32)).astype(BF16)
    suffix = _dot(hi, later) + _dot(lo, later)
    a = jnp.exp(z + log_1m + suffix + tail)
    if strict is not None:
        a = jnp.where(strict, a, 0.0)
    return _dot(a.astype(v_blk.dtype), v_blk), jnp.sum(log_1m, axis=-1, keepdims=True)


def _sb_kernel(q_ref, k_ref, v_ref, o_ref, acc_ref, tail_ref):
    seq = q_ref.shape[0]
    t = ATTN_BLOCK
    row = lax.broadcasted_iota(jnp.int32, (t, t), 0)
    col = lax.broadcasted_iota(jnp.int32, (t, t), 1)
    strict = col < row
    later = jnp.where(row > col, 1.0, 0.0).astype(BF16)
    blk = lambda n: slice(n * t, (n + 1) * t)

    for i in range(seq // t):
        q = q_ref[blk(i), :]
        acc, tail = _sb_tile(q, k_ref[blk(i), :], v_ref[blk(i), :], 0.0, later, strict)
        if i > 0:
            y, s = _sb_tile(q, k_ref[blk(i - 1), :], v_ref[blk(i - 1), :], tail, later, None)
            acc, tail = acc + y, tail + s
        acc_ref[blk(i), :] = acc
        tail_ref[blk(i), :] = tail

    for i in range(2, seq // t):
        alive = jnp.max(tail_ref[blk(i), :], axis=0, keepdims=True)[0, 0] > SB_DEAD_TAIL

        @pl.when(alive)
        def _():
            q = q_ref[blk(i), :]
            acc, tail = acc_ref[blk(i), :], tail_ref[blk(i), :]
            for n in range(i - 2, -1, -1):
                y, s = _sb_tile(q, k_ref[blk(n), :], v_ref[blk(n), :], tail, later, None)
                acc, tail = acc + y, tail + s
            acc_ref[blk(i), :] = acc

    o_ref[...] = acc_ref[...].astype(o_ref.dtype)


def sb_attention(proj3):
    bsz, seq, _ = proj3.shape
    return pl.pallas_call(
        _sb_kernel,
        grid=(bsz, GROUP_HEADS),
        in_specs=[
            _proj_col_spec(seq, COL_SB),
            _proj_col_spec(seq, COL_SB + GROUP_WIDTH),
            _proj_col_spec(seq, COL_SB + 2 * GROUP_WIDTH),
        ],
        out_specs=pl.BlockSpec((None, seq, LANES), lambda b, hh: (b, 0, hh)),
        out_shape=jax.ShapeDtypeStruct((bsz, seq, GROUP_WIDTH), BF16),
        scratch_shapes=[pltpu.VMEM((seq, LANES), F32), pltpu.VMEM((seq, 1), F32)],
        compiler_params=_cparams(("parallel", "parallel")),
        name="sb_attention",
    )(proj3, proj3, proj3)


def _ret_kernel(q_ref, k_ref, v_ref, g_ref, cos_ref, sin_ref, lg_ref, o_ref):
    seq = q_ref.shape[0]
    c = RET_CHUNK
    log_gamma = lg_ref[...]
    ri = lax.broadcasted_iota(jnp.int32, (c, c), 0).astype(F32)
    ci = lax.broadcasted_iota(jnp.int32, (c, c), 1).astype(F32)
    rel = ri - ci
    intra_decay = jnp.where(rel >= 0, jnp.exp(jnp.maximum(rel, 0.0) * log_gamma), 0.0)
    idx = lax.broadcasted_iota(jnp.int32, (c, 1), 0).astype(F32)
    query_decay = jnp.exp((idx + 1.0) * log_gamma)
    key_decay = jnp.exp((c - 1.0 - idx) * log_gamma)
    chunk_decay = jnp.exp(c * log_gamma)
    k_scale = HEAD_DIM ** -0.5

    state = jnp.zeros((HEAD_DIM, HEAD_DIM), F32)
    for n in range(seq // c):
        sl = slice(n * c, (n + 1) * c)
        cos2, sin_s = cos_ref[sl, :], sin_ref[sl, :]
        q = _rope_full(q_ref[sl, :].astype(F32), cos2, sin_s)
        k = _rope_full(k_ref[sl, :].astype(F32), cos2, sin_s) * k_scale
        vb = v_ref[sl, :]
        qb = q.astype(BF16)
        scores = _dot_nt(qb, k.astype(BF16)) * intra_decay
        y = _dot(scores.astype(BF16), vb)
        y = y + _dot(qb, state.astype(BF16)) * query_decay
        kd_t = jnp.transpose(k * key_decay).astype(BF16)
        state = state * chunk_decay + _dot(kd_t, vb)
        mu = jnp.mean(y, axis=-1, keepdims=True)
        yc = y - mu
        var = jnp.mean(yc * yc, axis=-1, keepdims=True)
        yn = yc * lax.rsqrt(var + NORM_EPS)
        g = g_ref[sl, :].astype(F32)
        o_ref[sl, :] = (g * _sigmoid(g) * yn).astype(o_ref.dtype)


def retention(proj3, cos_full, sin_full):
    bsz, seq, _ = proj3.shape
    h = GROUP_HEADS
    log_gamma = jnp.log(1.0 - 2.0 ** (-5.0 - jnp.arange(h, dtype=F32))).reshape(h, 1, 1)
    col = lambda k: pl.BlockSpec((None, seq, LANES),
                                 lambda b, hh: (b, 0, (COL_RET + k * GROUP_WIDTH) // LANES + hh))
    const = lambda shape: pl.BlockSpec(shape, lambda b, hh: (0,) * len(shape))
    return pl.pallas_call(
        _ret_kernel,
        grid=(bsz, h),
        in_specs=[col(0), col(1), col(2), col(3), const((seq, LANES)), const((seq, LANES)),
                  pl.BlockSpec((None, 1, 1), lambda b, hh: (hh, 0, 0))],
        out_specs=pl.BlockSpec((None, seq, LANES), lambda b, hh: (b, 0, hh)),
        out_shape=jax.ShapeDtypeStruct((bsz, seq, GROUP_WIDTH), BF16),
        compiler_params=_cparams(("parallel", "parallel")),
        name="retention",
    )(proj3, proj3, proj3, proj3, cos_full, sin_full, log_gamma)


def _out_proj_kernel(y0_ref, y1_ref, y2_ref, y3_ref, gg_ref, w_ref, x_ref, gate_ref, o_ref, h_ref):
    @pl.when(pl.program_id(1) == 0)
    def _():
        for grp, y_ref in enumerate((y0_ref, y1_ref, y2_ref, y3_ref)):
            yn = _rms(y_ref[...].astype(F32)) * gg_ref[grp:grp + 1, :]
            h_ref[:, grp * GROUP_WIDTH:(grp + 1) * GROUP_WIDTH] = yn.astype(h_ref.dtype)

    o_ref[...] = x_ref[...] + gate_ref[...] * _dot(h_ref[...], w_ref[...])


def out_proj_residual(ys, group_g, w_out, x2, gate, seq, tm=1024, tn=1024):
    t, d = x2.shape
    tm = min(tm, seq)
    per_b = seq // tm
    bsz = gate.shape[0]
    y_spec = pl.BlockSpec((tm, GROUP_WIDTH), lambda i, j: (i, 0))
    return pl.pallas_call(
        _out_proj_kernel,
        grid=(t // tm, d // tn),
        in_specs=[y_spec, y_spec, y_spec, y_spec,
                  pl.BlockSpec((4, GROUP_WIDTH), lambda i, j: (0, 0)),
                  pl.BlockSpec((4 * GROUP_WIDTH, tn), lambda i, j: (0, j)),
                  pl.BlockSpec((tm, tn), lambda i, j: (i, j)),
                  pl.BlockSpec((None, 1, tn), lambda i, j: (i // per_b, 0, j))],
        out_specs=pl.BlockSpec((tm, tn), lambda i, j: (i, j)),
        out_shape=jax.ShapeDtypeStruct((t, d), F32),
        scratch_shapes=[pltpu.VMEM((tm, 4 * GROUP_WIDTH), BF16)],
        compiler_params=_cparams(("parallel", "arbitrary")),
        name="out_proj_residual",
    )(*[y.reshape(t, GROUP_WIDTH) for y in ys], group_g, w_out, x2, gate.reshape(bsz, 1, d))


def _swiglu_act(h, w1_ref, w3_ref):
    a = _dot(h, w1_ref[...])
    return (a * _sigmoid(a) * _dot(h, w3_ref[...])).astype(BF16)


def _ffn_kernel(x_ref, g_ref, sh_ref, sc_ref, gate_ref, w1_ref, w3_ref, w2_ref, o_ref, h_ref):
    f = pl.program_id(1)

    @pl.when(f == 0)
    def _():
        _norm_mod_rows(x_ref, g_ref, sh_ref, sc_ref, h_ref)
        o_ref[...] = jnp.zeros_like(o_ref)

    o_ref[...] += _dot(_swiglu_act(h_ref[...], w1_ref, w3_ref), w2_ref[...])

    @pl.when(f == pl.num_programs(1) - 1)
    def _():
        o_ref[...] = x_ref[...] + gate_ref[...] * o_ref[...]


def dense_ffn_residual(x2, g, shift, scale, gate, w1, w3, w2, seq, tm=1024, tf=512):
    t, d = x2.shape
    ffn = w1.shape[1]
    tm = min(tm, seq)
    per_b = seq // tm
    bsz = gate.shape[0]
    row = lambda: pl.BlockSpec((None, 1, d), lambda i, f: (i // per_b, 0, 0))
    return pl.pallas_call(
        _ffn_kernel,
        grid=(t // tm, ffn // tf),
        in_specs=[pl.BlockSpec((tm, d), lambda i, f: (i, 0), pipeline_mode=pl.Buffered(1)),
                  pl.BlockSpec((1, d), lambda i, f: (0, 0)),
                  row(), row(), row(),
                  pl.BlockSpec((d, tf), lambda i, f: (0, f)),
                  pl.BlockSpec((d, tf), lambda i, f: (0, f)),
                  pl.BlockSpec((tf, d), lambda i, f: (f, 0))],
        out_specs=pl.BlockSpec((tm, d), lambda i, f: (i, 0)),
        out_shape=jax.ShapeDtypeStruct((t, d), F32),
        scratch_shapes=[pltpu.VMEM((tm, d), BF16)],
        compiler_params=_cparams(("parallel", "arbitrary")),
        name="dense_ffn_residual",
    )(x2, g.reshape(1, d), shift.reshape(bsz, 1, d), scale.reshape(bsz, 1, d),
      gate.reshape(bsz, 1, d), w1, w3, w2)


def _router_kernel(x_ref, g_ref, sh_ref, sc_ref, rw_ref, h_ref, idx_ref, gates_ref):
    _norm_mod_rows(x_ref, g_ref, sh_ref, sc_ref, h_ref)
    logits = jnp.dot(h_ref[...], rw_ref[...], precision=lax.Precision.HIGHEST,
                     preferred_element_type=F32)
    lane = lax.broadcasted_iota(jnp.int32, logits.shape, 1)
    lane_f = lane.astype(F32)
    logits = jnp.where(lane < N_EXPERTS, logits, -jnp.inf)
    m0 = jnp.max(logits, axis=-1, keepdims=True)
    e0 = jnp.min(jnp.where(logits == m0, lane_f, float(LANES)), axis=-1, keepdims=True)
    rest = jnp.where(lane_f == e0, -jnp.inf, logits)
    m1 = jnp.max(rest, axis=-1, keepdims=True)
    e1 = jnp.min(jnp.where(rest == m1, lane_f, float(LANES)), axis=-1, keepdims=True)
    p1 = jnp.exp(m1 - m0)
    g0 = 1.0 / (1.0 + p1)
    idx_ref[...] = jnp.where(lane == 0, e0, jnp.where(lane == 1, e1, 0.0)).astype(jnp.int32)
    gates_ref[...] = jnp.where(lane == 0, g0, jnp.where(lane == 1, p1 * g0, 0.0))


def moe_router(x2, g, shift, scale, router_w, seq, tm=256):
    t, d = x2.shape
    tm = min(tm, seq)
    per_b = seq // tm
    bsz = shift.shape[0]
    rw = jnp.zeros((d, LANES), F32).at[:, :N_EXPERTS].set(router_w)
    row = lambda: pl.BlockSpec((None, 1, d), lambda i: (i // per_b, 0, 0))
    return pl.pallas_call(
        _router_kernel,
        grid=(t // tm,),
        in_specs=[pl.BlockSpec((tm, d), lambda i: (i, 0)),
                  pl.BlockSpec((1, d), lambda i: (0, 0)),
                  row(), row(),
                  pl.BlockSpec((d, LANES), lambda i: (0, 0))],
        out_specs=[pl.BlockSpec((tm, d), lambda i: (i, 0)),
                   pl.BlockSpec((tm, LANES), lambda i: (i, 0)),
                   pl.BlockSpec((tm, LANES), lambda i: (i, 0))],
        out_shape=[jax.ShapeDtypeStruct((t, d), F32),
                   jax.ShapeDtypeStruct((t, LANES), jnp.int32),
                   jax.ShapeDtypeStruct((t, LANES), F32)],
        compiler_params=_cparams(("parallel",)),
        name="moe_router",
    )(x2, g.reshape(1, d), shift.reshape(bsz, 1, d), scale.reshape(bsz, 1, d), rw)


GATHER_UNROLL = 8
MOE_STEPS = 4
MOE_TILE_ROWS = 528


def _row_copy(src_ref, dst_ref, sem, src_row, dst_row):
    return pltpu.make_async_copy(src_ref.at[pl.ds(src_row, 1), :],
                                 dst_ref.at[pl.ds(dst_row, 1), :], sem)


def _start_row_gather(src_ref, dst_ref, sem, row_of, n_rows):
    def start(r, _):
        _row_copy(src_ref, dst_ref, sem, row_of(r), r).start()
        return 0

    lax.fori_loop(0, n_rows, start, 0, unroll=GATHER_UNROLL)


def _wait_row_gather(dst_ref, sem):
    pltpu.make_async_copy(dst_ref, dst_ref, sem).wait()


def _moe_ffn_kernel(tile_e_ref, tile_ok_ref, buf_t_ref, h_hbm, w1_ref, w3_ref, w2_ref, o_ref,
                    hf_ref, hb_ref, sem):
    i = pl.program_id(0)
    f = pl.program_id(1)
    n_tiles = pl.num_programs(0)
    nf = MOE_STEPS
    tm = hb_ref.shape[0]
    rows_per_step = tm // nf
    ok = tile_ok_ref[i] > 0
    last = (i == n_tiles - 1) & (f == nf - 1)
    nxt = jnp.minimum(i + 1, n_tiles - 1)
    started = (i == 0) | (tile_ok_ref[jnp.maximum(i - 1, 0)] > 0)

    @pl.when(f == 0)
    def _():
        @pl.when(i == 0)
        def _():
            _start_row_gather(h_hbm, hf_ref, sem, lambda r: buf_t_ref[r], tm)

        @pl.when(started)
        def _():
            _wait_row_gather(hf_ref, sem)

        @pl.when(ok)
        def _():
            hb_ref[...] = hf_ref[...].astype(hb_ref.dtype)
            o_ref[...] = jnp.zeros_like(o_ref)

    @pl.when(ok)
    def _():
        base = f * rows_per_step
        for r in range(rows_per_step):
            _row_copy(h_hbm, hf_ref, sem, buf_t_ref[nxt * tm + base + r], base + r).start()
        o_ref[...] += _dot(_swiglu_act(hb_ref[...], w1_ref, w3_ref), w2_ref[...])

    @pl.when(ok & last)
    def _():
        _wait_row_gather(hf_ref, sem)

    @pl.when(jnp.logical_not(ok) & (f == nf - 1))
    def _():
        o_ref[...] = jnp.zeros_like(o_ref)


def moe_expert_ffn(h2, tile_e, tile_ok, buf_t, w1, w3, w2, tm):
    t, d = h2.shape
    ffn = w1.shape[2]
    nf = MOE_STEPS
    tf = ffn // nf
    n_tiles = tile_e.shape[0]

    def fsel(i, f, ok):
        return jnp.where(ok[i] > 0, f, nf - 1)

    grid_spec = pltpu.PrefetchScalarGridSpec(
        num_scalar_prefetch=3,
        grid=(n_tiles, nf),
        in_specs=[pl.BlockSpec(memory_space=pl.ANY),
                  pl.BlockSpec((None, d, tf), lambda i, f, te, ok, bt: (te[i], 0, fsel(i, f, ok))),
                  pl.BlockSpec((None, d, tf), lambda i, f, te, ok, bt: (te[i], 0, fsel(i, f, ok))),
                  pl.BlockSpec((None, tf, d), lambda i, f, te, ok, bt: (te[i], fsel(i, f, ok), 0))],
        out_specs=pl.BlockSpec((tm, d), lambda i, f, te, ok, bt: (i, 0)),
        scratch_shapes=[pltpu.VMEM((tm, d), F32), pltpu.VMEM((tm, d), BF16),
                        pltpu.SemaphoreType.DMA(())],
    )
    return pl.pallas_call(
        _moe_ffn_kernel,
        grid_spec=grid_spec,
        out_shape=jax.ShapeDtypeStruct((n_tiles * tm, d), F32),
        compiler_params=_cparams(("arbitrary", "arbitrary"), VMEM_LIMIT_MOE),
        name="moe_expert_ffn",
    )(tile_e, tile_ok, buf_t, h2, w1, w3, w2)


def _moe_combine_kernel(d0_ref, d1_ref, y_hbm, x_ref, gate_ref, gates_ref, o_ref,
                        y0_ref, y1_ref, sems0, sems1):
    i = pl.program_id(0)
    tm = x_ref.shape[0]
    slot = i % 2

    def start_gather(tile, into):
        def start(r, _):
            _row_copy(y_hbm, y0_ref.at[into], sems0.at[into], d0_ref[tile * tm + r], r).start()
            _row_copy(y_hbm, y1_ref.at[into], sems1.at[into], d1_ref[tile * tm + r], r).start()
            return 0

        lax.fori_loop(0, tm, start, 0, unroll=GATHER_UNROLL)

    @pl.when(i == 0)
    def _():
        start_gather(0, 0)

    @pl.when(i + 1 < pl.num_programs(0))
    def _():
        start_gather(i + 1, 1 - slot)

    _wait_row_gather(y0_ref.at[slot], sems0.at[slot])
    _wait_row_gather(y1_ref.at[slot], sems1.at[slot])
    gates = gates_ref[...]
    y = gates[:, 0:1] * y0_ref[slot] + gates[:, 1:2] * y1_ref[slot]
    o_ref[...] = x_ref[...] + gate_ref[...] * y


def moe_combine_residual(dest0, dest1, y_buf, x2, gate, gates, seq, tm=256):
    t, d = x2.shape
    tm = min(tm, seq)
    per_b = seq // tm
    bsz = gate.shape[0]
    grid_spec = pltpu.PrefetchScalarGridSpec(
        num_scalar_prefetch=2,
        grid=(t // tm,),
        in_specs=[pl.BlockSpec(memory_space=pl.ANY),
                  pl.BlockSpec((tm, d), lambda i, d0, d1: (i, 0)),
                  pl.BlockSpec((None, 1, d), lambda i, d0, d1: (i // per_b, 0, 0)),
                  pl.BlockSpec((tm, LANES), lambda i, d0, d1: (i, 0))],
        out_specs=pl.BlockSpec((tm, d), lambda i, d0, d1: (i, 0)),
        scratch_shapes=[pltpu.VMEM((2, tm, d), F32), pltpu.VMEM((2, tm, d), F32),
                        pltpu.SemaphoreType.DMA((2,)), pltpu.SemaphoreType.DMA((2,))],
    )
    return pl.pallas_call(
        _moe_combine_kernel,
        grid_spec=grid_spec,
        out_shape=jax.ShapeDtypeStruct((t, d), F32),
        compiler_params=_cparams(("arbitrary",)),
        name="moe_combine_residual",
    )(dest0, dest1, y_buf, x2, gate.reshape(bsz, 1, d), gates)


def _moe_routing_tables(idx, tm):
    t = idx.shape[0]
    n_slots = t * MOE_TOPK
    n_tiles = n_slots // tm + N_EXPERTS
    flat_e = idx[:, :MOE_TOPK].reshape(-1)
    onehot = (flat_e[:, None] == jnp.arange(N_EXPERTS)[None, :]).astype(jnp.int32)
    counts = jnp.sum(onehot, axis=0)
    rank = jnp.sum((jnp.cumsum(onehot, axis=0) - onehot) * onehot, axis=1)
    padded = (counts + tm - 1) // tm * tm
    pad_ends = jnp.cumsum(padded)
    pad_starts = pad_ends - padded
    dest = (pad_starts[flat_e] + rank).astype(jnp.int32)
    buf_t = jnp.zeros((n_tiles * tm,), jnp.int32).at[dest].set(
        (jnp.arange(n_slots) // MOE_TOPK).astype(jnp.int32))
    tile_start = jnp.arange(n_tiles) * tm
    tile_e = jnp.minimum(jnp.searchsorted(pad_ends, tile_start, side="right"),
                         N_EXPERTS - 1).astype(jnp.int32)
    tile_ok = (tile_start < pad_ends[-1]).astype(jnp.int32)
    dest2 = dest.reshape(t, MOE_TOPK)
    return tile_e, tile_ok, buf_t, dest2[:, 0], dest2[:, 1]


def moe_ffn_residual(x2, g, shift, scale, gate, router_w, w1, w3, w2, seq, tile_rows=MOE_TILE_ROWS):
    h2, idx, gates = moe_router(x2, g, shift, scale, router_w, seq)
    tile_e, tile_ok, buf_t, dest0, dest1 = _moe_routing_tables(idx, tile_rows)
    y_buf = moe_expert_ffn(h2, tile_e, tile_ok, buf_t, w1, w3, w2, tile_rows)
    return moe_combine_residual(dest0, dest1, y_buf, x2, gate, gates, seq)


def _pad_w_in(w_in):
    d = w_in.shape[0]
    kpe_end = MLA_Q_RANK + MLA_KV_RANK + MLA_ROPE_DIM
    tail = PROJ_WIDTH - (COL_MOBA + 3 * GROUP_WIDTH)
    return jnp.concatenate([
        w_in[:, :kpe_end], jnp.zeros((d, LANES - MLA_ROPE_DIM), w_in.dtype),
        w_in[:, kpe_end:], jnp.zeros((d, tail), w_in.dtype)], axis=1).astype(BF16)


def _pad_heads_192(a):
    lead = a.shape[:-1]
    a = a.reshape(lead + (GROUP_HEADS, MLA_QK_DIM))
    a = jnp.pad(a, [(0, 0)] * len(lead) + [(0, 0), (0, MLA_QK_PAD - MLA_QK_DIM)])
    return a.reshape(lead + (GROUP_HEADS * MLA_QK_PAD,))


def kernel(x, c, positions, ada_w, ada_b, norm_mix_g, norm_ffn_g, w_in, mla_q_norm_g, mla_kv_norm_g, mla_w_uq, mla_w_ukv, mla_q_head_g, mla_k_head_g, moba_q_head_g, moba_k_head_g, group_norm_g, w_out, ffn_w1, ffn_w3, ffn_w2, router_w, moe_w1, moe_w3, moe_w2):
    bsz, seq, d = x.shape
    depth = ada_w.shape[0]
    cos_pe, sin_pe = _rope_tables(positions, MLA_ROPE_DIM)
    cos_full, sin_full = _rope_tables(positions, HEAD_DIM)
    mod = ada_modulation(c, ada_w, ada_b)
    x2 = x.reshape(bsz * seq, d)
    for l in range(depth):
        shift_m, scale_m, gate_m, shift_f, scale_f, gate_f = jnp.split(mod[l], 6, axis=-1)
        proj = norm_mod_proj(x2, norm_mix_g[l], shift_m, scale_m, _pad_w_in(w_in[l]), seq)
        proj3 = proj.reshape(bsz, seq, PROJ_WIDTH)
        pad_g = lambda g: jnp.pad(g, (0, MLA_QK_PAD - MLA_QK_DIM)).reshape(1, MLA_QK_PAD)
        q_mla, k_mla, v_mla = mla_prep(
            proj3, mla_q_norm_g[l], mla_kv_norm_g[l],
            _pad_heads_192(mla_w_uq[l]).astype(BF16), mla_w_ukv[l].astype(BF16),
            pad_g(mla_q_head_g[l]), pad_g(mla_k_head_g[l]), cos_pe, sin_pe)
        y_mla = mla_attention(q_mla, k_mla, v_mla)
        y_ret = retention(proj3, cos_full, sin_full)
        y_sb = sb_attention(proj3)
        q_mb, k_mb, kmean = moba_prep(proj3, moba_q_head_g[l], moba_k_head_g[l], cos_full, sin_full)
        y_moba = moba_attention(q_mb, k_mb, kmean, proj3)
        x2 = out_proj_residual((y_mla, y_ret, y_sb, y_moba), group_norm_g[l],
                               w_out[l].astype(BF16), x2, gate_m, seq)
        j = l // 2
        if l % 2 == 0:
            x2 = dense_ffn_residual(x2, norm_ffn_g[l], shift_f, scale_f, gate_f,
                                    ffn_w1[j].astype(BF16), ffn_w3[j].astype(BF16),
                                    ffn_w2[j].astype(BF16), seq)
        else:
            x2 = moe_ffn_residual(x2, norm_ffn_g[l], shift_f, scale_f, gate_f, router_w[j],
                                  moe_w1[j].astype(BF16), moe_w3[j].astype(BF16),
                                  moe_w2[j].astype(BF16), seq)
    return x2.reshape(bsz, seq, d)
```

```python
import functools

import jax
import jax.numpy as jnp
from jax import lax
from jax.experimental import pallas as pl
from jax.experimental.pallas import tpu as pltpu

F32 = jnp.float32
BF16 = jnp.bfloat16

HEAD_DIM = 128
GROUP_HEADS = 4
GROUP_WIDTH = 512
MLA_Q_RANK = 512
MLA_KV_RANK = 256
MLA_NOPE_DIM = 128
MLA_ROPE_DIM = 64
MLA_QK_DIM = MLA_NOPE_DIM + MLA_ROPE_DIM
MLA_QK_PAD = 256
RET_CHUNK = 128
MOBA_BLOCK = 256
MOBA_TOPK = 3
ROPE_THETA = 10000.0
NORM_EPS = 1e-6
NEG = -1e30
N_EXPERTS = 8
MOE_TOPK = 2

LANES = 128
ATTN_BLOCK = 256
VMEM_LIMIT = 56 * 1024 * 1024
VMEM_LIMIT_MOE = 60 * 1024 * 1024

COL_CQ = 0
COL_CKV = 512
COL_KPE = 768
COL_RET = 896
COL_SB = COL_RET + 4 * GROUP_WIDTH
COL_MOBA = COL_SB + 3 * GROUP_WIDTH
PROJ_WIDTH = 6144


def _cparams(sem, vmem=VMEM_LIMIT):
    return pltpu.CompilerParams(dimension_semantics=sem, vmem_limit_bytes=vmem)


def _dot(a, b):
    return jnp.dot(a, b, preferred_element_type=F32)


def _dot_nt(a, b):
    return lax.dot_general(a, b, (((1,), (1,)), ((), ())), preferred_element_type=F32)


def _sigmoid(x):
    return 1.0 / (1.0 + jnp.exp(-x))


def _softplus(z):
    return jnp.maximum(z, 0.0) + jnp.log(1.0 + jnp.exp(-jnp.abs(z)))


def _rms(xf, width=None):
    width = xf.shape[-1] if width is None else width
    ss = jnp.sum(xf * xf, axis=-1, keepdims=True) * (1.0 / width)
    return xf * lax.rsqrt(ss + NORM_EPS)


def _ada_kernel(c_ref, w_ref, b_ref, o_ref):
    c = c_ref[...]
    cond = c * _sigmoid(c)
    o_ref[...] = _dot(cond.astype(BF16), w_ref[...].astype(BF16)) + b_ref[...]


def ada_modulation(c, ada_w, ada_b, tn=1024):
    depth, d, n = ada_w.shape
    b = c.shape[0]
    return pl.pallas_call(
        _ada_kernel,
        grid=(depth, n // tn),
        in_specs=[
            pl.BlockSpec((b, d), lambda l, j: (0, 0)),
            pl.BlockSpec((None, d, tn), lambda l, j: (l, 0, j)),
            pl.BlockSpec((None, 1, tn), lambda l, j: (l, 0, j)),
        ],
        out_specs=pl.BlockSpec((None, b, tn), lambda l, j: (l, 0, j)),
        out_shape=jax.ShapeDtypeStruct((depth, b, n), F32),
        compiler_params=_cparams(("parallel", "parallel")),
        name="ada_modulation",
    )(c, ada_w, ada_b.reshape(depth, 1, n))


def _norm_mod_rows(x_ref, g_ref, sh_ref, sc_ref, dst_ref, rows=32):
    tm = x_ref.shape[0]
    mul = g_ref[...] * (1.0 + sc_ref[...])
    sh = sh_ref[...]

    def body(r, _):
        sl = pl.ds(pl.multiple_of(r * rows, rows), rows)
        dst_ref[sl, :] = (_rms(x_ref[sl, :]) * mul + sh).astype(dst_ref.dtype)
        return 0

    lax.fori_loop(0, tm // rows, body, 0, unroll=2)


def _proj_kernel(x_ref, g_ref, sh_ref, sc_ref, w_ref, o_ref, h_ref):
    @pl.when(pl.program_id(1) == 0)
    def _():
        _norm_mod_rows(x_ref, g_ref, sh_ref, sc_ref, h_ref)

    o_ref[...] = _dot(h_ref[...], w_ref[...]).astype(o_ref.dtype)


def norm_mod_proj(x2, g, shift, scale, w, seq, tm=1024, tn=2048):
    t, d = x2.shape
    n = w.shape[1]
    tm = min(tm, seq)
    per_b = seq // tm
    bsz = shift.shape[0]
    return pl.pallas_call(
        _proj_kernel,
        grid=(t // tm, n // tn),
        in_specs=[
            pl.BlockSpec((tm, d), lambda i, j: (i, 0)),
            pl.BlockSpec((1, d), lambda i, j: (0, 0)),
            pl.BlockSpec((None, 1, d), lambda i, j: (i // per_b, 0, 0)),
            pl.BlockSpec((None, 1, d), lambda i, j: (i // per_b, 0, 0)),
            pl.BlockSpec((d, tn), lambda i, j: (0, j)),
        ],
        out_specs=pl.BlockSpec((tm, tn), lambda i, j: (i, j)),
        out_shape=jax.ShapeDtypeStruct((t, n), BF16),
        scratch_shapes=[pltpu.VMEM((tm, d), BF16)],
        compiler_params=_cparams(("parallel", "arbitrary")),
        name="norm_mod_proj",
    )(x2, g.reshape(1, d), shift.reshape(bsz, 1, d), scale.reshape(bsz, 1, d), w)


def _rope_tables(positions, dim):
    inv_freq = ROPE_THETA ** (-jnp.arange(0, dim, 2, dtype=F32) / dim)
    ang = positions.astype(F32)[:, None] * inv_freq[None, :]
    cos, sin = jnp.cos(ang), jnp.sin(ang)
    pad = jnp.zeros((positions.shape[0], LANES - dim), F32)
    return (jnp.concatenate([cos, cos, pad], axis=-1),
            jnp.concatenate([-sin, sin, pad], axis=-1))


def _rope_full(z, cos2, sin_s):
    return z * cos2 + pltpu.roll(z, 64, 1) * sin_s


def _rope_64(z, cos2, sin_s):
    lane = lax.broadcasted_iota(jnp.int32, z.shape, 1)
    partner = jnp.where(lane < 32, pltpu.roll(z, 96, 1), pltpu.roll(z, 32, 1))
    return z * cos2 + partner * sin_s


def _mla_prep_kernel(p_ref, qg_ref, kvg_ref, wuq_ref, wukv_ref, qhg_ref, khg_ref,
                     cos_ref, sin_ref, q_ref, k_ref, v_ref):
    p = p_ref[...].astype(F32)
    c_q = p[:, COL_CQ:COL_CQ + MLA_Q_RANK]
    c_kv = p[:, COL_CKV:COL_CKV + MLA_KV_RANK]
    k_pe = p[:, COL_KPE:COL_KPE + LANES]
    q = _dot((_rms(c_q) * qg_ref[...]).astype(BF16), wuq_ref[...])
    kv = _dot((_rms(c_kv) * kvg_ref[...]).astype(BF16), wukv_ref[...])
    cos2, sin_s = cos_ref[...], sin_ref[...]
    qhg, khg = qhg_ref[...], khg_ref[...]
    scale = MLA_QK_DIM ** -0.5
    pe_ss = jnp.sum(k_pe * k_pe, axis=-1, keepdims=True)
    for h in range(GROUP_HEADS):
        qh = q[:, h * MLA_QK_PAD:(h + 1) * MLA_QK_PAD]
        qh = _rms(qh, MLA_QK_DIM) * qhg * scale
        q_ref[h, :, :LANES] = qh[:, :LANES].astype(q_ref.dtype)
        q_ref[h, :, LANES:] = _rope_64(qh[:, LANES:], cos2, sin_s).astype(q_ref.dtype)
        k_nope = kv[:, 2 * h * LANES:(2 * h + 1) * LANES]
        ss = (jnp.sum(k_nope * k_nope, axis=-1, keepdims=True) + pe_ss) * (1.0 / MLA_QK_DIM)
        r = lax.rsqrt(ss + NORM_EPS)
        k_ref[h, :, :LANES] = (k_nope * r * khg[:, :LANES]).astype(k_ref.dtype)
        k_ref[h, :, LANES:] = _rope_64(k_pe * r * khg[:, LANES:], cos2, sin_s).astype(k_ref.dtype)
        v_ref[h] = kv[:, (2 * h + 1) * LANES:(2 * h + 2) * LANES].astype(v_ref.dtype)


def mla_prep(proj3, q_norm_g, kv_norm_g, w_uq_pad, w_ukv, q_head_g_pad, k_head_g_pad,
             cos_pe, sin_pe, tm=512):
    bsz, seq, _ = proj3.shape
    h = GROUP_HEADS
    n_in = COL_KPE + LANES
    const = lambda shape: pl.BlockSpec(shape, lambda b, i: (0,) * len(shape))
    return pl.pallas_call(
        _mla_prep_kernel,
        grid=(bsz, seq // tm),
        in_specs=[
            pl.BlockSpec((None, tm, n_in), lambda b, i: (b, i, 0)),
            const((1, MLA_Q_RANK)), const((1, MLA_KV_RANK)),
            const((MLA_Q_RANK, h * MLA_QK_PAD)), const((MLA_KV_RANK, h * 2 * LANES)),
            const((1, MLA_QK_PAD)), const((1, MLA_QK_PAD)),
            pl.BlockSpec((tm, LANES), lambda b, i: (i, 0)),
            pl.BlockSpec((tm, LANES), lambda b, i: (i, 0)),
        ],
        out_specs=[
            pl.BlockSpec((None, h, tm, MLA_QK_PAD), lambda b, i: (b, 0, i, 0)),
            pl.BlockSpec((None, h, tm, MLA_QK_PAD), lambda b, i: (b, 0, i, 0)),
            pl.BlockSpec((None, h, tm, LANES), lambda b, i: (b, 0, i, 0)),
        ],
        out_shape=[
            jax.ShapeDtypeStruct((bsz, h, seq, MLA_QK_PAD), BF16),
            jax.ShapeDtypeStruct((bsz, h, seq, MLA_QK_PAD), BF16),
            jax.ShapeDtypeStruct((bsz, h, seq, LANES), BF16),
        ],
        compiler_params=_cparams(("parallel", "parallel")),
        name="mla_prep",
    )(proj3, q_norm_g.reshape(1, -1), kv_norm_g.reshape(1, -1), w_uq_pad, w_ukv,
      q_head_g_pad, k_head_g_pad, cos_pe, sin_pe)


def _moba_gates(q_ref, kmean_ref):
    km = kmean_ref[...]
    nb = km.shape[0]
    hi = km.astype(BF16).astype(F32)
    mid = (km - hi).astype(BF16).astype(F32)
    lo = (km - hi - mid).astype(BF16).astype(F32)
    pieces = jnp.concatenate([hi, mid, lo, jnp.zeros_like(km)], axis=0).astype(BF16)
    r = _dot_nt(pieces, q_ref[...])
    return r[:nb] + r[nb:2 * nb] + r[2 * nb:3 * nb]


def _moba_block_choice(gate, n_past):
    blk = lax.broadcasted_iota(jnp.int32, gate.shape, 0)
    rank = jnp.zeros(gate.shape, jnp.int32)
    for o in range(n_past):
        g_o = gate[o:o + 1, :]
        tie = jnp.where(o < blk, 1, 0)
        rank = rank + jnp.where(g_o > gate, 1, jnp.where(g_o == gate, tie, 0))
    return jnp.where(rank < MOBA_TOPK, 1.0, 0.0)


def _attn_kernel(*refs, moba):
    if moba:
        q_ref, k_ref, v_ref, kmean_ref, o_ref = refs
        gates = _moba_gates(q_ref, kmean_ref)
    else:
        q_ref, k_ref, v_ref, o_ref = refs
    seq = q_ref.shape[0]
    t = ATTN_BLOCK
    v_t = jnp.transpose(v_ref[...].astype(F32)).astype(BF16)
    key = lax.broadcasted_iota(jnp.int32, (t, t), 0)
    qry = lax.broadcasted_iota(jnp.int32, (t, t), 1)
    causal = key <= qry
    for i in range(seq // t):
        own = slice(i * t, (i + 1) * t)
        q = q_ref[own, :]
        s_own = jnp.where(causal, _dot_nt(k_ref[own, :], q), NEG)
        m = jnp.max(s_own, axis=0, keepdims=True)
        s_past = []
        if i > 0:
            s_all = _dot_nt(k_ref[:i * t, :], q)
            if moba and i > MOBA_TOPK:
                sel = _moba_block_choice(gates[:, own], i)
                s_past = [jnp.where(sel[n:n + 1, :] > 0.5, s_all[n * t:(n + 1) * t, :], NEG)
                          for n in range(i)]
            else:
                s_past = [s_all]
            for s in s_past:
                m = jnp.maximum(m, jnp.max(s, axis=0, keepdims=True))
        p = jnp.exp(s_own - m)
        l = jnp.sum(p, axis=0, keepdims=True)
        acc_t = _dot(v_t[:, own], p.astype(BF16))
        start = 0
        for s in s_past:
            p = jnp.exp(s - m)
            l = l + jnp.sum(p, axis=0, keepdims=True)
            acc_t = acc_t + _dot(v_t[:, start:start + s.shape[0]], p.astype(BF16))
            start += s.shape[0]
        o_ref[own, :] = jnp.transpose(acc_t / l).astype(o_ref.dtype)


def _head_spec(seq, d):
    return pl.BlockSpec((None, None, seq, d), lambda b, hh: (b, hh, 0, 0))


def _proj_col_spec(seq, col):
    cb = col // LANES
    return pl.BlockSpec((None, seq, LANES), lambda b, hh: (b, 0, cb + hh))


def _attention(q, k, v, v_spec, kmean=None):
    bsz, h, seq, dk = q.shape
    in_specs = [_head_spec(seq, dk), _head_spec(seq, dk), v_spec]
    args = [q, k, v]
    if kmean is not None:
        in_specs.append(_head_spec(kmean.shape[2], LANES))
        args.append(kmean)
    return pl.pallas_call(
        functools.partial(_attn_kernel, moba=kmean is not None),
        grid=(bsz, h),
        in_specs=in_specs,
        out_specs=pl.BlockSpec((None, seq, LANES), lambda b, hh: (b, 0, hh)),
        out_shape=jax.ShapeDtypeStruct((bsz, seq, h * LANES), BF16),
        compiler_params=_cparams(("parallel", "parallel")),
        name="moba_attention" if kmean is not None else "mla_attention",
    )(*args)


def mla_attention(q, k, v):
    return _attention(q, k, v, _head_spec(q.shape[2], LANES))


def _moba_prep_kernel(q_ref, k_ref, qg_ref, kg_ref, cos_ref, sin_ref, qo_ref, ko_ref, km_ref):
    seq = q_ref.shape[0]
    scale = HEAD_DIM ** -0.5
    for blk in range(seq // MOBA_BLOCK):
        sl = slice(blk * MOBA_BLOCK, (blk + 1) * MOBA_BLOCK)
        cos2, sin_s = cos_ref[sl, :], sin_ref[sl, :]
        qn = _rms(q_ref[sl, :].astype(F32)) * qg_ref[...]
        qo_ref[sl, :] = (_rope_full(qn, cos2, sin_s) * scale).astype(qo_ref.dtype)
        kn = _rope_full(_rms(k_ref[sl, :].astype(F32)) * kg_ref[...], cos2, sin_s)
        ko_ref[sl, :] = kn.astype(ko_ref.dtype)
        km_ref[blk:blk + 1, :] = jnp.mean(kn, axis=0, keepdims=True)


def moba_prep(proj3, q_head_g, k_head_g, cos_full, sin_full):
    bsz, seq, _ = proj3.shape
    h = GROUP_HEADS
    nb = seq // MOBA_BLOCK
    const = lambda shape: pl.BlockSpec(shape, lambda b, hh: (0,) * len(shape))
    return pl.pallas_call(
        _moba_prep_kernel,
        grid=(bsz, h),
        in_specs=[
            _proj_col_spec(seq, COL_MOBA),
            _proj_col_spec(seq, COL_MOBA + GROUP_WIDTH),
            const((1, LANES)), const((1, LANES)),
            const((seq, LANES)), const((seq, LANES)),
        ],
        out_specs=[_head_spec(seq, LANES), _head_spec(seq, LANES), _head_spec(nb, LANES)],
        out_shape=[
            jax.ShapeDtypeStruct((bsz, h, seq, LANES), BF16),
            jax.ShapeDtypeStruct((bsz, h, seq, LANES), BF16),
            jax.ShapeDtypeStruct((bsz, h, nb, LANES), F32),
        ],
        compiler_params=_cparams(("parallel", "parallel")),
        name="moba_prep",
    )(proj3, proj3, q_head_g.reshape(1, -1), k_head_g.reshape(1, -1), cos_full, sin_full)


def moba_attention(q, k, kmean, proj3):
    v_spec = _proj_col_spec(q.shape[2], COL_MOBA + 2 * GROUP_WIDTH)
    return _attention(q, k, proj3, v_spec, kmean=kmean)


SB_DEAD_TAIL = -104.0


def _sb_tile(q, k_blk, v_blk, tail, later, strict):
    z = _dot_nt(q, k_blk) * (HEAD_DIM ** -0.5)
    log_1m = -_softplus(z)
    if strict is not None:
        log_1m = jnp.where(strict, log_1m, 0.0)
    hi = log_1m.astype(BF16)
    lo = (log_1m - hi.astype(F32)).astype(BF16)
    suffix = _dot(hi, later) + _dot(lo, later)
    a = jnp.exp(z + log_1m + suffix + tail)
    if strict is not None:
        a = jnp.where(strict, a, 0.0)
    return _dot(a.astype(v_blk.dtype), v_blk), jnp.sum(log_1m, axis=-1, keepdims=True)


def _sb_kernel(q_ref, k_ref, v_ref, o_ref, acc_ref, tail_ref):
    seq = q_ref.shape[0]
    t = ATTN_BLOCK
    row = lax.broadcasted_iota(jnp.int32, (t, t), 0)
    col = lax.broadcasted_iota(jnp.int32, (t, t), 1)
    strict = col < row
    later = jnp.where(row > col, 1.0, 0.0).astype(BF16)
    blk = lambda n: slice(n * t, (n + 1) * t)

    for i in range(seq // t):
        q = q_ref[blk(i), :]
        acc, tail = _sb_tile(q, k_ref[blk(i), :], v_ref[blk(i), :], 0.0, later, strict)
        if i > 0:
            y, s = _sb_tile(q, k_ref[blk(i - 1), :], v_ref[blk(i - 1), :], tail, later, None)
            acc, tail = acc + y, tail + s
        acc_ref[blk(i), :] = acc
        tail_ref[blk(i), :] = tail

    for i in range(2, seq // t):
        alive = jnp.max(tail_ref[blk(i), :], axis=0, keepdims=True)[0, 0] > SB_DEAD_TAIL

        @pl.when(alive)
        def _():
            q = q_ref[blk(i), :]
            acc, tail = acc_ref[blk(i), :], tail_ref[blk(i), :]
            for n in range(i - 2, -1, -1):
                y, s = _sb_tile(q, k_ref[blk(n), :], v_ref[blk(n), :], tail, later, None)
                acc, tail = acc + y, tail + s
            acc_ref[blk(i), :] = acc

    o_ref[...] = acc_ref[...].astype(o_ref.dtype)


def sb_attention(proj3):
    bsz, seq, _ = proj3.shape
    return pl.pallas_call(
        _sb_kernel,
        grid=(bsz, GROUP_HEADS),
        in_specs=[
            _proj_col_spec(seq, COL_SB),
            _proj_col_spec(seq, COL_SB + GROUP_WIDTH),
            _proj_col_spec(seq, COL_SB + 2 * GROUP_WIDTH),
        ],
        out_specs=pl.BlockSpec((None, seq, LANES), lambda b, hh: (b, 0, hh)),
        out_shape=jax.ShapeDtypeStruct((bsz, seq, GROUP_WIDTH), BF16),
        scratch_shapes=[pltpu.VMEM((seq, LANES), F32), pltpu.VMEM((seq, 1), F32)],
        compiler_params=_cparams(("parallel", "parallel")),
        name="sb_attention",
    )(proj3, proj3, proj3)


def _ret_kernel(q_ref, k_ref, v_ref, g_ref, cos_ref, sin_ref, lg_ref, o_ref):
    seq = q_ref.shape[0]
    c = RET_CHUNK
    log_gamma = lg_ref[...]
    ri = lax.broadcasted_iota(jnp.int32, (c, c), 0).astype(F32)
    ci = lax.broadcasted_iota(jnp.int32, (c, c), 1).astype(F32)
    rel = ri - ci
    intra_decay = jnp.where(rel >= 0, jnp.exp(jnp.maximum(rel, 0.0) * log_gamma), 0.0)
    idx = lax.broadcasted_iota(jnp.int32, (c, 1), 0).astype(F32)
    query_decay = jnp.exp((idx + 1.0) * log_gamma)
    key_decay = jnp.exp((c - 1.0 - idx) * log_gamma)
    chunk_decay = jnp.exp(c * log_gamma)
    k_scale = HEAD_DIM ** -0.5

    state = jnp.zeros((HEAD_DIM, HEAD_DIM), F32)
    for n in range(seq // c):
        sl = slice(n * c, (n + 1) * c)
        cos2, sin_s = cos_ref[sl, :], sin_ref[sl, :]
        q = _rope_full(q_ref[sl, :].astype(F32), cos2, sin_s)
        k = _rope_full(k_ref[sl, :].astype(F32), cos2, sin_s) * k_scale
        vb = v_ref[sl, :]
        qb = q.astype(BF16)
        scores = _dot_nt(qb, k.astype(BF16)) * intra_decay
        y = _dot(scores.astype(BF16), vb)
        y = y + _dot(qb, state.astype(BF16)) * query_decay
        kd_t = jnp.transpose(k * key_decay).astype(BF16)
        state = state * chunk_decay + _dot(kd_t, vb)
        mu = jnp.mean(y, axis=-1, keepdims=True)
        yc = y - mu
        var = jnp.mean(yc * yc, axis=-1, keepdims=True)
        yn = yc * lax.rsqrt(var + NORM_EPS)
        g = g_ref[sl, :].astype(F32)
        o_ref[sl, :] = (g * _sigmoid(g) * yn).astype(o_ref.dtype)


def retention(proj3, cos_full, sin_full):
    bsz, seq, _ = proj3.shape
    h = GROUP_HEADS
    log_gamma = jnp.log(1.0 - 2.0 ** (-5.0 - jnp.arange(h, dtype=F32))).reshape(h, 1, 1)
    col = lambda k: pl.BlockSpec((None, seq, LANES),
                                 lambda b, hh: (b, 0, (COL_RET + k * GROUP_WIDTH) // LANES + hh))
    const = lambda shape: pl.BlockSpec(shape, lambda b, hh: (0,) * len(shape))
    return pl.pallas_call(
        _ret_kernel,
        grid=(bsz, h),
        in_specs=[col(0), col(1), col(2), col(3), const((seq, LANES)), const((seq, LANES)),
                  pl.BlockSpec((None, 1, 1), lambda b, hh: (hh, 0, 0))],
        out_specs=pl.BlockSpec((None, seq, LANES), lambda b, hh: (b, 0, hh)),
        out_shape=jax.ShapeDtypeStruct((bsz, seq, GROUP_WIDTH), BF16),
        compiler_params=_cparams(("parallel", "parallel")),
        name="retention",
    )(proj3, proj3, proj3, proj3, cos_full, sin_full, log_gamma)


def _out_proj_kernel(y0_ref, y1_ref, y2_ref, y3_ref, gg_ref, w_ref, x_ref, gate_ref, o_ref, h_ref):
    @pl.when(pl.program_id(1) == 0)
    def _():
        for grp, y_ref in enumerate((y0_ref, y1_ref, y2_ref, y3_ref)):
            yn = _rms(y_ref[...].astype(F32)) * gg_ref[grp:grp + 1, :]
            h_ref[:, grp * GROUP_WIDTH:(grp + 1) * GROUP_WIDTH] = yn.astype(h_ref.dtype)

    o_ref[...] = x_ref[...] + gate_ref[...] * _dot(h_ref[...], w_ref[...])


def out_proj_residual(ys, group_g, w_out, x2, gate, seq, tm=1024, tn=1024):
    t, d = x2.shape
    tm = min(tm, seq)
    per_b = seq // tm
    bsz = gate.shape[0]
    y_spec = pl.BlockSpec((tm, GROUP_WIDTH), lambda i, j: (i, 0))
    return pl.pallas_call(
        _out_proj_kernel,
        grid=(t // tm, d // tn),
        in_specs=[y_spec, y_spec, y_spec, y_spec,
                  pl.BlockSpec((4, GROUP_WIDTH), lambda i, j: (0, 0)),
                  pl.BlockSpec((4 * GROUP_WIDTH, tn), lambda i, j: (0, j)),
                  pl.BlockSpec((tm, tn), lambda i, j: (i, j)),
                  pl.BlockSpec((None, 1, tn), lambda i, j: (i // per_b, 0, j))],
        out_specs=pl.BlockSpec((tm, tn), lambda i, j: (i, j)),
        out_shape=jax.ShapeDtypeStruct((t, d), F32),
        scratch_shapes=[pltpu.VMEM((tm, 4 * GROUP_WIDTH), BF16)],
        compiler_params=_cparams(("parallel", "arbitrary")),
        name="out_proj_residual",
    )(*[y.reshape(t, GROUP_WIDTH) for y in ys], group_g, w_out, x2, gate.reshape(bsz, 1, d))


def _swiglu_act(h, w1_ref, w3_ref):
    a = _dot(h, w1_ref[...])
    return (a * _sigmoid(a) * _dot(h, w3_ref[...])).astype(BF16)


def _ffn_kernel(x_ref, g_ref, sh_ref, sc_ref, gate_ref, w1_ref, w3_ref, w2_ref, o_ref, h_ref):
    f = pl.program_id(1)

    @pl.when(f == 0)
    def _():
        _norm_mod_rows(x_ref, g_ref, sh_ref, sc_ref, h_ref)
        o_ref[...] = jnp.zeros_like(o_ref)

    o_ref[...] += _dot(_swiglu_act(h_ref[...], w1_ref, w3_ref), w2_ref[...])

    @pl.when(f == pl.num_programs(1) - 1)
    def _():
        o_ref[...] = x_ref[...] + gate_ref[...] * o_ref[...]


def _chunk_cols(w, n_chunks):
    *lead, d, ffn = w.shape
    w = w.astype(BF16).reshape(*lead, d, n_chunks, ffn // n_chunks)
    return jnp.swapaxes(w, -3, -2)


def dense_ffn_residual(x2, g, shift, scale, gate, w1, w3, w2, seq, tm=1024, tf=512):
    t, d = x2.shape
    ffn = w1.shape[1]
    w1, w3, w2 = _chunk_cols(w1, ffn // tf), _chunk_cols(w3, ffn // tf), w2.astype(BF16)
    tm = min(tm, seq)
    per_b = seq // tm
    bsz = gate.shape[0]
    row = lambda: pl.BlockSpec((None, 1, d), lambda i, f: (i // per_b, 0, 0))
    return pl.pallas_call(
        _ffn_kernel,
        grid=(t // tm, ffn // tf),
        in_specs=[pl.BlockSpec((tm, d), lambda i, f: (i, 0), pipeline_mode=pl.Buffered(1)),
                  pl.BlockSpec((1, d), lambda i, f: (0, 0)),
                  row(), row(), row(),
                  pl.BlockSpec((None, d, tf), lambda i, f: (f, 0, 0)),
                  pl.BlockSpec((None, d, tf), lambda i, f: (f, 0, 0)),
                  pl.BlockSpec((tf, d), lambda i, f: (f, 0))],
        out_specs=pl.BlockSpec((tm, d), lambda i, f: (i, 0)),
        out_shape=jax.ShapeDtypeStruct((t, d), F32),
        scratch_shapes=[pltpu.VMEM((tm, d), BF16)],
        compiler_params=_cparams(("parallel", "arbitrary")),
        name="dense_ffn_residual",
    )(x2, g.reshape(1, d), shift.reshape(bsz, 1, d), scale.reshape(bsz, 1, d),
      gate.reshape(bsz, 1, d), w1, w3, w2)


def _router_kernel(x_ref, g_ref, sh_ref, sc_ref, rw_ref, h_ref, idx_ref, gates_ref):
    _norm_mod_rows(x_ref, g_ref, sh_ref, sc_ref, h_ref)
    logits = jnp.dot(h_ref[...], rw_ref[...], precision=lax.Precision.HIGHEST,
                     preferred_element_type=F32)
    lane = lax.broadcasted_iota(jnp.int32, logits.shape, 1)
    lane_f = lane.astype(F32)
    logits = jnp.where(lane < N_EXPERTS, logits, -jnp.inf)
    m0 = jnp.max(logits, axis=-1, keepdims=True)
    e0 = jnp.min(jnp.where(logits == m0, lane_f, float(LANES)), axis=-1, keepdims=True)
    rest = jnp.where(lane_f == e0, -jnp.inf, logits)
    m1 = jnp.max(rest, axis=-1, keepdims=True)
    e1 = jnp.min(jnp.where(rest == m1, lane_f, float(LANES)), axis=-1, keepdims=True)
    p1 = jnp.exp(m1 - m0)
    g0 = 1.0 / (1.0 + p1)
    idx_ref[...] = jnp.where(lane == 0, e0, jnp.where(lane == 1, e1, 0.0)).astype(jnp.int32)
    gates_ref[...] = jnp.where(lane == 0, g0, jnp.where(lane == 1, p1 * g0, 0.0))


def moe_router(x2, g, shift, scale, router_w, seq, tm=256):
    t, d = x2.shape
    tm = min(tm, seq)
    per_b = seq // tm
    bsz = shift.shape[0]
    rw = jnp.zeros((d, LANES), F32).at[:, :N_EXPERTS].set(router_w)
    row = lambda: pl.BlockSpec((None, 1, d), lambda i: (i // per_b, 0, 0))
    return pl.pallas_call(
        _router_kernel,
        grid=(t // tm,),
        in_specs=[pl.BlockSpec((tm, d), lambda i: (i, 0)),
                  pl.BlockSpec((1, d), lambda i: (0, 0)),
                  row(), row(),
                  pl.BlockSpec((d, LANES), lambda i: (0, 0))],
        out_specs=[pl.BlockSpec((tm, d), lambda i: (i, 0)),
                   pl.BlockSpec((tm, LANES), lambda i: (i, 0)),
                   pl.BlockSpec((tm, LANES), lambda i: (i, 0))],
        out_shape=[jax.ShapeDtypeStruct((t, d), F32),
                   jax.ShapeDtypeStruct((t, LANES), jnp.int32),
                   jax.ShapeDtypeStruct((t, LANES), F32)],
        compiler_params=_cparams(("parallel",)),
        name="moe_router",
    )(x2, g.reshape(1, d), shift.reshape(bsz, 1, d), scale.reshape(bsz, 1, d), rw)


GATHER_UNROLL = 8
MOE_STEPS = 4
MOE_TILE_ROWS = 528


def _row_copy(src_ref, dst_ref, sem, src_row, dst_row):
    return pltpu.make_async_copy(src_ref.at[pl.ds(src_row, 1), :],
                                 dst_ref.at[pl.ds(dst_row, 1), :], sem)


def _start_row_gather(src_ref, dst_ref, sem, row_of, n_rows):
    def start(r, _):
        _row_copy(src_ref, dst_ref, sem, row_of(r), r).start()
        return 0

    lax.fori_loop(0, n_rows, start, 0, unroll=GATHER_UNROLL)


def _wait_row_gather(dst_ref, sem):
    pltpu.make_async_copy(dst_ref, dst_ref, sem).wait()


def _moe_ffn_kernel(tile_e_ref, tile_ok_ref, buf_t_ref, h_hbm, w1_ref, w3_ref, w2_ref, o_ref,
                    hf_ref, hb_ref, sem):
    i = pl.program_id(0)
    f = pl.program_id(1)
    n_tiles = pl.num_programs(0)
    nf = MOE_STEPS
    tm = hb_ref.shape[0]
    rows_per_step = tm // nf
    ok = tile_ok_ref[i] > 0
    last = (i == n_tiles - 1) & (f == nf - 1)
    nxt = jnp.minimum(i + 1, n_tiles - 1)
    started = (i == 0) | (tile_ok_ref[jnp.maximum(i - 1, 0)] > 0)

    @pl.when(f == 0)
    def _():
        @pl.when(i == 0)
        def _():
            _start_row_gather(h_hbm, hf_ref, sem, lambda r: buf_t_ref[r], tm)

        @pl.when(started)
        def _():
            _wait_row_gather(hf_ref, sem)

        @pl.when(ok)
        def _():
            hb_ref[...] = hf_ref[...].astype(hb_ref.dtype)
            o_ref[...] = jnp.zeros_like(o_ref)

    @pl.when(ok)
    def _():
        base = f * rows_per_step
        for r in range(rows_per_step):
            _row_copy(h_hbm, hf_ref, sem, buf_t_ref[nxt * tm + base + r], base + r).start()
        o_ref[...] += _dot(_swiglu_act(hb_ref[...], w1_ref, w3_ref), w2_ref[...])

    @pl.when(ok & last)
    def _():
        _wait_row_gather(hf_ref, sem)

    @pl.when(jnp.logical_not(ok) & (f == nf - 1))
    def _():
        o_ref[...] = jnp.zeros_like(o_ref)


def moe_expert_ffn(h2, tile_e, tile_ok, buf_t, w1, w3, w2, tm):
    t, d = h2.shape
    ffn = w1.shape[2]
    nf = MOE_STEPS
    tf = ffn // nf
    w1, w3, w2 = _chunk_cols(w1, nf), _chunk_cols(w3, nf), w2.astype(BF16)
    n_tiles = tile_e.shape[0]

    def fsel(i, f, ok):
        return jnp.where(ok[i] > 0, f, nf - 1)

    grid_spec = pltpu.PrefetchScalarGridSpec(
        num_scalar_prefetch=3,
        grid=(n_tiles, nf),
        in_specs=[pl.BlockSpec(memory_space=pl.ANY),
                  pl.BlockSpec((None, None, d, tf),
                               lambda i, f, te, ok, bt: (te[i], fsel(i, f, ok), 0, 0)),
                  pl.BlockSpec((None, None, d, tf),
                               lambda i, f, te, ok, bt: (te[i], fsel(i, f, ok), 0, 0)),
                  pl.BlockSpec((None, tf, d), lambda i, f, te, ok, bt: (te[i], fsel(i, f, ok), 0))],
        out_specs=pl.BlockSpec((tm, d), lambda i, f, te, ok, bt: (i, 0)),
        scratch_shapes=[pltpu.VMEM((tm, d), F32), pltpu.VMEM((tm, d), BF16),
                        pltpu.SemaphoreType.DMA(())],
    )
    return pl.pallas_call(
        _moe_ffn_kernel,
        grid_spec=grid_spec,
        out_shape=jax.ShapeDtypeStruct((n_tiles * tm, d), F32),
        compiler_params=_cparams(("arbitrary", "arbitrary"), VMEM_LIMIT_MOE),
        name="moe_expert_ffn",
    )(tile_e, tile_ok, buf_t, h2, w1, w3, w2)


def _moe_combine_kernel(d0_ref, d1_ref, y_hbm, x_ref, gate_ref, gates_ref, o_ref,
                        y0_ref, y1_ref, sems0, sems1):
    i = pl.program_id(0)
    tm = x_ref.shape[0]
    slot = i % 2

    def start_gather(tile, into):
        def start(r, _):
            _row_copy(y_hbm, y0_ref.at[into], sems0.at[into], d0_ref[tile * tm + r], r).start()
            _row_copy(y_hbm, y1_ref.at[into], sems1.at[into], d1_ref[tile * tm + r], r).start()
            return 0

        lax.fori_loop(0, tm, start, 0, unroll=GATHER_UNROLL)

    @pl.when(i == 0)
    def _():
        start_gather(0, 0)

    @pl.when(i + 1 < pl.num_programs(0))
    def _():
        start_gather(i + 1, 1 - slot)

    _wait_row_gather(y0_ref.at[slot], sems0.at[slot])
    _wait_row_gather(y1_ref.at[slot], sems1.at[slot])
    gates = gates_ref[...]
    y = gates[:, 0:1] * y0_ref[slot] + gates[:, 1:2] * y1_ref[slot]
    o_ref[...] = x_ref[...] + gate_ref[...] * y


def moe_combine_residual(dest0, dest1, y_buf, x2, gate, gates, seq, tm=256):
    t, d = x2.shape
    tm = min(tm, seq)
    per_b = seq // tm
    bsz = gate.shape[0]
    grid_spec = pltpu.PrefetchScalarGridSpec(
        num_scalar_prefetch=2,
        grid=(t // tm,),
        in_specs=[pl.BlockSpec(memory_space=pl.ANY),
                  pl.BlockSpec((tm, d), lambda i, d0, d1: (i, 0)),
                  pl.BlockSpec((None, 1, d), lambda i, d0, d1: (i // per_b, 0, 0)),
                  pl.BlockSpec((tm, LANES), lambda i, d0, d1: (i, 0))],
        out_specs=pl.BlockSpec((tm, d), lambda i, d0, d1: (i, 0)),
        scratch_shapes=[pltpu.VMEM((2, tm, d), F32), pltpu.VMEM((2, tm, d), F32),
                        pltpu.SemaphoreType.DMA((2,)), pltpu.SemaphoreType.DMA((2,))],
    )
    return pl.pallas_call(
        _moe_combine_kernel,
        grid_spec=grid_spec,
        out_shape=jax.ShapeDtypeStruct((t, d), F32),
        compiler_params=_cparams(("arbitrary",)),
        name="moe_combine_residual",
    )(dest0, dest1, y_buf, x2, gate.reshape(bsz, 1, d), gates)


def _moe_routing_tables(idx, tm):
    t = idx.shape[0]
    n_slots = t * MOE_TOPK
    n_tiles = n_slots // tm + N_EXPERTS
    flat_e = idx[:, :MOE_TOPK].reshape(-1)
    onehot = (flat_e[:, None] == jnp.arange(N_EXPERTS)[None, :]).astype(jnp.int32)
    counts = jnp.sum(onehot, axis=0)
    rank = jnp.sum((jnp.cumsum(onehot, axis=0) - onehot) * onehot, axis=1)
    padded = (counts + tm - 1) // tm * tm
    pad_ends = jnp.cumsum(padded)
    pad_starts = pad_ends - padded
    dest = (pad_starts[flat_e] + rank).astype(jnp.int32)
    buf_t = jnp.zeros((n_tiles * tm,), jnp.int32).at[dest].set(
        (jnp.arange(n_slots) // MOE_TOPK).astype(jnp.int32))
    tile_start = jnp.arange(n_tiles) * tm
    tile_e = jnp.minimum(jnp.searchsorted(pad_ends, tile_start, side="right"),
                         N_EXPERTS - 1).astype(jnp.int32)
    tile_ok = (tile_start < pad_ends[-1]).astype(jnp.int32)
    dest2 = dest.reshape(t, MOE_TOPK)
    return tile_e, tile_ok, buf_t, dest2[:, 0], dest2[:, 1]


def moe_ffn_residual(x2, g, shift, scale, gate, router_w, w1, w3, w2, seq, tile_rows=MOE_TILE_ROWS):
    h2, idx, gates = moe_router(x2, g, shift, scale, router_w, seq)
    tile_e, tile_ok, buf_t, dest0, dest1 = _moe_routing_tables(idx, tile_rows)
    y_buf = moe_expert_ffn(h2, tile_e, tile_ok, buf_t, w1, w3, w2, tile_rows)
    return moe_combine_residual(dest0, dest1, y_buf, x2, gate, gates, seq)


def _pad_w_in(w_in):
    d = w_in.shape[0]
    kpe_end = MLA_Q_RANK + MLA_KV_RANK + MLA_ROPE_DIM
    tail = PROJ_WIDTH - (COL_MOBA + 3 * GROUP_WIDTH)
    return jnp.concatenate([
        w_in[:, :kpe_end], jnp.zeros((d, LANES - MLA_ROPE_DIM), w_in.dtype),
        w_in[:, kpe_end:], jnp.zeros((d, tail), w_in.dtype)], axis=1).astype(BF16)


def _pad_heads_192(a):
    lead = a.shape[:-1]
    a = a.reshape(lead + (GROUP_HEADS, MLA_QK_DIM))
    a = jnp.pad(a, [(0, 0)] * len(lead) + [(0, 0), (0, MLA_QK_PAD - MLA_QK_DIM)])
    return a.reshape(lead + (GROUP_HEADS * MLA_QK_PAD,))


def kernel(x, c, positions, ada_w, ada_b, norm_mix_g, norm_ffn_g, w_in, mla_q_norm_g, mla_kv_norm_g, mla_w_uq, mla_w_ukv, mla_q_head_g, mla_k_head_g, moba_q_head_g, moba_k_head_g, group_norm_g, w_out, ffn_w1, ffn_w3, ffn_w2, router_w, moe_w1, moe_w3, moe_w2):
    bsz, seq, d = x.shape
    depth = ada_w.shape[0]
    cos_pe, sin_pe = _rope_tables(positions, MLA_ROPE_DIM)
    cos_full, sin_full = _rope_tables(positions, HEAD_DIM)
    mod = ada_modulation(c, ada_w, ada_b)
    x2 = x.reshape(bsz * seq, d)
    for l in range(depth):
        shift_m, scale_m, gate_m, shift_f, scale_f, gate_f = jnp.split(mod[l], 6, axis=-1)
        proj = norm_mod_proj(x2, norm_mix_g[l], shift_m, scale_m, _pad_w_in(w_in[l]), seq)
        proj3 = proj.reshape(bsz, seq, PROJ_WIDTH)
        pad_g = lambda g: jnp.pad(g, (0, MLA_QK_PAD - MLA_QK_DIM)).reshape(1, MLA_QK_PAD)
        q_mla, k_mla, v_mla = mla_prep(
            proj3, mla_q_norm_g[l], mla_kv_norm_g[l],
            _pad_heads_192(mla_w_uq[l]).astype(BF16), mla_w_ukv[l].astype(BF16),
            pad_g(mla_q_head_g[l]), pad_g(mla_k_head_g[l]), cos_pe, sin_pe)
        y_mla = mla_attention(q_mla, k_mla, v_mla)
        y_ret = retention(proj3, cos_full, sin_full)
        y_sb = sb_attention(proj3)
        q_mb, k_mb, kmean = moba_prep(proj3, moba_q_head_g[l], moba_k_head_g[l], cos_full, sin_full)
        y_moba = moba_attention(q_mb, k_mb, kmean, proj3)
        x2 = out_proj_residual((y_mla, y_ret, y_sb, y_moba), group_norm_g[l],
                               w_out[l].astype(BF16), x2, gate_m, seq)
        j = l // 2
        if l % 2 == 0:
            x2 = dense_ffn_residual(x2, norm_ffn_g[l], shift_f, scale_f, gate_f,
                                    ffn_w1[j], ffn_w3[j], ffn_w2[j], seq)
        else:
            x2 = moe_ffn_residual(x2, norm_ffn_g[l], shift_f, scale_f, gate_f, router_w[j],
                                  moe_w1[j], moe_w3[j], moe_w2[j], seq)
    return x2.reshape(bsz, seq, d)
```

```python
import functools

import jax
import jax.numpy as jnp
from jax import lax
from jax.experimental import pallas as pl
from jax.experimental.pallas import tpu as pltpu

F32 = jnp.float32
BF16 = jnp.bfloat16

HEAD_DIM = 128
GROUP_HEADS = 4
GROUP_WIDTH = 512
MLA_Q_RANK = 512
MLA_KV_RANK = 256
MLA_NOPE_DIM = 128
MLA_ROPE_DIM = 64
MLA_QK_DIM = MLA_NOPE_DIM + MLA_ROPE_DIM
MLA_QK_PAD = 256
RET_CHUNK = 128
MOBA_BLOCK = 256
MOBA_TOPK = 3
ROPE_THETA = 10000.0
NORM_EPS = 1e-6
NEG = -1e30
N_EXPERTS = 8
MOE_TOPK = 2

LANES = 128
ATTN_BLOCK = 256
VMEM_LIMIT = 56 * 1024 * 1024
VMEM_LIMIT_MOE = 60 * 1024 * 1024

COL_CQ = 0
COL_CKV = 512
COL_KPE = 768
COL_RET = 896
COL_SB = COL_RET + 4 * GROUP_WIDTH
COL_MOBA = COL_SB + 3 * GROUP_WIDTH
PROJ_WIDTH = 6144


def _cparams(sem, vmem=VMEM_LIMIT):
    return pltpu.CompilerParams(dimension_semantics=sem, vmem_limit_bytes=vmem)


def _dot(a, b):
    return jnp.dot(a, b, preferred_element_type=F32)


def _dot_nt(a, b):
    return lax.dot_general(a, b, (((1,), (1,)), ((), ())), preferred_element_type=F32)


def _sigmoid(x):
    return 1.0 / (1.0 + jnp.exp(-x))


def _softplus(z):
    return jnp.maximum(z, 0.0) + jnp.log(1.0 + jnp.exp(-jnp.abs(z)))


def _rms(xf, width=None):
    width = xf.shape[-1] if width is None else width
    ss = jnp.sum(xf * xf, axis=-1, keepdims=True) * (1.0 / width)
    return xf * lax.rsqrt(ss + NORM_EPS)


def _ada_kernel(c_ref, w_ref, b_ref, o_ref):
    c = c_ref[...]
    cond = c * _sigmoid(c)
    o_ref[...] = _dot(cond.astype(BF16), w_ref[...].astype(BF16)) + b_ref[...]


def ada_modulation(c, ada_w, ada_b, tn=1024):
    depth, d, n = ada_w.shape
    b = c.shape[0]
    return pl.pallas_call(
        _ada_kernel,
        grid=(depth, n // tn),
        in_specs=[
            pl.BlockSpec((b, d), lambda l, j: (0, 0)),
            pl.BlockSpec((None, d, tn), lambda l, j: (l, 0, j)),
            pl.BlockSpec((None, 1, tn), lambda l, j: (l, 0, j)),
        ],
        out_specs=pl.BlockSpec((None, b, tn), lambda l, j: (l, 0, j)),
        out_shape=jax.ShapeDtypeStruct((depth, b, n), F32),
        compiler_params=_cparams(("parallel", "parallel")),
        name="ada_modulation",
    )(c, ada_w, ada_b.reshape(depth, 1, n))


def _norm_mod_rows(x_ref, g_ref, sh_ref, sc_ref, dst_ref, rows=32):
    tm = x_ref.shape[0]
    mul = g_ref[...] * (1.0 + sc_ref[...])
    sh = sh_ref[...]

    def body(r, _):
        sl = pl.ds(pl.multiple_of(r * rows, rows), rows)
        dst_ref[sl, :] = (_rms(x_ref[sl, :]) * mul + sh).astype(dst_ref.dtype)
        return 0

    lax.fori_loop(0, tm // rows, body, 0, unroll=2)


def _proj_kernel(x_ref, g_ref, sh_ref, sc_ref, w_ref, o_ref, h_ref):
    @pl.when(pl.program_id(1) == 0)
    def _():
        _norm_mod_rows(x_ref, g_ref, sh_ref, sc_ref, h_ref)

    o_ref[...] = _dot(h_ref[...], w_ref[...]).astype(o_ref.dtype)


def norm_mod_proj(x2, g, shift, scale, w, seq, tm=1024, tn=2048):
    t, d = x2.shape
    n = w.shape[1]
    tm = min(tm, seq)
    per_b = seq // tm
    bsz = shift.shape[0]
    return pl.pallas_call(
        _proj_kernel,
        grid=(t // tm, n // tn),
        in_specs=[
            pl.BlockSpec((tm, d), lambda i, j: (i, 0)),
            pl.BlockSpec((1, d), lambda i, j: (0, 0)),
            pl.BlockSpec((None, 1, d), lambda i, j: (i // per_b, 0, 0)),
            pl.BlockSpec((None, 1, d), lambda i, j: (i // per_b, 0, 0)),
            pl.BlockSpec((d, tn), lambda i, j: (0, j)),
        ],
        out_specs=pl.BlockSpec((tm, tn), lambda i, j: (i, j)),
        out_shape=jax.ShapeDtypeStruct((t, n), BF16),
        scratch_shapes=[pltpu.VMEM((tm, d), BF16)],
        compiler_params=_cparams(("parallel", "arbitrary")),
        name="norm_mod_proj",
    )(x2, g.reshape(1, d), shift.reshape(bsz, 1, d), scale.reshape(bsz, 1, d), w)


def _rope_tables(positions, dim):
    inv_freq = ROPE_THETA ** (-jnp.arange(0, dim, 2, dtype=F32) / dim)
    ang = positions.astype(F32)[:, None] * inv_freq[None, :]
    cos, sin = jnp.cos(ang), jnp.sin(ang)
    pad = jnp.zeros((positions.shape[0], LANES - dim), F32)
    return (jnp.concatenate([cos, cos, pad], axis=-1),
            jnp.concatenate([-sin, sin, pad], axis=-1))


def _rope_full(z, cos2, sin_s):
    return z * cos2 + pltpu.roll(z, 64, 1) * sin_s


def _rope_64(z, cos2, sin_s):
    lane = lax.broadcasted_iota(jnp.int32, z.shape, 1)
    partner = jnp.where(lane < 32, pltpu.roll(z, 96, 1), pltpu.roll(z, 32, 1))
    return z * cos2 + partner * sin_s


def _mla_prep_kernel(p_ref, qg_ref, kvg_ref, wuq_ref, wukv_ref, qhg_ref, khg_ref,
                     cos_ref, sin_ref, q_ref, k_ref, v_ref):
    p = p_ref[...].astype(F32)
    c_q = p[:, COL_CQ:COL_CQ + MLA_Q_RANK]
    c_kv = p[:, COL_CKV:COL_CKV + MLA_KV_RANK]
    k_pe = p[:, COL_KPE:COL_KPE + LANES]
    q = _dot((_rms(c_q) * qg_ref[...]).astype(BF16), wuq_ref[...])
    kv = _dot((_rms(c_kv) * kvg_ref[...]).astype(BF16), wukv_ref[...])
    cos2, sin_s = cos_ref[...], sin_ref[...]
    qhg, khg = qhg_ref[...], khg_ref[...]
    scale = MLA_QK_DIM ** -0.5
    pe_ss = jnp.sum(k_pe * k_pe, axis=-1, keepdims=True)
    for h in range(GROUP_HEADS):
        qh = q[:, h * MLA_QK_PAD:(h + 1) * MLA_QK_PAD]
        qh = _rms(qh, MLA_QK_DIM) * qhg * scale
        q_ref[h, :, :LANES] = qh[:, :LANES].astype(q_ref.dtype)
        q_ref[h, :, LANES:] = _rope_64(qh[:, LANES:], cos2, sin_s).astype(q_ref.dtype)
        k_nope = kv[:, 2 * h * LANES:(2 * h + 1) * LANES]
        ss = (jnp.sum(k_nope * k_nope, axis=-1, keepdims=True) + pe_ss) * (1.0 / MLA_QK_DIM)
        r = lax.rsqrt(ss + NORM_EPS)
        k_ref[h, :, :LANES] = (k_nope * r * khg[:, :LANES]).astype(k_ref.dtype)
        k_ref[h, :, LANES:] = _rope_64(k_pe * r * khg[:, LANES:], cos2, sin_s).astype(k_ref.dtype)
        v_ref[h] = kv[:, (2 * h + 1) * LANES:(2 * h + 2) * LANES].astype(v_ref.dtype)


def mla_prep(proj3, q_norm_g, kv_norm_g, w_uq_pad, w_ukv, q_head_g_pad, k_head_g_pad,
             cos_pe, sin_pe, tm=512):
    bsz, seq, _ = proj3.shape
    h = GROUP_HEADS
    n_in = COL_KPE + LANES
    const = lambda shape: pl.BlockSpec(shape, lambda b, i: (0,) * len(shape))
    return pl.pallas_call(
        _mla_prep_kernel,
        grid=(bsz, seq // tm),
        in_specs=[
            pl.BlockSpec((None, tm, n_in), lambda b, i: (b, i, 0)),
            const((1, MLA_Q_RANK)), const((1, MLA_KV_RANK)),
            const((MLA_Q_RANK, h * MLA_QK_PAD)), const((MLA_KV_RANK, h * 2 * LANES)),
            const((1, MLA_QK_PAD)), const((1, MLA_QK_PAD)),
            pl.BlockSpec((tm, LANES), lambda b, i: (i, 0)),
            pl.BlockSpec((tm, LANES), lambda b, i: (i, 0)),
        ],
        out_specs=[
            pl.BlockSpec((None, h, tm, MLA_QK_PAD), lambda b, i: (b, 0, i, 0)),
            pl.BlockSpec((None, h, tm, MLA_QK_PAD), lambda b, i: (b, 0, i, 0)),
            pl.BlockSpec((None, h, tm, LANES), lambda b, i: (b, 0, i, 0)),
        ],
        out_shape=[
            jax.ShapeDtypeStruct((bsz, h, seq, MLA_QK_PAD), BF16),
            jax.ShapeDtypeStruct((bsz, h, seq, MLA_QK_PAD), BF16),
            jax.ShapeDtypeStruct((bsz, h, seq, LANES), BF16),
        ],
        compiler_params=_cparams(("parallel", "parallel")),
        name="mla_prep",
    )(proj3, q_norm_g.reshape(1, -1), kv_norm_g.reshape(1, -1), w_uq_pad, w_ukv,
      q_head_g_pad, k_head_g_pad, cos_pe, sin_pe)


def _moba_gates(q_ref, kmean_ref):
    km = kmean_ref[...]
    nb = km.shape[0]
    hi = km.astype(BF16).astype(F32)
    mid = (km - hi).astype(BF16).astype(F32)
    lo = (km - hi - mid).astype(BF16).astype(F32)
    pieces = jnp.concatenate([hi, mid, lo, jnp.zeros_like(km)], axis=0).astype(BF16)
    r = _dot_nt(pieces, q_ref[...])
    return r[:nb] + r[nb:2 * nb] + r[2 * nb:3 * nb]


def _moba_block_choice(gate, n_past):
    blk = lax.broadcasted_iota(jnp.int32, gate.shape, 0)
    rank = jnp.zeros(gate.shape, jnp.int32)
    for o in range(n_past):
        g_o = gate[o:o + 1, :]
        tie = jnp.where(o < blk, 1, 0)
        rank = rank + jnp.where(g_o > gate, 1, jnp.where(g_o == gate, tie, 0))
    return jnp.where(rank < MOBA_TOPK, 1.0, 0.0)


def _attn_kernel(*refs, moba):
    if moba:
        q_ref, k_ref, v_ref, kmean_ref, o_ref = refs
        gates = _moba_gates(q_ref, kmean_ref)
    else:
        q_ref, k_ref, v_ref, o_ref = refs
    seq = q_ref.shape[0]
    t = ATTN_BLOCK
    v_t = jnp.transpose(v_ref[...].astype(F32)).astype(BF16)
    key = lax.broadcasted_iota(jnp.int32, (t, t), 0)
    qry = lax.broadcasted_iota(jnp.int32, (t, t), 1)
    causal = key <= qry
    for i in range(seq // t):
        own = slice(i * t, (i + 1) * t)
        q = q_ref[own, :]
        s_own = jnp.where(causal, _dot_nt(k_ref[own, :], q), NEG)
        m = jnp.max(s_own, axis=0, keepdims=True)
        s_past = []
        if i > 0:
            s_all = _dot_nt(k_ref[:i * t, :], q)
            if moba and i > MOBA_TOPK:
                sel = _moba_block_choice(gates[:, own], i)
                s_past = [jnp.where(sel[n:n + 1, :] > 0.5, s_all[n * t:(n + 1) * t, :], NEG)
                          for n in range(i)]
            else:
                s_past = [s_all]
            for s in s_past:
                m = jnp.maximum(m, jnp.max(s, axis=0, keepdims=True))
        p = jnp.exp(s_own - m)
        l = jnp.sum(p, axis=0, keepdims=True)
        acc_t = _dot(v_t[:, own], p.astype(BF16))
        start = 0
        for s in s_past:
            p = jnp.exp(s - m)
            l = l + jnp.sum(p, axis=0, keepdims=True)
            acc_t = acc_t + _dot(v_t[:, start:start + s.shape[0]], p.astype(BF16))
            start += s.shape[0]
        o_ref[own, :] = jnp.transpose(acc_t / l).astype(o_ref.dtype)


def _head_spec(seq, d):
    return pl.BlockSpec((None, None, seq, d), lambda b, hh: (b, hh, 0, 0))


def _proj_col_spec(seq, col):
    cb = col // LANES
    return pl.BlockSpec((None, seq, LANES), lambda b, hh: (b, 0, cb + hh))


def _attention(q, k, v, v_spec, kmean=None):
    bsz, h, seq, dk = q.shape
    in_specs = [_head_spec(seq, dk), _head_spec(seq, dk), v_spec]
    args = [q, k, v]
    if kmean is not None:
        in_specs.append(_head_spec(kmean.shape[2], LANES))
        args.append(kmean)
    return pl.pallas_call(
        functools.partial(_attn_kernel, moba=kmean is not None),
        grid=(bsz, h),
        in_specs=in_specs,
        out_specs=pl.BlockSpec((None, seq, LANES), lambda b, hh: (b, 0, hh)),
        out_shape=jax.ShapeDtypeStruct((bsz, seq, h * LANES), BF16),
        compiler_params=_cparams(("parallel", "parallel")),
        name="moba_attention" if kmean is not None else "mla_attention",
    )(*args)


def mla_attention(q, k, v):
    return _attention(q, k, v, _head_spec(q.shape[2], LANES))


def _moba_prep_kernel(q_ref, k_ref, qg_ref, kg_ref, cos_ref, sin_ref, qo_ref, ko_ref, km_ref):
    seq = q_ref.shape[0]
    scale = HEAD_DIM ** -0.5
    for blk in range(seq // MOBA_BLOCK):
        sl = slice(blk * MOBA_BLOCK, (blk + 1) * MOBA_BLOCK)
        cos2, sin_s = cos_ref[sl, :], sin_ref[sl, :]
        qn = _rms(q_ref[sl, :].astype(F32)) * qg_ref[...]
        qo_ref[sl, :] = (_rope_full(qn, cos2, sin_s) * scale).astype(qo_ref.dtype)
        kn = _rope_full(_rms(k_ref[sl, :].astype(F32)) * kg_ref[...], cos2, sin_s)
        ko_ref[sl, :] = kn.astype(ko_ref.dtype)
        km_ref[blk:blk + 1, :] = jnp.mean(kn, axis=0, keepdims=True)


def moba_prep(proj3, q_head_g, k_head_g, cos_full, sin_full):
    bsz, seq, _ = proj3.shape
    h = GROUP_HEADS
    nb = seq // MOBA_BLOCK
    const = lambda shape: pl.BlockSpec(shape, lambda b, hh: (0,) * len(shape))
    return pl.pallas_call(
        _moba_prep_kernel,
        grid=(bsz, h),
        in_specs=[
            _proj_col_spec(seq, COL_MOBA),
            _proj_col_spec(seq, COL_MOBA + GROUP_WIDTH),
            const((1, LANES)), const((1, LANES)),
            const((seq, LANES)), const((seq, LANES)),
        ],
        out_specs=[_head_spec(seq, LANES), _head_spec(seq, LANES), _head_spec(nb, LANES)],
        out_shape=[
            jax.ShapeDtypeStruct((bsz, h, seq, LANES), BF16),
            jax.ShapeDtypeStruct((bsz, h, seq, LANES), BF16),
            jax.ShapeDtypeStruct((bsz, h, nb, LANES), F32),
        ],
        compiler_params=_cparams(("parallel", "parallel")),
        name="moba_prep",
    )(proj3, proj3, q_head_g.reshape(1, -1), k_head_g.reshape(1, -1), cos_full, sin_full)


def moba_attention(q, k, kmean, proj3):
    v_spec = _proj_col_spec(q.shape[2], COL_MOBA + 2 * GROUP_WIDTH)
    return _attention(q, k, proj3, v_spec, kmean=kmean)


SB_DEAD_TAIL = -104.0


def _sb_tile(q, k_blk, v_blk, tail, later, strict):
    z = _dot_nt(q, k_blk) * (HEAD_DIM ** -0.5)
    log_1m = -_softplus(z)
    if strict is not None:
        log_1m = jnp.where(strict, log_1m, 0.0)
    hi = log_1m.astype(BF16)
    lo = (log_1m - hi.astype(F32)).astype(BF16)
    suffix = _dot(hi, later) + _dot(lo, later)
    a = jnp.exp(z + log_1m + suffix + tail)
    if strict is not None:
        a = jnp.where(strict, a, 0.0)
    return _dot(a.astype(v_blk.dtype), v_blk), jnp.sum(log_1m, axis=-1, keepdims=True)


def _sb_kernel(q_ref, k_ref, v_ref, o_ref, acc_ref, tail_ref):
    seq = q_ref.shape[0]
    t = ATTN_BLOCK
    row = lax.broadcasted_iota(jnp.int32, (t, t), 0)
    col = lax.broadcasted_iota(jnp.int32, (t, t), 1)
    strict = col < row
    later = jnp.where(row > col, 1.0, 0.0).astype(BF16)
    blk = lambda n: slice(n * t, (n + 1) * t)

    for i in range(seq // t):
        q = q_ref[blk(i), :]
        acc, tail = _sb_tile(q, k_ref[blk(i), :], v_ref[blk(i), :], 0.0, later, strict)
        if i > 0:
            y, s = _sb_tile(q, k_ref[blk(i - 1), :], v_ref[blk(i - 1), :], tail, later, None)
            acc, tail = acc + y, tail + s
        acc_ref[blk(i), :] = acc
        tail_ref[blk(i), :] = tail

    for i in range(2, seq // t):
        alive = jnp.max(tail_ref[blk(i), :], axis=0, keepdims=True)[0, 0] > SB_DEAD_TAIL

        @pl.when(alive)
        def _():
            q = q_ref[blk(i), :]
            acc, tail = acc_ref[blk(i), :], tail_ref[blk(i), :]
            for n in range(i - 2, -1, -1):
                y, s = _sb_tile(q, k_ref[blk(n), :], v_ref[blk(n), :], tail, later, None)
                acc, tail = acc + y, tail + s
            acc_ref[blk(i), :] = acc

    o_ref[...] = acc_ref[...].astype(o_ref.dtype)


def sb_attention(proj3):
    bsz, seq, _ = proj3.shape
    return pl.pallas_call(
        _sb_kernel,
        grid=(bsz, GROUP_HEADS),
        in_specs=[
            _proj_col_spec(seq, COL_SB),
            _proj_col_spec(seq, COL_SB + GROUP_WIDTH),
            _proj_col_spec(seq, COL_SB + 2 * GROUP_WIDTH),
        ],
        out_specs=pl.BlockSpec((None, seq, LANES), lambda b, hh: (b, 0, hh)),
        out_shape=jax.ShapeDtypeStruct((bsz, seq, GROUP_WIDTH), BF16),
        scratch_shapes=[pltpu.VMEM((seq, LANES), F32), pltpu.VMEM((seq, 1), F32)],
        compiler_params=_cparams(("parallel", "parallel")),
        name="sb_attention",
    )(proj3, proj3, proj3)


def _ret_kernel(q_ref, k_ref, v_ref, g_ref, cos_ref, sin_ref, lg_ref, o_ref):
    seq = q_ref.shape[0]
    c = RET_CHUNK
    log_gamma = lg_ref[...]
    ri = lax.broadcasted_iota(jnp.int32, (c, c), 0).astype(F32)
    ci = lax.broadcasted_iota(jnp.int32, (c, c), 1).astype(F32)
    rel = ri - ci
    intra_decay = jnp.where(rel >= 0, jnp.exp(jnp.maximum(rel, 0.0) * log_gamma), 0.0)
    idx = lax.broadcasted_iota(jnp.int32, (c, 1), 0).astype(F32)
    query_decay = jnp.exp((idx + 1.0) * log_gamma)
    key_decay = jnp.exp((c - 1.0 - idx) * log_gamma)
    chunk_decay = jnp.exp(c * log_gamma)
    k_scale = HEAD_DIM ** -0.5

    state = jnp.zeros((HEAD_DIM, HEAD_DIM), F32)
    for n in range(seq // c):
        sl = slice(n * c, (n + 1) * c)
        cos2, sin_s = cos_ref[sl, :], sin_ref[sl, :]
        q = _rope_full(q_ref[sl, :].astype(F32), cos2, sin_s)
        k = _rope_full(k_ref[sl, :].astype(F32), cos2, sin_s) * k_scale
        vb = v_ref[sl, :]
        qb = q.astype(BF16)
        scores = _dot_nt(qb, k.astype(BF16)) * intra_decay
        y = _dot(scores.astype(BF16), vb)
        y = y + _dot(qb, state.astype(BF16)) * query_decay
        kd_t = jnp.transpose(k * key_decay).astype(BF16)
        state = state * chunk_decay + _dot(kd_t, vb)
        mu = jnp.mean(y, axis=-1, keepdims=True)
        yc = y - mu
        var = jnp.mean(yc * yc, axis=-1, keepdims=True)
        yn = yc * lax.rsqrt(var + NORM_EPS)
        g = g_ref[sl, :].astype(F32)
        o_ref[sl, :] = (g * _sigmoid(g) * yn).astype(o_ref.dtype)


def retention(proj3, cos_full, sin_full):
    bsz, seq, _ = proj3.shape
    h = GROUP_HEADS
    log_gamma = jnp.log(1.0 - 2.0 ** (-5.0 - jnp.arange(h, dtype=F32))).reshape(h, 1, 1)
    col = lambda k: pl.BlockSpec((None, seq, LANES),
                                 lambda b, hh: (b, 0, (COL_RET + k * GROUP_WIDTH) // LANES + hh))
    const = lambda shape: pl.BlockSpec(shape, lambda b, hh: (0,) * len(shape))
    return pl.pallas_call(
        _ret_kernel,
        grid=(bsz, h),
        in_specs=[col(0), col(1), col(2), col(3), const((seq, LANES)), const((seq, LANES)),
                  pl.BlockSpec((None, 1, 1), lambda b, hh: (hh, 0, 0))],
        out_specs=pl.BlockSpec((None, seq, LANES), lambda b, hh: (b, 0, hh)),
        out_shape=jax.ShapeDtypeStruct((bsz, seq, GROUP_WIDTH), BF16),
        compiler_params=_cparams(("parallel", "parallel")),
        name="retention",
    )(proj3, proj3, proj3, proj3, cos_full, sin_full, log_gamma)


def _out_proj_kernel(y0_ref, y1_ref, y2_ref, y3_ref, gg_ref, w_ref, x_ref, gate_ref, o_ref, h_ref):
    @pl.when(pl.program_id(1) == 0)
    def _():
        for grp, y_ref in enumerate((y0_ref, y1_ref, y2_ref, y3_ref)):
            yn = _rms(y_ref[...].astype(F32)) * gg_ref[grp:grp + 1, :]
            h_ref[:, grp * GROUP_WIDTH:(grp + 1) * GROUP_WIDTH] = yn.astype(h_ref.dtype)

    o_ref[...] = x_ref[...] + gate_ref[...] * _dot(h_ref[...], w_ref[...])


def out_proj_residual(ys, group_g, w_out, x2, gate, seq, tm=1024, tn=1024):
    t, d = x2.shape
    tm = min(tm, seq)
    per_b = seq // tm
    bsz = gate.shape[0]
    y_spec = pl.BlockSpec((tm, GROUP_WIDTH), lambda i, j: (i, 0))
    return pl.pallas_call(
        _out_proj_kernel,
        grid=(t // tm, d // tn),
        in_specs=[y_spec, y_spec, y_spec, y_spec,
                  pl.BlockSpec((4, GROUP_WIDTH), lambda i, j: (0, 0)),
                  pl.BlockSpec((4 * GROUP_WIDTH, tn), lambda i, j: (0, j)),
                  pl.BlockSpec((tm, tn), lambda i, j: (i, j)),
                  pl.BlockSpec((None, 1, tn), lambda i, j: (i // per_b, 0, j))],
        out_specs=pl.BlockSpec((tm, tn), lambda i, j: (i, j)),
        out_shape=jax.ShapeDtypeStruct((t, d), F32),
        scratch_shapes=[pltpu.VMEM((tm, 4 * GROUP_WIDTH), BF16)],
        compiler_params=_cparams(("parallel", "arbitrary")),
        name="out_proj_residual",
    )(*[y.reshape(t, GROUP_WIDTH) for y in ys], group_g, w_out, x2, gate.reshape(bsz, 1, d))


def _swiglu_act(h, w1_ref, w3_ref):
    a = _dot(h, w1_ref[...])
    return (a * _sigmoid(a) * _dot(h, w3_ref[...])).astype(BF16)


def _ffn_kernel(x_ref, g_ref, sh_ref, sc_ref, gate_ref, w1_ref, w3_ref, w2_ref, o_ref, h_ref):
    f = pl.program_id(1)

    @pl.when(f == 0)
    def _():
        _norm_mod_rows(x_ref, g_ref, sh_ref, sc_ref, h_ref)
        o_ref[...] = jnp.zeros_like(o_ref)

    o_ref[...] += _dot(_swiglu_act(h_ref[...], w1_ref, w3_ref), w2_ref[...])

    @pl.when(f == pl.num_programs(1) - 1)
    def _():
        o_ref[...] = x_ref[...] + gate_ref[...] * o_ref[...]


def dense_ffn_residual(x2, g, shift, scale, gate, w1, w3, w2, seq, tm=1024, tf=512):
    t, d = x2.shape
    ffn = w1.shape[1]
    w1, w3, w2 = w1.astype(BF16), w3.astype(BF16), w2.astype(BF16)
    tm = min(tm, seq)
    per_b = seq // tm
    bsz = gate.shape[0]
    row = lambda: pl.BlockSpec((None, 1, d), lambda i, f: (i // per_b, 0, 0))
    return pl.pallas_call(
        _ffn_kernel,
        grid=(t // tm, ffn // tf),
        in_specs=[pl.BlockSpec((tm, d), lambda i, f: (i, 0), pipeline_mode=pl.Buffered(1)),
                  pl.BlockSpec((1, d), lambda i, f: (0, 0)),
                  row(), row(), row(),
                  pl.BlockSpec((d, tf), lambda i, f: (0, f)),
                  pl.BlockSpec((d, tf), lambda i, f: (0, f)),
                  pl.BlockSpec((tf, d), lambda i, f: (f, 0))],
        out_specs=pl.BlockSpec((tm, d), lambda i, f: (i, 0)),
        out_shape=jax.ShapeDtypeStruct((t, d), F32),
        scratch_shapes=[pltpu.VMEM((tm, d), BF16)],
        compiler_params=_cparams(("parallel", "arbitrary")),
        name="dense_ffn_residual",
    )(x2, g.reshape(1, d), shift.reshape(bsz, 1, d), scale.reshape(bsz, 1, d),
      gate.reshape(bsz, 1, d), w1, w3, w2)


def _router_kernel(x_ref, g_ref, sh_ref, sc_ref, rw_ref, h_ref, idx_ref, gates_ref):
    _norm_mod_rows(x_ref, g_ref, sh_ref, sc_ref, h_ref)
    logits = jnp.dot(h_ref[...], rw_ref[...], precision=lax.Precision.HIGHEST,
                     preferred_element_type=F32)
    lane = lax.broadcasted_iota(jnp.int32, logits.shape, 1)
    lane_f = lane.astype(F32)
    logits = jnp.where(lane < N_EXPERTS, logits, -jnp.inf)
    m0 = jnp.max(logits, axis=-1, keepdims=True)
    e0 = jnp.min(jnp.where(logits == m0, lane_f, float(LANES)), axis=-1, keepdims=True)
    rest = jnp.where(lane_f == e0, -jnp.inf, logits)
    m1 = jnp.max(rest, axis=-1, keepdims=True)
    e1 = jnp.min(jnp.where(rest == m1, lane_f, float(LANES)), axis=-1, keepdims=True)
    p1 = jnp.exp(m1 - m0)
    g0 = 1.0 / (1.0 + p1)
    idx_ref[...] = jnp.where(lane == 0, e0, jnp.where(lane == 1, e1, 0.0)).astype(jnp.int32)
    gates_ref[...] = jnp.where(lane == 0, g0, jnp.where(lane == 1, p1 * g0, 0.0))


def moe_router(x2, g, shift, scale, router_w, seq, tm=256):
    t, d = x2.shape
    tm = min(tm, seq)
    per_b = seq // tm
    bsz = shift.shape[0]
    rw = jnp.zeros((d, LANES), F32).at[:, :N_EXPERTS].set(router_w)
    row = lambda: pl.BlockSpec((None, 1, d), lambda i: (i // per_b, 0, 0))
    return pl.pallas_call(
        _router_kernel,
        grid=(t // tm,),
        in_specs=[pl.BlockSpec((tm, d), lambda i: (i, 0)),
                  pl.BlockSpec((1, d), lambda i: (0, 0)),
                  row(), row(),
                  pl.BlockSpec((d, LANES), lambda i: (0, 0))],
        out_specs=[pl.BlockSpec((tm, d), lambda i: (i, 0)),
                   pl.BlockSpec((tm, LANES), lambda i: (i, 0)),
                   pl.BlockSpec((tm, LANES), lambda i: (i, 0))],
        out_shape=[jax.ShapeDtypeStruct((t, d), F32),
                   jax.ShapeDtypeStruct((t, LANES), jnp.int32),
                   jax.ShapeDtypeStruct((t, LANES), F32)],
        compiler_params=_cparams(("parallel",)),
        name="moe_router",
    )(x2, g.reshape(1, d), shift.reshape(bsz, 1, d), scale.reshape(bsz, 1, d), rw)


GATHER_UNROLL = 8
MOE_STEPS = 11
MOE_TILE_ROWS = 1056


def _row_copy(src_ref, dst_ref, sem, src_row, dst_row):
    return pltpu.make_async_copy(src_ref.at[pl.ds(src_row, 1), :],
                                 dst_ref.at[pl.ds(dst_row, 1), :], sem)


def _start_row_gather(src_ref, dst_ref, sem, row_of, n_rows):
    def start(r, _):
        _row_copy(src_ref, dst_ref, sem, row_of(r), r).start()
        return 0

    lax.fori_loop(0, n_rows, start, 0, unroll=GATHER_UNROLL)


def _wait_row_gather(dst_ref, sem):
    pltpu.make_async_copy(dst_ref, dst_ref, sem).wait()


def _moe_ffn_kernel(tile_e_ref, tile_ok_ref, buf_t_ref, h_hbm, w1_ref, w3_ref, w2_ref, o_ref,
                    hf_ref, hb_ref, sem):
    i = pl.program_id(0)
    f = pl.program_id(1)
    n_tiles = pl.num_programs(0)
    nf = MOE_STEPS
    tm = hb_ref.shape[0]
    rows_per_step = tm // nf
    ok = tile_ok_ref[i] > 0
    last = (i == n_tiles - 1) & (f == nf - 1)
    nxt = jnp.minimum(i + 1, n_tiles - 1)
    started = (i == 0) | (tile_ok_ref[jnp.maximum(i - 1, 0)] > 0)

    @pl.when(f == 0)
    def _():
        @pl.when(i == 0)
        def _():
            _start_row_gather(h_hbm, hf_ref, sem, lambda r: buf_t_ref[r], tm)

        @pl.when(started)
        def _():
            _wait_row_gather(hf_ref, sem)

        @pl.when(ok)
        def _():
            hb_ref[...] = hf_ref[...].astype(hb_ref.dtype)
            o_ref[...] = jnp.zeros_like(o_ref)

    @pl.when(ok)
    def _():
        base = f * rows_per_step
        for r in range(rows_per_step):
            _row_copy(h_hbm, hf_ref, sem, buf_t_ref[nxt * tm + base + r], base + r).start()
        o_ref[...] += _dot(_swiglu_act(hb_ref[...], w1_ref, w3_ref), w2_ref[...])

    @pl.when(ok & last)
    def _():
        _wait_row_gather(hf_ref, sem)

    @pl.when(jnp.logical_not(ok) & (f == nf - 1))
    def _():
        o_ref[...] = jnp.zeros_like(o_ref)


def moe_expert_ffn(h2, tile_e, tile_ok, buf_t, w1, w3, w2, tm):
    t, d = h2.shape
    ffn = w1.shape[2]
    nf = MOE_STEPS
    tf = ffn // nf
    w1, w3, w2 = w1.astype(BF16), w3.astype(BF16), w2.astype(BF16)
    n_tiles = tile_e.shape[0]

    def fsel(i, f, ok):
        return jnp.where(ok[i] > 0, f, nf - 1)

    grid_spec = pltpu.PrefetchScalarGridSpec(
        num_scalar_prefetch=3,
        grid=(n_tiles, nf),
        in_specs=[pl.BlockSpec(memory_space=pl.ANY),
                  pl.BlockSpec((None, d, tf), lambda i, f, te, ok, bt: (te[i], 0, fsel(i, f, ok))),
                  pl.BlockSpec((None, d, tf), lambda i, f, te, ok, bt: (te[i], 0, fsel(i, f, ok))),
                  pl.BlockSpec((None, tf, d), lambda i, f, te, ok, bt: (te[i], fsel(i, f, ok), 0))],
        out_specs=pl.BlockSpec((tm, d), lambda i, f, te, ok, bt: (i, 0)),
        scratch_shapes=[pltpu.VMEM((tm, d), F32), pltpu.VMEM((tm, d), BF16),
                        pltpu.SemaphoreType.DMA(())],
    )
    return pl.pallas_call(
        _moe_ffn_kernel,
        grid_spec=grid_spec,
        out_shape=jax.ShapeDtypeStruct((n_tiles * tm, d), F32),
        compiler_params=_cparams(("arbitrary", "arbitrary"), VMEM_LIMIT_MOE),
        name="moe_expert_ffn",
    )(tile_e, tile_ok, buf_t, h2, w1, w3, w2)


def _moe_combine_kernel(d0_ref, d1_ref, y_hbm, x_ref, gate_ref, gates_ref, o_ref,
                        y0_ref, y1_ref, sems0, sems1):
    i = pl.program_id(0)
    tm = x_ref.shape[0]
    slot = i % 2

    def start_gather(tile, into):
        def start(r, _):
            _row_copy(y_hbm, y0_ref.at[into], sems0.at[into], d0_ref[tile * tm + r], r).start()
            _row_copy(y_hbm, y1_ref.at[into], sems1.at[into], d1_ref[tile * tm + r], r).start()
            return 0

        lax.fori_loop(0, tm, start, 0, unroll=GATHER_UNROLL)

    @pl.when(i == 0)
    def _():
        start_gather(0, 0)

    @pl.when(i + 1 < pl.num_programs(0))
    def _():
        start_gather(i + 1, 1 - slot)

    _wait_row_gather(y0_ref.at[slot], sems0.at[slot])
    _wait_row_gather(y1_ref.at[slot], sems1.at[slot])
    gates = gates_ref[...]
    y = gates[:, 0:1] * y0_ref[slot] + gates[:, 1:2] * y1_ref[slot]
    o_ref[...] = x_ref[...] + gate_ref[...] * y


def moe_combine_residual(dest0, dest1, y_buf, x2, gate, gates, seq, tm=256):
    t, d = x2.shape
    tm = min(tm, seq)
    per_b = seq // tm
    bsz = gate.shape[0]
    grid_spec = pltpu.PrefetchScalarGridSpec(
        num_scalar_prefetch=2,
        grid=(t // tm,),
        in_specs=[pl.BlockSpec(memory_space=pl.ANY),
                  pl.BlockSpec((tm, d), lambda i, d0, d1: (i, 0)),
                  pl.BlockSpec((None, 1, d), lambda i, d0, d1: (i // per_b, 0, 0)),
                  pl.BlockSpec((tm, LANES), lambda i, d0, d1: (i, 0))],
        out_specs=pl.BlockSpec((tm, d), lambda i, d0, d1: (i, 0)),
        scratch_shapes=[pltpu.VMEM((2, tm, d), F32), pltpu.VMEM((2, tm, d), F32),
                        pltpu.SemaphoreType.DMA((2,)), pltpu.SemaphoreType.DMA((2,))],
    )
    return pl.pallas_call(
        _moe_combine_kernel,
        grid_spec=grid_spec,
        out_shape=jax.ShapeDtypeStruct((t, d), F32),
        compiler_params=_cparams(("arbitrary",)),
        name="moe_combine_residual",
    )(dest0, dest1, y_buf, x2, gate.reshape(bsz, 1, d), gates)


def _moe_routing_tables(idx, tm):
    t = idx.shape[0]
    n_slots = t * MOE_TOPK
    n_tiles = n_slots // tm + N_EXPERTS
    flat_e = idx[:, :MOE_TOPK].reshape(-1)
    onehot = (flat_e[:, None] == jnp.arange(N_EXPERTS)[None, :]).astype(jnp.int32)
    counts = jnp.sum(onehot, axis=0)
    rank = jnp.sum((jnp.cumsum(onehot, axis=0) - onehot) * onehot, axis=1)
    padded = (counts + tm - 1) // tm * tm
    pad_ends = jnp.cumsum(padded)
    pad_starts = pad_ends - padded
    dest = (pad_starts[flat_e] + rank).astype(jnp.int32)
    buf_t = jnp.zeros((n_tiles * tm,), jnp.int32).at[dest].set(
        (jnp.arange(n_slots) // MOE_TOPK).astype(jnp.int32))
    tile_start = jnp.arange(n_tiles) * tm
    tile_e = jnp.minimum(jnp.searchsorted(pad_ends, tile_start, side="right"),
                         N_EXPERTS - 1).astype(jnp.int32)
    tile_ok = (tile_start < pad_ends[-1]).astype(jnp.int32)
    dest2 = dest.reshape(t, MOE_TOPK)
    return tile_e, tile_ok, buf_t, dest2[:, 0], dest2[:, 1]


def moe_ffn_residual(x2, g, shift, scale, gate, router_w, w1, w3, w2, seq, tile_rows=MOE_TILE_ROWS):
    h2, idx, gates = moe_router(x2, g, shift, scale, router_w, seq)
    tile_e, tile_ok, buf_t, dest0, dest1 = _moe_routing_tables(idx, tile_rows)
    y_buf = moe_expert_ffn(h2, tile_e, tile_ok, buf_t, w1, w3, w2, tile_rows)
    return moe_combine_residual(dest0, dest1, y_buf, x2, gate, gates, seq)


def _pad_w_in(w_in):
    d = w_in.shape[0]
    kpe_end = MLA_Q_RANK + MLA_KV_RANK + MLA_ROPE_DIM
    tail = PROJ_WIDTH - (COL_MOBA + 3 * GROUP_WIDTH)
    w = w_in.astype(BF16)
    return jnp.concatenate([
        w[:, :kpe_end], jnp.zeros((d, LANES - MLA_ROPE_DIM), BF16),
        w[:, kpe_end:], jnp.zeros((d, tail), BF16)], axis=1)


def _pad_heads_192(a):
    lead = a.shape[:-1]
    a = a.reshape(lead + (GROUP_HEADS, MLA_QK_DIM))
    a = jnp.pad(a, [(0, 0)] * len(lead) + [(0, 0), (0, MLA_QK_PAD - MLA_QK_DIM)])
    return a.reshape(lead + (GROUP_HEADS * MLA_QK_PAD,))


def kernel(x, c, positions, ada_w, ada_b, norm_mix_g, norm_ffn_g, w_in, mla_q_norm_g, mla_kv_norm_g, mla_w_uq, mla_w_ukv, mla_q_head_g, mla_k_head_g, moba_q_head_g, moba_k_head_g, group_norm_g, w_out, ffn_w1, ffn_w3, ffn_w2, router_w, moe_w1, moe_w3, moe_w2):
    bsz, seq, d = x.shape
    depth = ada_w.shape[0]
    cos_pe, sin_pe = _rope_tables(positions, MLA_ROPE_DIM)
    cos_full, sin_full = _rope_tables(positions, HEAD_DIM)
    mod = ada_modulation(c, ada_w, ada_b)
    x2 = x.reshape(bsz * seq, d)
    for l in range(depth):
        shift_m, scale_m, gate_m, shift_f, scale_f, gate_f = jnp.split(mod[l], 6, axis=-1)
        proj = norm_mod_proj(x2, norm_mix_g[l], shift_m, scale_m, _pad_w_in(w_in[l]), seq)
        proj3 = proj.reshape(bsz, seq, PROJ_WIDTH)
        pad_g = lambda g: jnp.pad(g, (0, MLA_QK_PAD - MLA_QK_DIM)).reshape(1, MLA_QK_PAD)
        q_mla, k_mla, v_mla = mla_prep(
            proj3, mla_q_norm_g[l], mla_kv_norm_g[l],
            _pad_heads_192(mla_w_uq[l]).astype(BF16), mla_w_ukv[l].astype(BF16),
            pad_g(mla_q_head_g[l]), pad_g(mla_k_head_g[l]), cos_pe, sin_pe)
        y_mla = mla_attention(q_mla, k_mla, v_mla)
        y_ret = retention(proj3, cos_full, sin_full)
        y_sb = sb_attention(proj3)
        q_mb, k_mb, kmean = moba_prep(proj3, moba_q_head_g[l], moba_k_head_g[l], cos_full, sin_full)
        y_moba = moba_attention(q_mb, k_mb, kmean, proj3)
        x2 = out_proj_residual((y_mla, y_ret, y_sb, y_moba), group_norm_g[l],
                               w_out[l].astype(BF16), x2, gate_m, seq)
        j = l // 2
        if l % 2 == 0:
            x2 = dense_ffn_residual(x2, norm_ffn_g[l], shift_f, scale_f, gate_f,
                                    ffn_w1[j], ffn_w3[j], ffn_w2[j], seq)
        else:
            x2 = moe_ffn_residual(x2, norm_ffn_g[l], shift_f, scale_f, gate_f, router_w[j],
                                  moe_w1[j], moe_w3[j], moe_w2[j], seq)
    return x2.reshape(bsz, seq, d)
```

```python
import jax
import jax.numpy as jnp
from jax import lax
from jax.experimental import pallas as pl
from jax.experimental.pallas import tpu as pltpu

F32 = jnp.float32
BF16 = jnp.bfloat16

HEAD_DIM = 128
GROUP_HEADS = 4
GROUP_WIDTH = 512
MLA_Q_RANK = 512
MLA_KV_RANK = 256
MLA_NOPE_DIM = 128
MLA_ROPE_DIM = 64
MLA_QK_DIM = MLA_NOPE_DIM + MLA_ROPE_DIM
MLA_QK_PAD = 256
RET_CHUNK = 128
MOBA_BLOCK = 256
MOBA_TOPK = 3
ROPE_THETA = 10000.0
NORM_EPS = 1e-6
NEG = -1e30
N_EXPERTS = 8
MOE_TOPK = 2

LANES = 128
ATTN_BLOCK = 256
VMEM_LIMIT = 56 * 1024 * 1024
VMEM_LIMIT_MOE = 60 * 1024 * 1024

COL_CQ = 0
COL_CKV = 512
COL_KPE = 768
COL_RET = 896
COL_SB = COL_RET + 4 * GROUP_WIDTH
COL_MOBA = COL_SB + 3 * GROUP_WIDTH
PROJ_WIDTH = 6144


def _cparams(sem, vmem=VMEM_LIMIT):
    return pltpu.CompilerParams(dimension_semantics=sem, vmem_limit_bytes=vmem)


def _dot(a, b):
    return jnp.dot(a, b, preferred_element_type=F32)


def _dot_nt(a, b):
    return lax.dot_general(a, b, (((1,), (1,)), ((), ())), preferred_element_type=F32)


def _sigmoid(x):
    return 1.0 / (1.0 + jnp.exp(-x))


def _softplus(z):
    return jnp.maximum(z, 0.0) + jnp.log(1.0 + jnp.exp(-jnp.abs(z)))


def _rms(xf, width=None):
    width = xf.shape[-1] if width is None else width
    ss = jnp.sum(xf * xf, axis=-1, keepdims=True) * (1.0 / width)
    return xf * lax.rsqrt(ss + NORM_EPS)


def _ada_kernel(c_ref, w_ref, b_ref, o_ref):
    c = c_ref[...]
    cond = c * _sigmoid(c)
    o_ref[...] = _dot(cond.astype(BF16), w_ref[...].astype(BF16)) + b_ref[...]


def ada_modulation(c, ada_w, ada_b, tn=1024):
    depth, d, n = ada_w.shape
    b = c.shape[0]
    return pl.pallas_call(
        _ada_kernel,
        grid=(depth, n // tn),
        in_specs=[
            pl.BlockSpec((b, d), lambda l, j: (0, 0)),
            pl.BlockSpec((None, d, tn), lambda l, j: (l, 0, j)),
            pl.BlockSpec((None, 1, tn), lambda l, j: (l, 0, j)),
        ],
        out_specs=pl.BlockSpec((None, b, tn), lambda l, j: (l, 0, j)),
        out_shape=jax.ShapeDtypeStruct((depth, b, n), F32),
        compiler_params=_cparams(("parallel", "parallel")),
        name="ada_modulation",
    )(c, ada_w, ada_b.reshape(depth, 1, n))


def _norm_mod_rows(x_ref, g_ref, sh_ref, sc_ref, dst_ref, rows=32):
    tm = x_ref.shape[0]
    mul = g_ref[...] * (1.0 + sc_ref[...])
    sh = sh_ref[...]

    def body(r, _):
        sl = pl.ds(pl.multiple_of(r * rows, rows), rows)
        dst_ref[sl, :] = (_rms(x_ref[sl, :]) * mul + sh).astype(dst_ref.dtype)
        return 0

    lax.fori_loop(0, tm // rows, body, 0, unroll=2)


def _proj_kernel(x_ref, g_ref, sh_ref, sc_ref, w_ref, o_ref, h_ref):
    @pl.when(pl.program_id(1) == 0)
    def _():
        _norm_mod_rows(x_ref, g_ref, sh_ref, sc_ref, h_ref)

    o_ref[...] = _dot(h_ref[...], w_ref[...]).astype(o_ref.dtype)


def norm_mod_proj(x2, g, shift, scale, w, seq, tm=1024, tn=2048):
    t, d = x2.shape
    n = w.shape[1]
    tm = min(tm, seq)
    per_b = seq // tm
    bsz = shift.shape[0]
    return pl.pallas_call(
        _proj_kernel,
        grid=(t // tm, n // tn),
        in_specs=[
            pl.BlockSpec((tm, d), lambda i, j: (i, 0)),
            pl.BlockSpec((1, d), lambda i, j: (0, 0)),
            pl.BlockSpec((None, 1, d), lambda i, j: (i // per_b, 0, 0)),
            pl.BlockSpec((None, 1, d), lambda i, j: (i // per_b, 0, 0)),
            pl.BlockSpec((d, tn), lambda i, j: (0, j)),
        ],
        out_specs=pl.BlockSpec((tm, tn), lambda i, j: (i, j)),
        out_shape=jax.ShapeDtypeStruct((t, n), BF16),
        scratch_shapes=[pltpu.VMEM((tm, d), BF16)],
        compiler_params=_cparams(("parallel", "arbitrary")),
        name="norm_mod_proj",
    )(x2, g.reshape(1, d), shift.reshape(bsz, 1, d), scale.reshape(bsz, 1, d), w)


def _rope_tables(positions, dim):
    inv_freq = ROPE_THETA ** (-jnp.arange(0, dim, 2, dtype=F32) / dim)
    ang = positions.astype(F32)[:, None] * inv_freq[None, :]
    cos, sin = jnp.cos(ang), jnp.sin(ang)
    pad = jnp.zeros((positions.shape[0], LANES - dim), F32)
    return (jnp.concatenate([cos, cos, pad], axis=-1),
            jnp.concatenate([-sin, sin, pad], axis=-1))


def _rope_full(z, cos2, sin_s):
    return z * cos2 + pltpu.roll(z, 64, 1) * sin_s


def _rope_64(z, cos2, sin_s):
    lane = lax.broadcasted_iota(jnp.int32, z.shape, 1)
    partner = jnp.where(lane < 32, pltpu.roll(z, 96, 1), pltpu.roll(z, 32, 1))
    return z * cos2 + partner * sin_s


def _mla_prep_kernel(p_ref, qg_ref, kvg_ref, wuq_ref, wukv_ref, qhg_ref, khg_ref,
                     cos_ref, sin_ref, q_ref, k_ref, v_ref):
    p = p_ref[...].astype(F32)
    c_q = p[:, COL_CQ:COL_CQ + MLA_Q_RANK]
    c_kv = p[:, COL_CKV:COL_CKV + MLA_KV_RANK]
    k_pe = p[:, COL_KPE:COL_KPE + LANES]
    q = _dot((_rms(c_q) * qg_ref[...]).astype(BF16), wuq_ref[...])
    kv = _dot((_rms(c_kv) * kvg_ref[...]).astype(BF16), wukv_ref[...])
    cos2, sin_s = cos_ref[...], sin_ref[...]
    qhg, khg = qhg_ref[...], khg_ref[...]
    scale = MLA_QK_DIM ** -0.5
    pe_ss = jnp.sum(k_pe * k_pe, axis=-1, keepdims=True)
    for h in range(GROUP_HEADS):
        qh = q[:, h * MLA_QK_PAD:(h + 1) * MLA_QK_PAD]
        qh = _rms(qh, MLA_QK_DIM) * qhg * scale
        q_ref[h, :, :LANES] = qh[:, :LANES].astype(q_ref.dtype)
        q_ref[h, :, LANES:] = _rope_64(qh[:, LANES:], cos2, sin_s).astype(q_ref.dtype)
        k_nope = kv[:, 2 * h * LANES:(2 * h + 1) * LANES]
        ss = (jnp.sum(k_nope * k_nope, axis=-1, keepdims=True) + pe_ss) * (1.0 / MLA_QK_DIM)
        r = lax.rsqrt(ss + NORM_EPS)
        k_ref[h, :, :LANES] = (k_nope * r * khg[:, :LANES]).astype(k_ref.dtype)
        k_ref[h, :, LANES:] = _rope_64(k_pe * r * khg[:, LANES:], cos2, sin_s).astype(k_ref.dtype)
        v_ref[h] = kv[:, (2 * h + 1) * LANES:(2 * h + 2) * LANES].astype(v_ref.dtype)


def mla_prep(proj3, q_norm_g, kv_norm_g, w_uq_pad, w_ukv, q_head_g_pad, k_head_g_pad,
             cos_pe, sin_pe, tm=512):
    bsz, seq, _ = proj3.shape
    h = GROUP_HEADS
    n_in = COL_KPE + LANES
    const = lambda shape: pl.BlockSpec(shape, lambda b, i: (0,) * len(shape))
    return pl.pallas_call(
        _mla_prep_kernel,
        grid=(bsz, seq // tm),
        in_specs=[
            pl.BlockSpec((None, tm, n_in), lambda b, i: (b, i, 0)),
            const((1, MLA_Q_RANK)), const((1, MLA_KV_RANK)),
            const((MLA_Q_RANK, h * MLA_QK_PAD)), const((MLA_KV_RANK, h * 2 * LANES)),
            const((1, MLA_QK_PAD)), const((1, MLA_QK_PAD)),
            pl.BlockSpec((tm, LANES), lambda b, i: (i, 0)),
            pl.BlockSpec((tm, LANES), lambda b, i: (i, 0)),
        ],
        out_specs=[
            pl.BlockSpec((None, h, tm, MLA_QK_PAD), lambda b, i: (b, 0, i, 0)),
            pl.BlockSpec((None, h, tm, MLA_QK_PAD), lambda b, i: (b, 0, i, 0)),
            pl.BlockSpec((None, h, tm, LANES), lambda b, i: (b, 0, i, 0)),
        ],
        out_shape=[
            jax.ShapeDtypeStruct((bsz, h, seq, MLA_QK_PAD), BF16),
            jax.ShapeDtypeStruct((bsz, h, seq, MLA_QK_PAD), BF16),
            jax.ShapeDtypeStruct((bsz, h, seq, LANES), BF16),
        ],
        compiler_params=_cparams(("parallel", "parallel")),
        name="mla_prep",
    )(proj3, q_norm_g.reshape(1, -1), kv_norm_g.reshape(1, -1), w_uq_pad, w_ukv,
      q_head_g_pad, k_head_g_pad, cos_pe, sin_pe)


def _moba_gates(q_ref, kmean_ref):
    km = kmean_ref[...]
    nb = km.shape[0]
    hi = km.astype(BF16).astype(F32)
    mid = (km - hi).astype(BF16).astype(F32)
    lo = (km - hi - mid).astype(BF16).astype(F32)
    pieces = jnp.concatenate([hi, mid, lo, jnp.zeros_like(km)], axis=0).astype(BF16)
    r = _dot_nt(pieces, q_ref[...])
    return r[:nb] + r[nb:2 * nb] + r[2 * nb:3 * nb]


def _moba_block_choice(gate, n_past):
    blk = lax.broadcasted_iota(jnp.int32, gate.shape, 0)
    rank = jnp.zeros(gate.shape, jnp.int32)
    for o in range(n_past):
        g_o = gate[o:o + 1, :]
        tie = jnp.where(o < blk, 1, 0)
        rank = rank + jnp.where(g_o > gate, 1, jnp.where(g_o == gate, tie, 0))
    return jnp.where(rank < MOBA_TOPK, 1.0, 0.0)


def _attend(q_ref, k_ref, v_ref, kmean_ref, o_ref):
    moba = kmean_ref is not None
    if moba:
        gates = _moba_gates(q_ref, kmean_ref)
    seq = q_ref.shape[0]
    t = ATTN_BLOCK
    v_t = jnp.transpose(v_ref[...].astype(F32)).astype(BF16)
    key = lax.broadcasted_iota(jnp.int32, (t, t), 0)
    qry = lax.broadcasted_iota(jnp.int32, (t, t), 1)
    causal = key <= qry
    for i in range(seq // t):
        own = slice(i * t, (i + 1) * t)
        q = q_ref[own, :]
        s_own = jnp.where(causal, _dot_nt(k_ref[own, :], q), NEG)
        m = jnp.max(s_own, axis=0, keepdims=True)
        s_past = []
        if i > 0:
            s_all = _dot_nt(k_ref[:i * t, :], q)
            if moba and i > MOBA_TOPK:
                sel = _moba_block_choice(gates[:, own], i)
                s_past = [jnp.where(sel[n:n + 1, :] > 0.5, s_all[n * t:(n + 1) * t, :], NEG)
                          for n in range(i)]
            else:
                s_past = [s_all]
            for s in s_past:
                m = jnp.maximum(m, jnp.max(s, axis=0, keepdims=True))
        p = jnp.exp(s_own - m)
        l = jnp.sum(p, axis=0, keepdims=True)
        acc_t = _dot(v_t[:, own], p.astype(BF16))
        start = 0
        for s in s_past:
            p = jnp.exp(s - m)
            l = l + jnp.sum(p, axis=0, keepdims=True)
            acc_t = acc_t + _dot(v_t[:, start:start + s.shape[0]], p.astype(BF16))
            start += s.shape[0]
        o_ref[own, :] = jnp.transpose(acc_t / l).astype(o_ref.dtype)


def _head_spec(seq, d):
    return pl.BlockSpec((None, None, seq, d), lambda b, hh: (b, hh, 0, 0))


def _proj_col_spec(seq, col):
    cb = col // LANES
    return pl.BlockSpec((None, seq, LANES), lambda b, hh: (b, 0, cb + hh))


def _mla_attn_kernel(q_ref, k_ref, v_ref, o_ref):
    _attend(q_ref, k_ref, v_ref, None, o_ref)


def mla_attention(q, k, v):
    bsz, h, seq, dk = q.shape
    return pl.pallas_call(
        _mla_attn_kernel,
        grid=(bsz, h),
        in_specs=[_head_spec(seq, dk), _head_spec(seq, dk), _head_spec(seq, LANES)],
        out_specs=pl.BlockSpec((None, seq, LANES), lambda b, hh: (b, 0, hh)),
        out_shape=jax.ShapeDtypeStruct((bsz, seq, h * LANES), BF16),
        compiler_params=_cparams(("parallel", "parallel")),
        name="mla_attention",
    )(q, k, v)


def _moba_prep(q_ref, k_ref, qg_ref, kg_ref, cos_ref, sin_ref, qo_ref, ko_ref, km_ref):
    seq = q_ref.shape[0]
    scale = HEAD_DIM ** -0.5
    for blk in range(seq // MOBA_BLOCK):
        sl = slice(blk * MOBA_BLOCK, (blk + 1) * MOBA_BLOCK)
        cos2, sin_s = cos_ref[sl, :], sin_ref[sl, :]
        qn = _rms(q_ref[sl, :].astype(F32)) * qg_ref[...]
        qo_ref[sl, :] = (_rope_full(qn, cos2, sin_s) * scale).astype(qo_ref.dtype)
        kn = _rope_full(_rms(k_ref[sl, :].astype(F32)) * kg_ref[...], cos2, sin_s)
        ko_ref[sl, :] = kn.astype(ko_ref.dtype)
        km_ref[blk:blk + 1, :] = jnp.mean(kn, axis=0, keepdims=True)


def _moba_attn_kernel(q_ref, k_ref, v_ref, qg_ref, kg_ref, cos_ref, sin_ref, o_ref,
                      qn_ref, kn_ref, km_ref):
    _moba_prep(q_ref, k_ref, qg_ref, kg_ref, cos_ref, sin_ref, qn_ref, kn_ref, km_ref)
    _attend(qn_ref, kn_ref, v_ref, km_ref, o_ref)


def moba_attention(proj3, q_head_g, k_head_g, cos_full, sin_full):
    bsz, seq, _ = proj3.shape
    h = GROUP_HEADS
    const = lambda shape: pl.BlockSpec(shape, lambda b, hh: (0,) * len(shape))
    return pl.pallas_call(
        _moba_attn_kernel,
        grid=(bsz, h),
        in_specs=[
            _proj_col_spec(seq, COL_MOBA),
            _proj_col_spec(seq, COL_MOBA + GROUP_WIDTH),
            _proj_col_spec(seq, COL_MOBA + 2 * GROUP_WIDTH),
            const((1, LANES)), const((1, LANES)),
            const((seq, LANES)), const((seq, LANES)),
        ],
        out_specs=pl.BlockSpec((None, seq, LANES), lambda b, hh: (b, 0, hh)),
        out_shape=jax.ShapeDtypeStruct((bsz, seq, h * LANES), BF16),
        scratch_shapes=[pltpu.VMEM((seq, LANES), BF16), pltpu.VMEM((seq, LANES), BF16),
                        pltpu.VMEM((seq // MOBA_BLOCK, LANES), F32)],
        compiler_params=_cparams(("parallel", "parallel")),
        name="moba_attention",
    )(proj3, proj3, proj3, q_head_g.reshape(1, -1), k_head_g.reshape(1, -1), cos_full, sin_full)


SB_DEAD_TAIL = -104.0


def _sb_tile(q, k_blk, v_blk, tail, later, strict):
    z = _dot_nt(q, k_blk) * (HEAD_DIM ** -0.5)
    log_1m = -_softplus(z)
    if strict is not None:
        log_1m = jnp.where(strict, log_1m, 0.0)
    hi = log_1m.astype(BF16)
    lo = (log_1m - hi.astype(F32)).astype(BF16)
    suffix = _dot(hi, later) + _dot(lo, later)
    a = jnp.exp(z + log_1m + suffix + tail)
    if strict is not None:
        a = jnp.where(strict, a, 0.0)
    return _dot(a.astype(v_blk.dtype), v_blk), jnp.sum(log_1m, axis=-1, keepdims=True)


def _sb_kernel(q_ref, k_ref, v_ref, o_ref, acc_ref, tail_ref):
    seq = q_ref.shape[0]
    t = ATTN_BLOCK
    row = lax.broadcasted_iota(jnp.int32, (t, t), 0)
    col = lax.broadcasted_iota(jnp.int32, (t, t), 1)
    strict = col < row
    later = jnp.where(row > col, 1.0, 0.0).astype(BF16)
    blk = lambda n: slice(n * t, (n + 1) * t)

    for i in range(seq // t):
        q = q_ref[blk(i), :]
        acc, tail = _sb_tile(q, k_ref[blk(i), :], v_ref[blk(i), :], 0.0, later, strict)
        if i > 0:
            y, s = _sb_tile(q, k_ref[blk(i - 1), :], v_ref[blk(i - 1), :], tail, later, None)
            acc, tail = acc + y, tail + s
        acc_ref[blk(i), :] = acc
        tail_ref[blk(i), :] = tail

    for i in range(2, seq // t):
        alive = jnp.max(tail_ref[blk(i), :], axis=0, keepdims=True)[0, 0] > SB_DEAD_TAIL

        @pl.when(alive)
        def _():
            q = q_ref[blk(i), :]
            acc, tail = acc_ref[blk(i), :], tail_ref[blk(i), :]
            for n in range(i - 2, -1, -1):
                y, s = _sb_tile(q, k_ref[blk(n), :], v_ref[blk(n), :], tail, later, None)
                acc, tail = acc + y, tail + s
            acc_ref[blk(i), :] = acc

    o_ref[...] = acc_ref[...].astype(o_ref.dtype)


def sb_attention(proj3):
    bsz, seq, _ = proj3.shape
    return pl.pallas_call(
        _sb_kernel,
        grid=(bsz, GROUP_HEADS),
        in_specs=[
            _proj_col_spec(seq, COL_SB),
            _proj_col_spec(seq, COL_SB + GROUP_WIDTH),
            _proj_col_spec(seq, COL_SB + 2 * GROUP_WIDTH),
        ],
        out_specs=pl.BlockSpec((None, seq, LANES), lambda b, hh: (b, 0, hh)),
        out_shape=jax.ShapeDtypeStruct((bsz, seq, GROUP_WIDTH), BF16),
        scratch_shapes=[pltpu.VMEM((seq, LANES), F32), pltpu.VMEM((seq, 1), F32)],
        compiler_params=_cparams(("parallel", "parallel")),
        name="sb_attention",
    )(proj3, proj3, proj3)


def _ret_kernel(q_ref, k_ref, v_ref, g_ref, cos_ref, sin_ref, lg_ref, o_ref):
    seq = q_ref.shape[0]
    c = RET_CHUNK
    log_gamma = lg_ref[...]
    ri = lax.broadcasted_iota(jnp.int32, (c, c), 0).astype(F32)
    ci = lax.broadcasted_iota(jnp.int32, (c, c), 1).astype(F32)
    rel = ri - ci
    intra_decay = jnp.where(rel >= 0, jnp.exp(jnp.maximum(rel, 0.0) * log_gamma), 0.0)
    idx = lax.broadcasted_iota(jnp.int32, (c, 1), 0).astype(F32)
    query_decay = jnp.exp((idx + 1.0) * log_gamma)
    key_decay = jnp.exp((c - 1.0 - idx) * log_gamma)
    chunk_decay = jnp.exp(c * log_gamma)
    k_scale = HEAD_DIM ** -0.5

    state = jnp.zeros((HEAD_DIM, HEAD_DIM), F32)
    for n in range(seq // c):
        sl = slice(n * c, (n + 1) * c)
        cos2, sin_s = cos_ref[sl, :], sin_ref[sl, :]
        q = _rope_full(q_ref[sl, :].astype(F32), cos2, sin_s)
        k = _rope_full(k_ref[sl, :].astype(F32), cos2, sin_s) * k_scale
        vb = v_ref[sl, :]
        qb = q.astype(BF16)
        scores = _dot_nt(qb, k.astype(BF16)) * intra_decay
        y = _dot(scores.astype(BF16), vb)
        y = y + _dot(qb, state.astype(BF16)) * query_decay
        kd_t = jnp.transpose(k * key_decay).astype(BF16)
        state = state * chunk_decay + _dot(kd_t, vb)
        mu = jnp.mean(y, axis=-1, keepdims=True)
        yc = y - mu
        var = jnp.mean(yc * yc, axis=-1, keepdims=True)
        yn = yc * lax.rsqrt(var + NORM_EPS)
        g = g_ref[sl, :].astype(F32)
        o_ref[sl, :] = (g * _sigmoid(g) * yn).astype(o_ref.dtype)


def retention(proj3, cos_full, sin_full):
    bsz, seq, _ = proj3.shape
    h = GROUP_HEADS
    log_gamma = jnp.log(1.0 - 2.0 ** (-5.0 - jnp.arange(h, dtype=F32))).reshape(h, 1, 1)
    col = lambda k: pl.BlockSpec((None, seq, LANES),
                                 lambda b, hh: (b, 0, (COL_RET + k * GROUP_WIDTH) // LANES + hh))
    const = lambda shape: pl.BlockSpec(shape, lambda b, hh: (0,) * len(shape))
    return pl.pallas_call(
        _ret_kernel,
        grid=(bsz, h),
        in_specs=[col(0), col(1), col(2), col(3), const((seq, LANES)), const((seq, LANES)),
                  pl.BlockSpec((None, 1, 1), lambda b, hh: (hh, 0, 0))],
        out_specs=pl.BlockSpec((None, seq, LANES), lambda b, hh: (b, 0, hh)),
        out_shape=jax.ShapeDtypeStruct((bsz, seq, GROUP_WIDTH), BF16),
        compiler_params=_cparams(("parallel", "parallel")),
        name="retention",
    )(proj3, proj3, proj3, proj3, cos_full, sin_full, log_gamma)


def _out_proj_kernel(y0_ref, y1_ref, y2_ref, y3_ref, gg_ref, w_ref, x_ref, gate_ref, o_ref, h_ref):
    @pl.when(pl.program_id(1) == 0)
    def _():
        for grp, y_ref in enumerate((y0_ref, y1_ref, y2_ref, y3_ref)):
            yn = _rms(y_ref[...].astype(F32)) * gg_ref[grp:grp + 1, :]
            h_ref[:, grp * GROUP_WIDTH:(grp + 1) * GROUP_WIDTH] = yn.astype(h_ref.dtype)

    o_ref[...] = x_ref[...] + gate_ref[...] * _dot(h_ref[...], w_ref[...])


def out_proj_residual(ys, group_g, w_out, x2, gate, seq, tm=1024, tn=1024):
    t, d = x2.shape
    tm = min(tm, seq)
    per_b = seq // tm
    bsz = gate.shape[0]
    y_spec = pl.BlockSpec((tm, GROUP_WIDTH), lambda i, j: (i, 0))
    return pl.pallas_call(
        _out_proj_kernel,
        grid=(t // tm, d // tn),
        in_specs=[y_spec, y_spec, y_spec, y_spec,
                  pl.BlockSpec((4, GROUP_WIDTH), lambda i, j: (0, 0)),
                  pl.BlockSpec((4 * GROUP_WIDTH, tn), lambda i, j: (0, j)),
                  pl.BlockSpec((tm, tn), lambda i, j: (i, j)),
                  pl.BlockSpec((None, 1, tn), lambda i, j: (i // per_b, 0, j))],
        out_specs=pl.BlockSpec((tm, tn), lambda i, j: (i, j)),
        out_shape=jax.ShapeDtypeStruct((t, d), F32),
        scratch_shapes=[pltpu.VMEM((tm, 4 * GROUP_WIDTH), BF16)],
        compiler_params=_cparams(("parallel", "arbitrary")),
        name="out_proj_residual",
    )(*[y.reshape(t, GROUP_WIDTH) for y in ys], group_g, w_out, x2, gate.reshape(bsz, 1, d))


def _swiglu_act(h, w1_ref, w3_ref):
    a = _dot(h, w1_ref[...])
    return (a * _sigmoid(a) * _dot(h, w3_ref[...])).astype(BF16)


FFN_UP_STEPS = 11
FFN_DOWN_STEPS = 4


def _down_proj(a_ref, w2_ref):
    n, _, tf = a_ref.shape
    acc = _dot(a_ref[0], w2_ref[:tf, :])
    for c in range(1, n):
        acc = acc + _dot(a_ref[c], w2_ref[c * tf:(c + 1) * tf, :])
    return acc


def _ffn_kernel(x_ref, g_ref, sh_ref, sc_ref, w1_ref, w3_ref, w2_ref, xc_ref, gate_ref, o_ref,
                h_ref, a_ref):
    s = pl.program_id(1)

    @pl.when(s == 0)
    def _():
        _norm_mod_rows(x_ref, g_ref, sh_ref, sc_ref, h_ref)

    @pl.when(s < FFN_UP_STEPS)
    def _():
        a_ref[s] = _swiglu_act(h_ref[...], w1_ref, w3_ref)

    @pl.when(s >= FFN_UP_STEPS)
    def _():
        o_ref[...] = xc_ref[...] + gate_ref[...] * _down_proj(a_ref, w2_ref)


def _ffn_chunks(ffn, d):
    return ffn // FFN_UP_STEPS, d // FFN_DOWN_STEPS


def _up_idx(s):
    return jnp.minimum(s, FFN_UP_STEPS - 1)


def _down_idx(s):
    return jnp.maximum(s - FFN_UP_STEPS, 0)


def dense_ffn_residual(x2, g, shift, scale, gate, w1, w3, w2, seq, tm=1024):
    t, d = x2.shape
    ffn = w1.shape[1]
    tf, tn = _ffn_chunks(ffn, d)
    w1, w3, w2 = w1.astype(BF16), w3.astype(BF16), w2.astype(BF16)
    tm = min(tm, seq)
    per_b = seq // tm
    bsz = gate.shape[0]
    row = lambda: pl.BlockSpec((None, 1, d), lambda i, s: (i // per_b, 0, 0))
    return pl.pallas_call(
        _ffn_kernel,
        grid=(t // tm, FFN_UP_STEPS + FFN_DOWN_STEPS),
        in_specs=[pl.BlockSpec((tm, d), lambda i, s: (i, 0), pipeline_mode=pl.Buffered(1)),
                  pl.BlockSpec((1, d), lambda i, s: (0, 0)),
                  row(), row(),
                  pl.BlockSpec((d, tf), lambda i, s: (0, _up_idx(s))),
                  pl.BlockSpec((d, tf), lambda i, s: (0, _up_idx(s))),
                  pl.BlockSpec((ffn, tn), lambda i, s: (0, _down_idx(s))),
                  pl.BlockSpec((tm, tn), lambda i, s: (i, _down_idx(s))),
                  pl.BlockSpec((None, 1, tn), lambda i, s: (i // per_b, 0, _down_idx(s)))],
        out_specs=pl.BlockSpec((tm, tn), lambda i, s: (i, _down_idx(s))),
        out_shape=jax.ShapeDtypeStruct((t, d), F32),
        scratch_shapes=[pltpu.VMEM((tm, d), BF16), pltpu.VMEM((FFN_UP_STEPS, tm, tf), BF16)],
        compiler_params=_cparams(("parallel", "arbitrary")),
        name="dense_ffn_residual",
    )(x2, g.reshape(1, d), shift.reshape(bsz, 1, d), scale.reshape(bsz, 1, d),
      w1, w3, w2, x2, gate.reshape(bsz, 1, d))


def _router_kernel(x_ref, g_ref, sh_ref, sc_ref, rw_ref, h_ref, idx_ref, gates_ref):
    _norm_mod_rows(x_ref, g_ref, sh_ref, sc_ref, h_ref)
    logits = jnp.dot(h_ref[...], rw_ref[...], precision=lax.Precision.HIGHEST,
                     preferred_element_type=F32)
    lane = lax.broadcasted_iota(jnp.int32, logits.shape, 1)
    lane_f = lane.astype(F32)
    logits = jnp.where(lane < N_EXPERTS, logits, -jnp.inf)
    m0 = jnp.max(logits, axis=-1, keepdims=True)
    e0 = jnp.min(jnp.where(logits == m0, lane_f, float(LANES)), axis=-1, keepdims=True)
    rest = jnp.where(lane_f == e0, -jnp.inf, logits)
    m1 = jnp.max(rest, axis=-1, keepdims=True)
    e1 = jnp.min(jnp.where(rest == m1, lane_f, float(LANES)), axis=-1, keepdims=True)
    p1 = jnp.exp(m1 - m0)
    g0 = 1.0 / (1.0 + p1)
    idx_ref[...] = jnp.where(lane == 0, e0, jnp.where(lane == 1, e1, 0.0)).astype(jnp.int32)
    gates_ref[...] = jnp.where(lane == 0, g0, jnp.where(lane == 1, p1 * g0, 0.0))


def moe_router(x2, g, shift, scale, router_w, seq, tm=256):
    t, d = x2.shape
    tm = min(tm, seq)
    per_b = seq // tm
    bsz = shift.shape[0]
    rw = jnp.zeros((d, LANES), F32).at[:, :N_EXPERTS].set(router_w)
    row = lambda: pl.BlockSpec((None, 1, d), lambda i: (i // per_b, 0, 0))
    return pl.pallas_call(
        _router_kernel,
        grid=(t // tm,),
        in_specs=[pl.BlockSpec((tm, d), lambda i: (i, 0)),
                  pl.BlockSpec((1, d), lambda i: (0, 0)),
                  row(), row(),
                  pl.BlockSpec((d, LANES), lambda i: (0, 0))],
        out_specs=[pl.BlockSpec((tm, d), lambda i: (i, 0)),
                   pl.BlockSpec((tm, LANES), lambda i: (i, 0)),
                   pl.BlockSpec((tm, LANES), lambda i: (i, 0))],
        out_shape=[jax.ShapeDtypeStruct((t, d), F32),
                   jax.ShapeDtypeStruct((t, LANES), jnp.int32),
                   jax.ShapeDtypeStruct((t, LANES), F32)],
        compiler_params=_cparams(("parallel",)),
        name="moe_router",
    )(x2, g.reshape(1, d), shift.reshape(bsz, 1, d), scale.reshape(bsz, 1, d), rw)


GATHER_UNROLL = 8
MOE_TILE_ROWS = 1056


def _row_copy(src_ref, dst_ref, sem, src_row, dst_row):
    return pltpu.make_async_copy(src_ref.at[pl.ds(src_row, 1), :],
                                 dst_ref.at[pl.ds(dst_row, 1), :], sem)


def _start_row_gather(src_ref, dst_ref, sem, row_of, n_rows):
    def start(r, _):
        _row_copy(src_ref, dst_ref, sem, row_of(r), r).start()
        return 0

    lax.fori_loop(0, n_rows, start, 0, unroll=GATHER_UNROLL)


def _wait_row_gather(dst_ref, sem):
    pltpu.make_async_copy(dst_ref, dst_ref, sem).wait()


def _moe_ffn_kernel(tile_e_ref, tile_ok_ref, buf_t_ref, h_hbm, w1_ref, w3_ref, w2_ref, o_ref,
                    hf_ref, hb_ref, a_ref, sem):
    i = pl.program_id(0)
    f = pl.program_id(1)
    n_tiles = pl.num_programs(0)
    nf = FFN_UP_STEPS
    tm = hb_ref.shape[0]
    rows_per_step = tm // nf
    ok = tile_ok_ref[i] > 0
    last = (i == n_tiles - 1) & (f == nf - 1)
    nxt = jnp.minimum(i + 1, n_tiles - 1)
    started = (i == 0) | (tile_ok_ref[jnp.maximum(i - 1, 0)] > 0)

    @pl.when(f == 0)
    def _():
        @pl.when(i == 0)
        def _():
            _start_row_gather(h_hbm, hf_ref, sem, lambda r: buf_t_ref[r], tm)

        @pl.when(started)
        def _():
            _wait_row_gather(hf_ref, sem)

        @pl.when(ok)
        def _():
            hb_ref[...] = hf_ref[...].astype(hb_ref.dtype)

    @pl.when(ok & (f < nf))
    def _():
        base = f * rows_per_step
        for r in range(rows_per_step):
            _row_copy(h_hbm, hf_ref, sem, buf_t_ref[nxt * tm + base + r], base + r).start()
        a_ref[f] = _swiglu_act(hb_ref[...], w1_ref, w3_ref)

    @pl.when(ok & last)
    def _():
        _wait_row_gather(hf_ref, sem)

    @pl.when(f >= nf)
    def _():
        @pl.when(ok)
        def _():
            o_ref[...] = _down_proj(a_ref, w2_ref)

        @pl.when(jnp.logical_not(ok))
        def _():
            o_ref[...] = jnp.zeros_like(o_ref)


def moe_expert_ffn(h2, tile_e, tile_ok, buf_t, w1, w3, w2, tm):
    t, d = h2.shape
    ffn = w1.shape[2]
    tf, tn = _ffn_chunks(ffn, d)
    w1, w3, w2 = w1.astype(BF16), w3.astype(BF16), w2.astype(BF16)
    n_tiles = tile_e.shape[0]

    def up(i, s, ok):
        return jnp.where(ok[i] > 0, _up_idx(s), FFN_UP_STEPS - 1)

    def down(i, s, ok):
        return jnp.where(ok[i] > 0, _down_idx(s), 0)

    grid_spec = pltpu.PrefetchScalarGridSpec(
        num_scalar_prefetch=3,
        grid=(n_tiles, FFN_UP_STEPS + FFN_DOWN_STEPS),
        in_specs=[pl.BlockSpec(memory_space=pl.ANY),
                  pl.BlockSpec((None, d, tf), lambda i, s, te, ok, bt: (te[i], 0, up(i, s, ok))),
                  pl.BlockSpec((None, d, tf), lambda i, s, te, ok, bt: (te[i], 0, up(i, s, ok))),
                  pl.BlockSpec((None, ffn, tn), lambda i, s, te, ok, bt: (te[i], 0, down(i, s, ok)))],
        out_specs=pl.BlockSpec((tm, tn), lambda i, s, te, ok, bt: (i, _down_idx(s))),
        scratch_shapes=[pltpu.VMEM((tm, d), F32), pltpu.VMEM((tm, d), BF16),
                        pltpu.VMEM((FFN_UP_STEPS, tm, tf), BF16), pltpu.SemaphoreType.DMA(())],
    )
    return pl.pallas_call(
        _moe_ffn_kernel,
        grid_spec=grid_spec,
        out_shape=jax.ShapeDtypeStruct((n_tiles * tm, d), F32),
        compiler_params=_cparams(("arbitrary", "arbitrary"), VMEM_LIMIT_MOE),
        name="moe_expert_ffn",
    )(tile_e, tile_ok, buf_t, h2, w1, w3, w2)


def _moe_combine_kernel(d0_ref, d1_ref, y_hbm, x_ref, gate_ref, gates_ref, o_ref,
                        y0_ref, y1_ref, sems0, sems1):
    i = pl.program_id(0)
    tm = x_ref.shape[0]
    slot = i % 2

    def start_gather(tile, into):
        def start(r, _):
            _row_copy(y_hbm, y0_ref.at[into], sems0.at[into], d0_ref[tile * tm + r], r).start()
            _row_copy(y_hbm, y1_ref.at[into], sems1.at[into], d1_ref[tile * tm + r], r).start()
            return 0

        lax.fori_loop(0, tm, start, 0, unroll=GATHER_UNROLL)

    @pl.when(i == 0)
    def _():
        start_gather(0, 0)

    @pl.when(i + 1 < pl.num_programs(0))
    def _():
        start_gather(i + 1, 1 - slot)

    _wait_row_gather(y0_ref.at[slot], sems0.at[slot])
    _wait_row_gather(y1_ref.at[slot], sems1.at[slot])
    gates = gates_ref[...]
    y = gates[:, 0:1] * y0_ref[slot] + gates[:, 1:2] * y1_ref[slot]
    o_ref[...] = x_ref[...] + gate_ref[...] * y


def moe_combine_residual(dest0, dest1, y_buf, x2, gate, gates, seq, tm=256):
    t, d = x2.shape
    tm = min(tm, seq)
    per_b = seq // tm
    bsz = gate.shape[0]
    grid_spec = pltpu.PrefetchScalarGridSpec(
        num_scalar_prefetch=2,
        grid=(t // tm,),
        in_specs=[pl.BlockSpec(memory_space=pl.ANY),
                  pl.BlockSpec((tm, d), lambda i, d0, d1: (i, 0)),
                  pl.BlockSpec((None, 1, d), lambda i, d0, d1: (i // per_b, 0, 0)),
                  pl.BlockSpec((tm, LANES), lambda i, d0, d1: (i, 0))],
        out_specs=pl.BlockSpec((tm, d), lambda i, d0, d1: (i, 0)),
        scratch_shapes=[pltpu.VMEM((2, tm, d), F32), pltpu.VMEM((2, tm, d), F32),
                        pltpu.SemaphoreType.DMA((2,)), pltpu.SemaphoreType.DMA((2,))],
    )
    return pl.pallas_call(
        _moe_combine_kernel,
        grid_spec=grid_spec,
        out_shape=jax.ShapeDtypeStruct((t, d), F32),
        compiler_params=_cparams(("arbitrary",)),
        name="moe_combine_residual",
    )(dest0, dest1, y_buf, x2, gate.reshape(bsz, 1, d), gates)


def _moe_routing_tables(idx, tm):
    t = idx.shape[0]
    n_slots = t * MOE_TOPK
    n_tiles = n_slots // tm + N_EXPERTS
    flat_e = idx[:, :MOE_TOPK].reshape(-1)
    onehot = (flat_e[:, None] == jnp.arange(N_EXPERTS)[None, :]).astype(jnp.int32)
    counts = jnp.sum(onehot, axis=0)
    rank = jnp.sum((jnp.cumsum(onehot, axis=0) - onehot) * onehot, axis=1)
    padded = (counts + tm - 1) // tm * tm
    pad_ends = jnp.cumsum(padded)
    pad_starts = pad_ends - padded
    dest = (pad_starts[flat_e] + rank).astype(jnp.int32)
    buf_t = jnp.zeros((n_tiles * tm,), jnp.int32).at[dest].set(
        (jnp.arange(n_slots) // MOE_TOPK).astype(jnp.int32))
    tile_start = jnp.arange(n_tiles) * tm
    tile_e = jnp.minimum(jnp.searchsorted(pad_ends, tile_start, side="right"),
                         N_EXPERTS - 1).astype(jnp.int32)
    tile_ok = (tile_start < pad_ends[-1]).astype(jnp.int32)
    dest2 = dest.reshape(t, MOE_TOPK)
    return tile_e, tile_ok, buf_t, dest2[:, 0], dest2[:, 1]


def moe_ffn_residual(x2, g, shift, scale, gate, router_w, w1, w3, w2, seq, tile_rows=MOE_TILE_ROWS):
    h2, idx, gates = moe_router(x2, g, shift, scale, router_w, seq)
    tile_e, tile_ok, buf_t, dest0, dest1 = _moe_routing_tables(idx, tile_rows)
    y_buf = moe_expert_ffn(h2, tile_e, tile_ok, buf_t, w1, w3, w2, tile_rows)
    return moe_combine_residual(dest0, dest1, y_buf, x2, gate, gates, seq)


def _pad_w_in(w_in):
    d = w_in.shape[0]
    kpe_end = MLA_Q_RANK + MLA_KV_RANK + MLA_ROPE_DIM
    tail = PROJ_WIDTH - (COL_MOBA + 3 * GROUP_WIDTH)
    w = w_in.astype(BF16)
    return jnp.concatenate([
        w[:, :kpe_end], jnp.zeros((d, LANES - MLA_ROPE_DIM), BF16),
        w[:, kpe_end:], jnp.zeros((d, tail), BF16)], axis=1)


def _pad_heads_192(a):
    lead = a.shape[:-1]
    a = a.reshape(lead + (GROUP_HEADS, MLA_QK_DIM))
    a = jnp.pad(a, [(0, 0)] * len(lead) + [(0, 0), (0, MLA_QK_PAD - MLA_QK_DIM)])
    return a.reshape(lead + (GROUP_HEADS * MLA_QK_PAD,))


def kernel(x, c, positions, ada_w, ada_b, norm_mix_g, norm_ffn_g, w_in, mla_q_norm_g, mla_kv_norm_g, mla_w_uq, mla_w_ukv, mla_q_head_g, mla_k_head_g, moba_q_head_g, moba_k_head_g, group_norm_g, w_out, ffn_w1, ffn_w3, ffn_w2, router_w, moe_w1, moe_w3, moe_w2):
    bsz, seq, d = x.shape
    depth = ada_w.shape[0]
    cos_pe, sin_pe = _rope_tables(positions, MLA_ROPE_DIM)
    cos_full, sin_full = _rope_tables(positions, HEAD_DIM)
    mod = ada_modulation(c, ada_w, ada_b)
    x2 = x.reshape(bsz * seq, d)
    for l in range(depth):
        shift_m, scale_m, gate_m, shift_f, scale_f, gate_f = jnp.split(mod[l], 6, axis=-1)
        proj = norm_mod_proj(x2, norm_mix_g[l], shift_m, scale_m, _pad_w_in(w_in[l]), seq)
        proj3 = proj.reshape(bsz, seq, PROJ_WIDTH)
        pad_g = lambda g: jnp.pad(g, (0, MLA_QK_PAD - MLA_QK_DIM)).reshape(1, MLA_QK_PAD)
        q_mla, k_mla, v_mla = mla_prep(
            proj3, mla_q_norm_g[l], mla_kv_norm_g[l],
            _pad_heads_192(mla_w_uq[l]).astype(BF16), mla_w_ukv[l].astype(BF16),
            pad_g(mla_q_head_g[l]), pad_g(mla_k_head_g[l]), cos_pe, sin_pe)
        y_mla = mla_attention(q_mla, k_mla, v_mla)
        y_ret = retention(proj3, cos_full, sin_full)
        y_sb = sb_attention(proj3)
        y_moba = moba_attention(proj3, moba_q_head_g[l], moba_k_head_g[l], cos_full, sin_full)
        x2 = out_proj_residual((y_mla, y_ret, y_sb, y_moba), group_norm_g[l],
                               w_out[l].astype(BF16), x2, gate_m, seq)
        j = l // 2
        if l % 2 == 0:
            x2 = dense_ffn_residual(x2, norm_ffn_g[l], shift_f, scale_f, gate_f,
                                    ffn_w1[j], ffn_w3[j], ffn_w2[j], seq)
        else:
            x2 = moe_ffn_residual(x2, norm_ffn_g[l], shift_f, scale_f, gate_f, router_w[j],
                                  moe_w1[j], moe_w3[j], moe_w2[j], seq)
    return x2.reshape(bsz, seq, d)
```

```python
import jax
import jax.numpy as jnp
from jax import lax
from jax.experimental import pallas as pl
from jax.experimental.pallas import tpu as pltpu

F32 = jnp.float32
BF16 = jnp.bfloat16

HEAD_DIM = 128
GROUP_HEADS = 4
GROUP_WIDTH = 512
MLA_Q_RANK = 512
MLA_KV_RANK = 256
MLA_NOPE_DIM = 128
MLA_ROPE_DIM = 64
MLA_QK_DIM = MLA_NOPE_DIM + MLA_ROPE_DIM
MLA_QK_PAD = 256
RET_CHUNK = 128
MOBA_BLOCK = 256
MOBA_TOPK = 3
ROPE_THETA = 10000.0
NORM_EPS = 1e-6
NEG = -1e30
N_EXPERTS = 8
MOE_TOPK = 2

LANES = 128
ATTN_BLOCK = 256
VMEM_LIMIT = 56 * 1024 * 1024
VMEM_LIMIT_MOE = 60 * 1024 * 1024

COL_CQ = 0
COL_CKV = 512
COL_KPE = 768
COL_RET = 896
COL_SB = COL_RET + 4 * GROUP_WIDTH
COL_MOBA = COL_SB + 3 * GROUP_WIDTH
PROJ_WIDTH = 6144


def _cparams(sem, vmem=VMEM_LIMIT):
    return pltpu.CompilerParams(dimension_semantics=sem, vmem_limit_bytes=vmem)


def _dot(a, b):
    return jnp.dot(a, b, preferred_element_type=F32)


def _dot_nt(a, b):
    return lax.dot_general(a, b, (((1,), (1,)), ((), ())), preferred_element_type=F32)


def _sigmoid(x):
    return 1.0 / (1.0 + jnp.exp(-x))


def _softplus(z):
    return jnp.maximum(z, 0.0) + jnp.log(1.0 + jnp.exp(-jnp.abs(z)))


def _rms(xf, width=None):
    width = xf.shape[-1] if width is None else width
    ss = jnp.sum(xf * xf, axis=-1, keepdims=True) * (1.0 / width)
    return xf * lax.rsqrt(ss + NORM_EPS)


def _ada_kernel(c_ref, w_ref, b_ref, o_ref):
    c = c_ref[...]
    cond = c * _sigmoid(c)
    o_ref[...] = _dot(cond.astype(BF16), w_ref[...].astype(BF16)) + b_ref[...]


def ada_modulation(c, ada_w, ada_b, tn=1024):
    depth, d, n = ada_w.shape
    b = c.shape[0]
    return pl.pallas_call(
        _ada_kernel,
        grid=(depth, n // tn),
        in_specs=[
            pl.BlockSpec((b, d), lambda l, j: (0, 0)),
            pl.BlockSpec((None, d, tn), lambda l, j: (l, 0, j)),
            pl.BlockSpec((None, 1, tn), lambda l, j: (l, 0, j)),
        ],
        out_specs=pl.BlockSpec((None, b, tn), lambda l, j: (l, 0, j)),
        out_shape=jax.ShapeDtypeStruct((depth, b, n), F32),
        compiler_params=_cparams(("parallel", "parallel")),
        name="ada_modulation",
    )(c, ada_w, ada_b.reshape(depth, 1, n))


def _norm_mod_rows(x_ref, g_ref, sh_ref, sc_ref, dst_ref, rows=32):
    tm = x_ref.shape[0]
    mul = g_ref[...] * (1.0 + sc_ref[...])
    sh = sh_ref[...]

    def body(r, _):
        sl = pl.ds(pl.multiple_of(r * rows, rows), rows)
        dst_ref[sl, :] = (_rms(x_ref[sl, :]) * mul + sh).astype(dst_ref.dtype)
        return 0

    lax.fori_loop(0, tm // rows, body, 0, unroll=2)


def _proj_kernel(x_ref, g_ref, sh_ref, sc_ref, w_ref, o_ref, h_ref):
    @pl.when(pl.program_id(1) == 0)
    def _():
        _norm_mod_rows(x_ref, g_ref, sh_ref, sc_ref, h_ref)

    o_ref[...] = _dot(h_ref[...], w_ref[...]).astype(o_ref.dtype)


def norm_mod_proj(x2, g, shift, scale, w, seq, tm=1024, tn=2048):
    t, d = x2.shape
    n = w.shape[1]
    tm = min(tm, seq)
    per_b = seq // tm
    bsz = shift.shape[0]
    return pl.pallas_call(
        _proj_kernel,
        grid=(t // tm, n // tn),
        in_specs=[
            pl.BlockSpec((tm, d), lambda i, j: (i, 0)),
            pl.BlockSpec((1, d), lambda i, j: (0, 0)),
            pl.BlockSpec((None, 1, d), lambda i, j: (i // per_b, 0, 0)),
            pl.BlockSpec((None, 1, d), lambda i, j: (i // per_b, 0, 0)),
            pl.BlockSpec((d, tn), lambda i, j: (0, j)),
        ],
        out_specs=pl.BlockSpec((tm, tn), lambda i, j: (i, j)),
        out_shape=jax.ShapeDtypeStruct((t, n), BF16),
        scratch_shapes=[pltpu.VMEM((tm, d), BF16)],
        compiler_params=_cparams(("parallel", "arbitrary")),
        name="norm_mod_proj",
    )(x2, g.reshape(1, d), shift.reshape(bsz, 1, d), scale.reshape(bsz, 1, d), w)


def _rope_tables(positions, dim):
    inv_freq = ROPE_THETA ** (-jnp.arange(0, dim, 2, dtype=F32) / dim)
    ang = positions.astype(F32)[:, None] * inv_freq[None, :]
    cos, sin = jnp.cos(ang), jnp.sin(ang)
    pad = jnp.zeros((positions.shape[0], LANES - dim), F32)
    return (jnp.concatenate([cos, cos, pad], axis=-1),
            jnp.concatenate([-sin, sin, pad], axis=-1))


def _rope_full(z, cos2, sin_s):
    return z * cos2 + pltpu.roll(z, 64, 1) * sin_s


def _rope_64(z, cos2, sin_s):
    lane = lax.broadcasted_iota(jnp.int32, z.shape, 1)
    partner = jnp.where(lane < 32, pltpu.roll(z, 96, 1), pltpu.roll(z, 32, 1))
    return z * cos2 + partner * sin_s


def _mla_prep_kernel(p_ref, qg_ref, kvg_ref, wuq_ref, wukv_ref, qhg_ref, khg_ref,
                     cos_ref, sin_ref, q_ref, k_ref, v_ref):
    p = p_ref[...].astype(F32)
    c_q = p[:, COL_CQ:COL_CQ + MLA_Q_RANK]
    c_kv = p[:, COL_CKV:COL_CKV + MLA_KV_RANK]
    k_pe = p[:, COL_KPE:COL_KPE + LANES]
    q = _dot((_rms(c_q) * qg_ref[...]).astype(BF16), wuq_ref[...])
    kv = _dot((_rms(c_kv) * kvg_ref[...]).astype(BF16), wukv_ref[...])
    cos2, sin_s = cos_ref[...], sin_ref[...]
    qhg, khg = qhg_ref[...], khg_ref[...]
    scale = MLA_QK_DIM ** -0.5
    pe_ss = jnp.sum(k_pe * k_pe, axis=-1, keepdims=True)
    k_rope = _rope_64(k_pe * khg[:, LANES:], cos2, sin_s)
    for h in range(GROUP_HEADS):
        qh = q[:, h * MLA_QK_PAD:(h + 1) * MLA_QK_PAD]
        qh = _rms(qh, MLA_QK_DIM) * qhg * scale
        q_ref[h, :, :LANES] = qh[:, :LANES].astype(q_ref.dtype)
        q_ref[h, :, LANES:] = _rope_64(qh[:, LANES:], cos2, sin_s).astype(q_ref.dtype)
        k_nope = kv[:, 2 * h * LANES:(2 * h + 1) * LANES]
        ss = (jnp.sum(k_nope * k_nope, axis=-1, keepdims=True) + pe_ss) * (1.0 / MLA_QK_DIM)
        r = lax.rsqrt(ss + NORM_EPS)
        k_ref[h, :, :LANES] = (k_nope * r * khg[:, :LANES]).astype(k_ref.dtype)
        k_ref[h, :, LANES:] = (k_rope * r).astype(k_ref.dtype)
        v_ref[h] = kv[:, (2 * h + 1) * LANES:(2 * h + 2) * LANES].astype(v_ref.dtype)


def mla_prep(proj3, q_norm_g, kv_norm_g, w_uq_pad, w_ukv, q_head_g_pad, k_head_g_pad,
             cos_pe, sin_pe, tm=512):
    bsz, seq, _ = proj3.shape
    h = GROUP_HEADS
    n_in = COL_KPE + LANES
    const = lambda shape: pl.BlockSpec(shape, lambda b, i: (0,) * len(shape))
    return pl.pallas_call(
        _mla_prep_kernel,
        grid=(bsz, seq // tm),
        in_specs=[
            pl.BlockSpec((None, tm, n_in), lambda b, i: (b, i, 0)),
            const((1, MLA_Q_RANK)), const((1, MLA_KV_RANK)),
            const((MLA_Q_RANK, h * MLA_QK_PAD)), const((MLA_KV_RANK, h * 2 * LANES)),
            const((1, MLA_QK_PAD)), const((1, MLA_QK_PAD)),
            pl.BlockSpec((tm, LANES), lambda b, i: (i, 0)),
            pl.BlockSpec((tm, LANES), lambda b, i: (i, 0)),
        ],
        out_specs=[
            pl.BlockSpec((None, h, tm, MLA_QK_PAD), lambda b, i: (b, 0, i, 0)),
            pl.BlockSpec((None, h, tm, MLA_QK_PAD), lambda b, i: (b, 0, i, 0)),
            pl.BlockSpec((None, h, tm, LANES), lambda b, i: (b, 0, i, 0)),
        ],
        out_shape=[
            jax.ShapeDtypeStruct((bsz, h, seq, MLA_QK_PAD), BF16),
            jax.ShapeDtypeStruct((bsz, h, seq, MLA_QK_PAD), BF16),
            jax.ShapeDtypeStruct((bsz, h, seq, LANES), BF16),
        ],
        compiler_params=_cparams(("parallel", "parallel")),
        name="mla_prep",
    )(proj3, q_norm_g.reshape(1, -1), kv_norm_g.reshape(1, -1), w_uq_pad, w_ukv,
      q_head_g_pad, k_head_g_pad, cos_pe, sin_pe)


def _moba_gates(q_ref, kmean_ref):
    km = kmean_ref[...]
    nb = km.shape[0]
    hi = km.astype(BF16).astype(F32)
    mid = (km - hi).astype(BF16).astype(F32)
    lo = (km - hi - mid).astype(BF16).astype(F32)
    pieces = jnp.concatenate([hi, mid, lo, jnp.zeros_like(km)], axis=0).astype(BF16)
    r = _dot_nt(pieces, q_ref[...])
    return r[:nb] + r[nb:2 * nb] + r[2 * nb:3 * nb]


def _moba_block_choice(gate, n_past):
    blk = lax.broadcasted_iota(jnp.int32, gate.shape, 0)
    rank = jnp.zeros(gate.shape, jnp.int32)
    for o in range(n_past):
        g_o = gate[o:o + 1, :]
        tie = jnp.where(o < blk, 1, 0)
        rank = rank + jnp.where(g_o > gate, 1, jnp.where(g_o == gate, tie, 0))
    return jnp.where(rank < MOBA_TOPK, 1.0, 0.0)


def _attend(q_ref, k_ref, v_ref, kmean_ref, o_ref):
    moba = kmean_ref is not None
    if moba:
        gates = _moba_gates(q_ref, kmean_ref)
    seq = q_ref.shape[0]
    t = ATTN_BLOCK
    v_t = jnp.transpose(v_ref[...].astype(F32)).astype(BF16)
    key = lax.broadcasted_iota(jnp.int32, (t, t), 0)
    qry = lax.broadcasted_iota(jnp.int32, (t, t), 1)
    causal = key <= qry
    for i in range(seq // t):
        own = slice(i * t, (i + 1) * t)
        q = q_ref[own, :]
        s_own = jnp.where(causal, _dot_nt(k_ref[own, :], q), NEG)
        m = jnp.max(s_own, axis=0, keepdims=True)
        s_past = []
        if i > 0:
            s_all = _dot_nt(k_ref[:i * t, :], q)
            if moba and i > MOBA_TOPK:
                sel = _moba_block_choice(gates[:, own], i)
                s_past = [jnp.where(sel[n:n + 1, :] > 0.5, s_all[n * t:(n + 1) * t, :], NEG)
                          for n in range(i)]
            else:
                s_past = [s_all]
            for s in s_past:
                m = jnp.maximum(m, jnp.max(s, axis=0, keepdims=True))
        p = jnp.exp(s_own - m)
        l = jnp.sum(p, axis=0, keepdims=True)
        acc_t = _dot(v_t[:, own], p.astype(BF16))
        start = 0
        for s in s_past:
            p = jnp.exp(s - m)
            l = l + jnp.sum(p, axis=0, keepdims=True)
            acc_t = acc_t + _dot(v_t[:, start:start + s.shape[0]], p.astype(BF16))
            start += s.shape[0]
        o_ref[own, :] = jnp.transpose(acc_t / l).astype(o_ref.dtype)


def _head_spec(seq, d):
    return pl.BlockSpec((None, None, seq, d), lambda b, hh: (b, hh, 0, 0))


def _proj_col_spec(seq, col):
    cb = col // LANES
    return pl.BlockSpec((None, seq, LANES), lambda b, hh: (b, 0, cb + hh))


def _mla_attn_kernel(q_ref, k_ref, v_ref, o_ref):
    _attend(q_ref, k_ref, v_ref, None, o_ref)


def mla_attention(q, k, v):
    bsz, h, seq, dk = q.shape
    return pl.pallas_call(
        _mla_attn_kernel,
        grid=(bsz, h),
        in_specs=[_head_spec(seq, dk), _head_spec(seq, dk), _head_spec(seq, LANES)],
        out_specs=pl.BlockSpec((None, seq, LANES), lambda b, hh: (b, 0, hh)),
        out_shape=jax.ShapeDtypeStruct((bsz, seq, h * LANES), BF16),
        compiler_params=_cparams(("parallel", "parallel")),
        name="mla_attention",
    )(q, k, v)


def _moba_prep(q_ref, k_ref, qg_ref, kg_ref, cos_ref, sin_ref, qo_ref, ko_ref, km_ref):
    seq = q_ref.shape[0]
    scale = HEAD_DIM ** -0.5
    for blk in range(seq // MOBA_BLOCK):
        sl = slice(blk * MOBA_BLOCK, (blk + 1) * MOBA_BLOCK)
        cos2, sin_s = cos_ref[sl, :], sin_ref[sl, :]
        qn = _rms(q_ref[sl, :].astype(F32)) * qg_ref[...]
        qo_ref[sl, :] = (_rope_full(qn, cos2, sin_s) * scale).astype(qo_ref.dtype)
        kn = _rope_full(_rms(k_ref[sl, :].astype(F32)) * kg_ref[...], cos2, sin_s)
        ko_ref[sl, :] = kn.astype(ko_ref.dtype)
        km_ref[blk:blk + 1, :] = jnp.mean(kn, axis=0, keepdims=True)


def _moba_attn_kernel(q_ref, k_ref, v_ref, qg_ref, kg_ref, cos_ref, sin_ref, o_ref,
                      qn_ref, kn_ref, km_ref):
    _moba_prep(q_ref, k_ref, qg_ref, kg_ref, cos_ref, sin_ref, qn_ref, kn_ref, km_ref)
    _attend(qn_ref, kn_ref, v_ref, km_ref, o_ref)


def moba_attention(proj3, q_head_g, k_head_g, cos_full, sin_full):
    bsz, seq, _ = proj3.shape
    h = GROUP_HEADS
    const = lambda shape: pl.BlockSpec(shape, lambda b, hh: (0,) * len(shape))
    return pl.pallas_call(
        _moba_attn_kernel,
        grid=(bsz, h),
        in_specs=[
            _proj_col_spec(seq, COL_MOBA),
            _proj_col_spec(seq, COL_MOBA + GROUP_WIDTH),
            _proj_col_spec(seq, COL_MOBA + 2 * GROUP_WIDTH),
            const((1, LANES)), const((1, LANES)),
            const((seq, LANES)), const((seq, LANES)),
        ],
        out_specs=pl.BlockSpec((None, seq, LANES), lambda b, hh: (b, 0, hh)),
        out_shape=jax.ShapeDtypeStruct((bsz, seq, h * LANES), BF16),
        scratch_shapes=[pltpu.VMEM((seq, LANES), BF16), pltpu.VMEM((seq, LANES), BF16),
                        pltpu.VMEM((seq // MOBA_BLOCK, LANES), F32)],
        compiler_params=_cparams(("parallel", "parallel")),
        name="moba_attention",
    )(proj3, proj3, proj3, q_head_g.reshape(1, -1), k_head_g.reshape(1, -1), cos_full, sin_full)


SB_DEAD_TAIL = -104.0


def _sb_tile(q, k_blk, v_blk, tail, later, strict):
    z = _dot_nt(q, k_blk) * (HEAD_DIM ** -0.5)
    log_1m = -_softplus(z)
    if strict is not None:
        log_1m = jnp.where(strict, log_1m, 0.0)
    hi = log_1m.astype(BF16)
    lo = (log_1m - hi.astype(F32)).astype(BF16)
    suffix = _dot(hi, later) + _dot(lo, later)
    a = jnp.exp(z + log_1m + suffix + tail)
    if strict is not None:
        a = jnp.where(strict, a, 0.0)
    return _dot(a.astype(v_blk.dtype), v_blk), jnp.sum(log_1m, axis=-1, keepdims=True)


def _sb_kernel(q_ref, k_ref, v_ref, o_ref, acc_ref, tail_ref):
    seq = q_ref.shape[0]
    t = ATTN_BLOCK
    row = lax.broadcasted_iota(jnp.int32, (t, t), 0)
    col = lax.broadcasted_iota(jnp.int32, (t, t), 1)
    strict = col < row
    later = jnp.where(row > col, 1.0, 0.0).astype(BF16)
    blk = lambda n: slice(n * t, (n + 1) * t)

    for i in range(seq // t):
        q = q_ref[blk(i), :]
        acc, tail = _sb_tile(q, k_ref[blk(i), :], v_ref[blk(i), :], 0.0, later, strict)
        if i > 0:
            y, s = _sb_tile(q, k_ref[blk(i - 1), :], v_ref[blk(i - 1), :], tail, later, None)
            acc, tail = acc + y, tail + s
        acc_ref[blk(i), :] = acc
        tail_ref[blk(i), :] = tail

    for i in range(2, seq // t):
        alive = jnp.max(tail_ref[blk(i), :], axis=0, keepdims=True)[0, 0] > SB_DEAD_TAIL

        @pl.when(alive)
        def _():
            q = q_ref[blk(i), :]
            acc, tail = acc_ref[blk(i), :], tail_ref[blk(i), :]
            for n in range(i - 2, -1, -1):
                y, s = _sb_tile(q, k_ref[blk(n), :], v_ref[blk(n), :], tail, later, None)
                acc, tail = acc + y, tail + s
            acc_ref[blk(i), :] = acc

    o_ref[...] = acc_ref[...].astype(o_ref.dtype)


def sb_attention(proj3):
    bsz, seq, _ = proj3.shape
    return pl.pallas_call(
        _sb_kernel,
        grid=(bsz, GROUP_HEADS),
        in_specs=[
            _proj_col_spec(seq, COL_SB),
            _proj_col_spec(seq, COL_SB + GROUP_WIDTH),
            _proj_col_spec(seq, COL_SB + 2 * GROUP_WIDTH),
        ],
        out_specs=pl.BlockSpec((None, seq, LANES), lambda b, hh: (b, 0, hh)),
        out_shape=jax.ShapeDtypeStruct((bsz, seq, GROUP_WIDTH), BF16),
        scratch_shapes=[pltpu.VMEM((seq, LANES), F32), pltpu.VMEM((seq, 1), F32)],
        compiler_params=_cparams(("parallel", "parallel")),
        name="sb_attention",
    )(proj3, proj3, proj3)


def _ret_kernel(q_ref, k_ref, v_ref, g_ref, cos_ref, sin_ref, lg_ref, o_ref):
    seq = q_ref.shape[0]
    c = RET_CHUNK
    log_gamma = lg_ref[...]
    ri = lax.broadcasted_iota(jnp.int32, (c, c), 0).astype(F32)
    ci = lax.broadcasted_iota(jnp.int32, (c, c), 1).astype(F32)
    rel = ri - ci
    intra_decay = jnp.where(rel >= 0, jnp.exp(jnp.maximum(rel, 0.0) * log_gamma), 0.0)
    idx = lax.broadcasted_iota(jnp.int32, (c, 1), 0).astype(F32)
    query_decay = jnp.exp((idx + 1.0) * log_gamma)
    key_decay = jnp.exp((c - 1.0 - idx) * log_gamma)
    chunk_decay = jnp.exp(c * log_gamma)
    k_scale = HEAD_DIM ** -0.5

    state = jnp.zeros((HEAD_DIM, HEAD_DIM), F32)
    for n in range(seq // c):
        sl = slice(n * c, (n + 1) * c)
        cos2, sin_s = cos_ref[sl, :], sin_ref[sl, :]
        q = _rope_full(q_ref[sl, :].astype(F32), cos2, sin_s)
        k = _rope_full(k_ref[sl, :].astype(F32), cos2, sin_s) * k_scale
        vb = v_ref[sl, :]
        qb = q.astype(BF16)
        scores = _dot_nt(qb, k.astype(BF16)) * intra_decay
        y = _dot(scores.astype(BF16), vb)
        y = y + _dot(qb, state.astype(BF16)) * query_decay
        kd_t = jnp.transpose(k * key_decay).astype(BF16)
        state = state * chunk_decay + _dot(kd_t, vb)
        mu = jnp.mean(y, axis=-1, keepdims=True)
        yc = y - mu
        var = jnp.mean(yc * yc, axis=-1, keepdims=True)
        yn = yc * lax.rsqrt(var + NORM_EPS)
        g = g_ref[sl, :].astype(F32)
        o_ref[sl, :] = (g * _sigmoid(g) * yn).astype(o_ref.dtype)


def retention(proj3, cos_full, sin_full):
    bsz, seq, _ = proj3.shape
    h = GROUP_HEADS
    log_gamma = jnp.log(1.0 - 2.0 ** (-5.0 - jnp.arange(h, dtype=F32))).reshape(h, 1, 1)
    col = lambda k: pl.BlockSpec((None, seq, LANES),
                                 lambda b, hh: (b, 0, (COL_RET + k * GROUP_WIDTH) // LANES + hh))
    const = lambda shape: pl.BlockSpec(shape, lambda b, hh: (0,) * len(shape))
    return pl.pallas_call(
        _ret_kernel,
        grid=(bsz, h),
        in_specs=[col(0), col(1), col(2), col(3), const((seq, LANES)), const((seq, LANES)),
                  pl.BlockSpec((None, 1, 1), lambda b, hh: (hh, 0, 0))],
        out_specs=pl.BlockSpec((None, seq, LANES), lambda b, hh: (b, 0, hh)),
        out_shape=jax.ShapeDtypeStruct((bsz, seq, GROUP_WIDTH), BF16),
        compiler_params=_cparams(("parallel", "parallel")),
        name="retention",
    )(proj3, proj3, proj3, proj3, cos_full, sin_full, log_gamma)


def _out_proj_kernel(y0_ref, y1_ref, y2_ref, y3_ref, gg_ref, w_ref, x_ref, gate_ref, o_ref, h_ref):
    @pl.when(pl.program_id(1) == 0)
    def _():
        for grp, y_ref in enumerate((y0_ref, y1_ref, y2_ref, y3_ref)):
            yn = _rms(y_ref[...].astype(F32)) * gg_ref[grp:grp + 1, :]
            h_ref[:, grp * GROUP_WIDTH:(grp + 1) * GROUP_WIDTH] = yn.astype(h_ref.dtype)

    o_ref[...] = x_ref[...] + gate_ref[...] * _dot(h_ref[...], w_ref[...])


def out_proj_residual(ys, group_g, w_out, x2, gate, seq, tm=1024, tn=1024):
    t, d = x2.shape
    tm = min(tm, seq)
    per_b = seq // tm
    bsz = gate.shape[0]
    y_spec = pl.BlockSpec((tm, GROUP_WIDTH), lambda i, j: (i, 0))
    return pl.pallas_call(
        _out_proj_kernel,
        grid=(t // tm, d // tn),
        in_specs=[y_spec, y_spec, y_spec, y_spec,
                  pl.BlockSpec((4, GROUP_WIDTH), lambda i, j: (0, 0)),
                  pl.BlockSpec((4 * GROUP_WIDTH, tn), lambda i, j: (0, j)),
                  pl.BlockSpec((tm, tn), lambda i, j: (i, j)),
                  pl.BlockSpec((None, 1, tn), lambda i, j: (i // per_b, 0, j))],
        out_specs=pl.BlockSpec((tm, tn), lambda i, j: (i, j)),
        out_shape=jax.ShapeDtypeStruct((t, d), F32),
        scratch_shapes=[pltpu.VMEM((tm, 4 * GROUP_WIDTH), BF16)],
        compiler_params=_cparams(("parallel", "arbitrary")),
        name="out_proj_residual",
    )(*[y.reshape(t, GROUP_WIDTH) for y in ys], group_g, w_out, x2, gate.reshape(bsz, 1, d))


def _swiglu_act(h, w1_ref, w3_ref):
    a = _dot(h, w1_ref[...])
    return (a * _sigmoid(a) * _dot(h, w3_ref[...])).astype(BF16)


FFN_UP_STEPS = 11
FFN_DOWN_STEPS = 4


def _down_proj(a_ref, w2_ref):
    n, _, tf = a_ref.shape
    acc = _dot(a_ref[0], w2_ref[:tf, :])
    for c in range(1, n):
        acc = acc + _dot(a_ref[c], w2_ref[c * tf:(c + 1) * tf, :])
    return acc


def _ffn_kernel(x_ref, g_ref, sh_ref, sc_ref, w1_ref, w3_ref, w2_ref, xc_ref, gate_ref, o_ref,
                h_ref, a_ref):
    s = pl.program_id(1)

    @pl.when(s == 0)
    def _():
        _norm_mod_rows(x_ref, g_ref, sh_ref, sc_ref, h_ref)

    @pl.when(s < FFN_UP_STEPS)
    def _():
        a_ref[s] = _swiglu_act(h_ref[...], w1_ref, w3_ref)

    @pl.when(s >= FFN_UP_STEPS)
    def _():
        o_ref[...] = xc_ref[...] + gate_ref[...] * _down_proj(a_ref, w2_ref)


def _ffn_chunks(ffn, d):
    return ffn // FFN_UP_STEPS, d // FFN_DOWN_STEPS


def _up_idx(s):
    return jnp.minimum(s, FFN_UP_STEPS - 1)


def _down_idx(s):
    return jnp.maximum(s - FFN_UP_STEPS, 0)


def dense_ffn_residual(x2, g, shift, scale, gate, w1, w3, w2, seq, tm=1024):
    t, d = x2.shape
    ffn = w1.shape[1]
    tf, tn = _ffn_chunks(ffn, d)
    w1, w3, w2 = w1.astype(BF16), w3.astype(BF16), w2.astype(BF16)
    tm = min(tm, seq)
    per_b = seq // tm
    bsz = gate.shape[0]
    row = lambda: pl.BlockSpec((None, 1, d), lambda i, s: (i // per_b, 0, 0))
    return pl.pallas_call(
        _ffn_kernel,
        grid=(t // tm, FFN_UP_STEPS + FFN_DOWN_STEPS),
        in_specs=[pl.BlockSpec((tm, d), lambda i, s: (i, 0), pipeline_mode=pl.Buffered(1)),
                  pl.BlockSpec((1, d), lambda i, s: (0, 0)),
                  row(), row(),
                  pl.BlockSpec((d, tf), lambda i, s: (0, _up_idx(s))),
                  pl.BlockSpec((d, tf), lambda i, s: (0, _up_idx(s))),
                  pl.BlockSpec((ffn, tn), lambda i, s: (0, _down_idx(s))),
                  pl.BlockSpec((tm, tn), lambda i, s: (i, _down_idx(s))),
                  pl.BlockSpec((None, 1, tn), lambda i, s: (i // per_b, 0, _down_idx(s)))],
        out_specs=pl.BlockSpec((tm, tn), lambda i, s: (i, _down_idx(s))),
        out_shape=jax.ShapeDtypeStruct((t, d), F32),
        scratch_shapes=[pltpu.VMEM((tm, d), BF16), pltpu.VMEM((FFN_UP_STEPS, tm, tf), BF16)],
        compiler_params=_cparams(("parallel", "arbitrary")),
        name="dense_ffn_residual",
    )(x2, g.reshape(1, d), shift.reshape(bsz, 1, d), scale.reshape(bsz, 1, d),
      w1, w3, w2, x2, gate.reshape(bsz, 1, d))


def _router_kernel(x_ref, g_ref, sh_ref, sc_ref, rw_ref, h_ref, idx_ref, gates_ref):
    _norm_mod_rows(x_ref, g_ref, sh_ref, sc_ref, h_ref)
    hf, rw = h_ref[...], rw_ref[...]
    h_hi, w_hi = hf.astype(BF16), rw.astype(BF16)
    h_lo = (hf - h_hi.astype(F32)).astype(BF16)
    w_lo = (rw - w_hi.astype(F32)).astype(BF16)
    logits = _dot(h_hi, w_hi) + _dot(h_hi, w_lo) + _dot(h_lo, w_hi)
    lane = lax.broadcasted_iota(jnp.int32, logits.shape, 1)
    lane_f = lane.astype(F32)
    logits = jnp.where(lane < N_EXPERTS, logits, -jnp.inf)
    m0 = jnp.max(logits, axis=-1, keepdims=True)
    e0 = jnp.min(jnp.where(logits == m0, lane_f, float(LANES)), axis=-1, keepdims=True)
    rest = jnp.where(lane_f == e0, -jnp.inf, logits)
    m1 = jnp.max(rest, axis=-1, keepdims=True)
    e1 = jnp.min(jnp.where(rest == m1, lane_f, float(LANES)), axis=-1, keepdims=True)
    p1 = jnp.exp(m1 - m0)
    g0 = 1.0 / (1.0 + p1)
    idx_ref[...] = jnp.where(lane == 0, e0, jnp.where(lane == 1, e1, 0.0)).astype(jnp.int32)
    gates_ref[...] = jnp.where(lane == 0, g0, jnp.where(lane == 1, p1 * g0, 0.0))


def moe_router(x2, g, shift, scale, router_w, seq, tm=256):
    t, d = x2.shape
    tm = min(tm, seq)
    per_b = seq // tm
    bsz = shift.shape[0]
    rw = jnp.zeros((d, LANES), F32).at[:, :N_EXPERTS].set(router_w)
    row = lambda: pl.BlockSpec((None, 1, d), lambda i: (i // per_b, 0, 0))
    return pl.pallas_call(
        _router_kernel,
        grid=(t // tm,),
        in_specs=[pl.BlockSpec((tm, d), lambda i: (i, 0)),
                  pl.BlockSpec((1, d), lambda i: (0, 0)),
                  row(), row(),
                  pl.BlockSpec((d, LANES), lambda i: (0, 0))],
        out_specs=[pl.BlockSpec((tm, d), lambda i: (i, 0)),
                   pl.BlockSpec((tm, LANES), lambda i: (i, 0)),
                   pl.BlockSpec((tm, LANES), lambda i: (i, 0))],
        out_shape=[jax.ShapeDtypeStruct((t, d), F32),
                   jax.ShapeDtypeStruct((t, LANES), jnp.int32),
                   jax.ShapeDtypeStruct((t, LANES), F32)],
        compiler_params=_cparams(("parallel",)),
        name="moe_router",
    )(x2, g.reshape(1, d), shift.reshape(bsz, 1, d), scale.reshape(bsz, 1, d), rw)


GATHER_UNROLL = 8
MOE_TILE_ROWS = 1056


def _row_copy(src_ref, dst_ref, sem, src_row, dst_row):
    return pltpu.make_async_copy(src_ref.at[pl.ds(src_row, 1), :],
                                 dst_ref.at[pl.ds(dst_row, 1), :], sem)


def _start_row_gather(src_ref, dst_ref, sem, row_of, n_rows):
    def start(r, _):
        _row_copy(src_ref, dst_ref, sem, row_of(r), r).start()
        return 0

    lax.fori_loop(0, n_rows, start, 0, unroll=GATHER_UNROLL)


def _wait_row_gather(dst_ref, sem):
    pltpu.make_async_copy(dst_ref, dst_ref, sem).wait()


def _moe_ffn_kernel(tile_e_ref, tile_ok_ref, buf_t_ref, h_hbm, w1_ref, w3_ref, w2_ref, o_ref,
                    hf_ref, hb_ref, a_ref, sem):
    i = pl.program_id(0)
    f = pl.program_id(1)
    n_tiles = pl.num_programs(0)
    nf = FFN_UP_STEPS
    tm = hb_ref.shape[0]
    rows_per_step = tm // nf
    ok = tile_ok_ref[i] > 0
    last = (i == n_tiles - 1) & (f == nf - 1)
    nxt = jnp.minimum(i + 1, n_tiles - 1)
    started = (i == 0) | (tile_ok_ref[jnp.maximum(i - 1, 0)] > 0)

    @pl.when(f == 0)
    def _():
        @pl.when(i == 0)
        def _():
            _start_row_gather(h_hbm, hf_ref, sem, lambda r: buf_t_ref[r], tm)

        @pl.when(started)
        def _():
            _wait_row_gather(hf_ref, sem)

        @pl.when(ok)
        def _():
            hb_ref[...] = hf_ref[...].astype(hb_ref.dtype)

    @pl.when(ok & (f < nf))
    def _():
        base = f * rows_per_step
        for r in range(rows_per_step):
            _row_copy(h_hbm, hf_ref, sem, buf_t_ref[nxt * tm + base + r], base + r).start()
        a_ref[f] = _swiglu_act(hb_ref[...], w1_ref, w3_ref)

    @pl.when(ok & last)
    def _():
        _wait_row_gather(hf_ref, sem)

    @pl.when(f >= nf)
    def _():
        @pl.when(ok)
        def _():
            o_ref[...] = _down_proj(a_ref, w2_ref)

        @pl.when(jnp.logical_not(ok))
        def _():
            o_ref[...] = jnp.zeros_like(o_ref)


def moe_expert_ffn(h2, tile_e, tile_ok, buf_t, w1, w3, w2, tm):
    t, d = h2.shape
    ffn = w1.shape[2]
    tf, tn = _ffn_chunks(ffn, d)
    w1, w3, w2 = w1.astype(BF16), w3.astype(BF16), w2.astype(BF16)
    n_tiles = tile_e.shape[0]

    def up(i, s, ok):
        return jnp.where(ok[i] > 0, _up_idx(s), FFN_UP_STEPS - 1)

    def down(i, s, ok):
        return jnp.where(ok[i] > 0, _down_idx(s), 0)

    grid_spec = pltpu.PrefetchScalarGridSpec(
        num_scalar_prefetch=3,
        grid=(n_tiles, FFN_UP_STEPS + FFN_DOWN_STEPS),
        in_specs=[pl.BlockSpec(memory_space=pl.ANY),
                  pl.BlockSpec((None, d, tf), lambda i, s, te, ok, bt: (te[i], 0, up(i, s, ok))),
                  pl.BlockSpec((None, d, tf), lambda i, s, te, ok, bt: (te[i], 0, up(i, s, ok))),
                  pl.BlockSpec((None, ffn, tn), lambda i, s, te, ok, bt: (te[i], 0, down(i, s, ok)))],
        out_specs=pl.BlockSpec((tm, tn), lambda i, s, te, ok, bt: (i, _down_idx(s))),
        scratch_shapes=[pltpu.VMEM((tm, d), F32), pltpu.VMEM((tm, d), BF16),
                        pltpu.VMEM((FFN_UP_STEPS, tm, tf), BF16), pltpu.SemaphoreType.DMA(())],
    )
    return pl.pallas_call(
        _moe_ffn_kernel,
        grid_spec=grid_spec,
        out_shape=jax.ShapeDtypeStruct((n_tiles * tm, d), F32),
        compiler_params=_cparams(("arbitrary", "arbitrary"), VMEM_LIMIT_MOE),
        name="moe_expert_ffn",
    )(tile_e, tile_ok, buf_t, h2, w1, w3, w2)


def _moe_combine_kernel(d0_ref, d1_ref, y_hbm, x_ref, gate_ref, gates_ref, o_ref,
                        y0_ref, y1_ref, sems0, sems1):
    i = pl.program_id(0)
    tm = x_ref.shape[0]
    slot = i % 2

    def start_gather(tile, into):
        def start(r, _):
            _row_copy(y_hbm, y0_ref.at[into], sems0.at[into], d0_ref[tile * tm + r], r).start(priority=0)
            _row_copy(y_hbm, y1_ref.at[into], sems1.at[into], d1_ref[tile * tm + r], r).start(priority=1)
            return 0

        lax.fori_loop(0, tm, start, 0, unroll=GATHER_UNROLL)

    @pl.when(i == 0)
    def _():
        start_gather(0, 0)

    @pl.when(i + 1 < pl.num_programs(0))
    def _():
        start_gather(i + 1, 1 - slot)

    _wait_row_gather(y0_ref.at[slot], sems0.at[slot])
    _wait_row_gather(y1_ref.at[slot], sems1.at[slot])
    gates = gates_ref[...]
    y = gates[:, 0:1] * y0_ref[slot] + gates[:, 1:2] * y1_ref[slot]
    o_ref[...] = x_ref[...] + gate_ref[...] * y


def moe_combine_residual(dest0, dest1, y_buf, x2, gate, gates, seq, tm=256):
    t, d = x2.shape
    tm = min(tm, seq)
    per_b = seq // tm
    bsz = gate.shape[0]
    grid_spec = pltpu.PrefetchScalarGridSpec(
        num_scalar_prefetch=2,
        grid=(t // tm,),
        in_specs=[pl.BlockSpec(memory_space=pl.ANY),
                  pl.BlockSpec((tm, d), lambda i, d0, d1: (i, 0)),
                  pl.BlockSpec((None, 1, d), lambda i, d0, d1: (i // per_b, 0, 0)),
                  pl.BlockSpec((tm, LANES), lambda i, d0, d1: (i, 0))],
        out_specs=pl.BlockSpec((tm, d), lambda i, d0, d1: (i, 0)),
        scratch_shapes=[pltpu.VMEM((2, tm, d), F32), pltpu.VMEM((2, tm, d), F32),
                        pltpu.SemaphoreType.DMA((2,)), pltpu.SemaphoreType.DMA((2,))],
    )
    return pl.pallas_call(
        _moe_combine_kernel,
        grid_spec=grid_spec,
        out_shape=jax.ShapeDtypeStruct((t, d), F32),
        compiler_params=_cparams(("arbitrary",)),
        name="moe_combine_residual",
    )(dest0, dest1, y_buf, x2, gate.reshape(bsz, 1, d), gates)


def _moe_routing_tables(idx, tm):
    t = idx.shape[0]
    n_slots = t * MOE_TOPK
    n_tiles = n_slots // tm + N_EXPERTS
    flat_e = idx[:, :MOE_TOPK].reshape(-1)
    onehot = (flat_e[:, None] == jnp.arange(N_EXPERTS)[None, :]).astype(jnp.int32)
    counts = jnp.sum(onehot, axis=0)
    rank = jnp.sum((jnp.cumsum(onehot, axis=0) - onehot) * onehot, axis=1)
    padded = (counts + tm - 1) // tm * tm
    pad_ends = jnp.cumsum(padded)
    pad_starts = pad_ends - padded
    dest = (pad_starts[flat_e] + rank).astype(jnp.int32)
    buf_t = jnp.zeros((n_tiles * tm,), jnp.int32).at[dest].set(
        (jnp.arange(n_slots) // MOE_TOPK).astype(jnp.int32))
    tile_start = jnp.arange(n_tiles) * tm
    tile_e = jnp.minimum(jnp.searchsorted(pad_ends, tile_start, side="right"),
                         N_EXPERTS - 1).astype(jnp.int32)
    tile_ok = (tile_start < pad_ends[-1]).astype(jnp.int32)
    dest2 = dest.reshape(t, MOE_TOPK)
    return tile_e, tile_ok, buf_t, dest2[:, 0], dest2[:, 1]


def moe_ffn_residual(x2, g, shift, scale, gate, router_w, w1, w3, w2, seq, tile_rows=MOE_TILE_ROWS):
    h2, idx, gates = moe_router(x2, g, shift, scale, router_w, seq)
    tile_e, tile_ok, buf_t, dest0, dest1 = _moe_routing_tables(idx, tile_rows)
    y_buf = moe_expert_ffn(h2, tile_e, tile_ok, buf_t, w1, w3, w2, tile_rows)
    return moe_combine_residual(dest0, dest1, y_buf, x2, gate, gates, seq)


def _pad_w_in_kernel(w_ref, o_ref):
    rows = w_ref.shape[0]
    kpe_end = MLA_Q_RANK + MLA_KV_RANK + MLA_ROPE_DIM
    tail = PROJ_WIDTH - (COL_MOBA + 3 * GROUP_WIDTH)
    w = w_ref[...]
    o_ref[...] = jnp.concatenate([
        w[:, :kpe_end], jnp.zeros((rows, LANES - MLA_ROPE_DIM), F32),
        w[:, kpe_end:], jnp.zeros((rows, tail), F32)], axis=1).astype(o_ref.dtype)


def _pad_w_in(w_in, rows=256):
    d, n = w_in.shape
    return pl.pallas_call(
        _pad_w_in_kernel,
        grid=(d // rows,),
        in_specs=[pl.BlockSpec((rows, n), lambda i: (i, 0))],
        out_specs=pl.BlockSpec((rows, PROJ_WIDTH), lambda i: (i, 0)),
        out_shape=jax.ShapeDtypeStruct((d, PROJ_WIDTH), BF16),
        compiler_params=_cparams(("parallel",)),
        name="pad_w_in",
    )(w_in)


def _pad_heads_192(a):
    lead = a.shape[:-1]
    a = a.reshape(lead + (GROUP_HEADS, MLA_QK_DIM))
    a = jnp.pad(a, [(0, 0)] * len(lead) + [(0, 0), (0, MLA_QK_PAD - MLA_QK_DIM)])
    return a.reshape(lead + (GROUP_HEADS * MLA_QK_PAD,))


def kernel(x, c, positions, ada_w, ada_b, norm_mix_g, norm_ffn_g, w_in, mla_q_norm_g, mla_kv_norm_g, mla_w_uq, mla_w_ukv, mla_q_head_g, mla_k_head_g, moba_q_head_g, moba_k_head_g, group_norm_g, w_out, ffn_w1, ffn_w3, ffn_w2, router_w, moe_w1, moe_w3, moe_w2):
    bsz, seq, d = x.shape
    depth = ada_w.shape[0]
    cos_pe, sin_pe = _rope_tables(positions, MLA_ROPE_DIM)
    cos_full, sin_full = _rope_tables(positions, HEAD_DIM)
    mod = ada_modulation(c, ada_w, ada_b)
    x2 = x.reshape(bsz * seq, d)
    for l in range(depth):
        shift_m, scale_m, gate_m, shift_f, scale_f, gate_f = jnp.split(mod[l], 6, axis=-1)
        proj = norm_mod_proj(x2, norm_mix_g[l], shift_m, scale_m, _pad_w_in(w_in[l]), seq)
        proj3 = proj.reshape(bsz, seq, PROJ_WIDTH)
        pad_g = lambda g: jnp.pad(g, (0, MLA_QK_PAD - MLA_QK_DIM)).reshape(1, MLA_QK_PAD)
        q_mla, k_mla, v_mla = mla_prep(
            proj3, mla_q_norm_g[l], mla_kv_norm_g[l],
            _pad_heads_192(mla_w_uq[l]).astype(BF16), mla_w_ukv[l].astype(BF16),
            pad_g(mla_q_head_g[l]), pad_g(mla_k_head_g[l]), cos_pe, sin_pe)
        y_mla = mla_attention(q_mla, k_mla, v_mla)
        y_ret = retention(proj3, cos_full, sin_full)
        y_sb = sb_attention(proj3)
        y_moba = moba_attention(proj3, moba_q_head_g[l], moba_k_head_g[l], cos_full, sin_full)
        x2 = out_proj_residual((y_mla, y_ret, y_sb, y_moba), group_norm_g[l],
                               w_out[l].astype(BF16), x2, gate_m, seq)
        j = l // 2
        if l % 2 == 0:
            x2 = dense_ffn_residual(x2, norm_ffn_g[l], shift_f, scale_f, gate_f,
                                    ffn_w1[j], ffn_w3[j], ffn_w2[j], seq)
        else:
            x2 = moe_ffn_residual(x2, norm_ffn_g[l], shift_f, scale_f, gate_f, router_w[j],
                                  moe_w1[j], moe_w3[j], moe_w2[j], seq)
    return x2.reshape(bsz, seq, d)
```

```python
import jax
import jax.numpy as jnp
from jax import lax
from jax.experimental import pallas as pl
from jax.experimental.pallas import tpu as pltpu

F32 = jnp.float32
BF16 = jnp.bfloat16

HEAD_DIM = 128
GROUP_HEADS = 4
GROUP_WIDTH = 512
MLA_Q_RANK = 512
MLA_KV_RANK = 256
MLA_NOPE_DIM = 128
MLA_ROPE_DIM = 64
MLA_QK_DIM = MLA_NOPE_DIM + MLA_ROPE_DIM
MLA_QK_PAD = 256
RET_CHUNK = 128
MOBA_BLOCK = 256
MOBA_TOPK = 3
ROPE_THETA = 10000.0
NORM_EPS = 1e-6
NEG = -1e30
N_EXPERTS = 8
MOE_TOPK = 2

LANES = 128
ATTN_BLOCK = 256
MLA_ATTN_BLOCK = 1024
VMEM_LIMIT = 56 * 1024 * 1024
VMEM_LIMIT_MOE = 62 * 1024 * 1024

COL_CQ = 0
COL_CKV = 512
COL_KPE = 768
COL_RET = 896
COL_SB = COL_RET + 4 * GROUP_WIDTH
COL_MOBA = COL_SB + 3 * GROUP_WIDTH
PROJ_WIDTH = 6144


def _cparams(sem, vmem=VMEM_LIMIT):
    return pltpu.CompilerParams(dimension_semantics=sem, vmem_limit_bytes=vmem)


def _dot(a, b):
    return jnp.dot(a, b, preferred_element_type=F32)


def _dot_nt(a, b):
    return lax.dot_general(a, b, (((1,), (1,)), ((), ())), preferred_element_type=F32)


def _sigmoid(x):
    return 1.0 / (1.0 + jnp.exp(-x))


def _softplus(z):
    return jnp.maximum(z, 0.0) + jnp.log(1.0 + jnp.exp(-jnp.abs(z)))


def _rms(xf, width=None):
    width = xf.shape[-1] if width is None else width
    ss = jnp.sum(xf * xf, axis=-1, keepdims=True) * (1.0 / width)
    return xf * lax.rsqrt(ss + NORM_EPS)


def _ada_kernel(c_ref, w_ref, b_ref, o_ref):
    c = c_ref[...]
    cond = c * _sigmoid(c)
    o_ref[...] = _dot(cond.astype(BF16), w_ref[...].astype(BF16)) + b_ref[...]


def ada_modulation(c, ada_w, ada_b, tn=1024):
    depth, d, n = ada_w.shape
    b = c.shape[0]
    return pl.pallas_call(
        _ada_kernel,
        grid=(depth, n // tn),
        in_specs=[
            pl.BlockSpec((b, d), lambda l, j: (0, 0)),
            pl.BlockSpec((None, d, tn), lambda l, j: (l, 0, j)),
            pl.BlockSpec((None, 1, tn), lambda l, j: (l, 0, j)),
        ],
        out_specs=pl.BlockSpec((None, b, tn), lambda l, j: (l, 0, j)),
        out_shape=jax.ShapeDtypeStruct((depth, b, n), F32),
        compiler_params=_cparams(("parallel", "parallel")),
        name="ada_modulation",
    )(c, ada_w, ada_b.reshape(depth, 1, n))


def _norm_mod_rows(x_ref, g_ref, sh_ref, sc_ref, dst_ref, rows=32):
    tm = x_ref.shape[0]
    mul = g_ref[...] * (1.0 + sc_ref[...])
    sh = sh_ref[...]

    def body(r, _):
        sl = pl.ds(pl.multiple_of(r * rows, rows), rows)
        dst_ref[sl, :] = (_rms(x_ref[sl, :]) * mul + sh).astype(dst_ref.dtype)
        return 0

    lax.fori_loop(0, tm // rows, body, 0, unroll=2)


def _proj_kernel(x_ref, g_ref, sh_ref, sc_ref, w_ref, o_ref, h_ref):
    @pl.when(pl.program_id(1) == 0)
    def _():
        _norm_mod_rows(x_ref, g_ref, sh_ref, sc_ref, h_ref)

    o_ref[...] = _dot(h_ref[...], w_ref[...]).astype(o_ref.dtype)


def norm_mod_proj(x2, g, shift, scale, w, seq, tm=1024, tn=2048):
    t, d = x2.shape
    n = w.shape[1]
    tm = min(tm, seq)
    per_b = seq // tm
    bsz = shift.shape[0]
    return pl.pallas_call(
        _proj_kernel,
        grid=(t // tm, n // tn),
        in_specs=[
            pl.BlockSpec((tm, d), lambda i, j: (i, 0)),
            pl.BlockSpec((1, d), lambda i, j: (0, 0)),
            pl.BlockSpec((None, 1, d), lambda i, j: (i // per_b, 0, 0)),
            pl.BlockSpec((None, 1, d), lambda i, j: (i // per_b, 0, 0)),
            pl.BlockSpec((d, tn), lambda i, j: (0, j)),
        ],
        out_specs=pl.BlockSpec((tm, tn), lambda i, j: (i, j)),
        out_shape=jax.ShapeDtypeStruct((t, n), BF16),
        scratch_shapes=[pltpu.VMEM((tm, d), BF16)],
        compiler_params=_cparams(("parallel", "arbitrary")),
        name="norm_mod_proj",
    )(x2, g.reshape(1, d), shift.reshape(bsz, 1, d), scale.reshape(bsz, 1, d), w)


def _rope_tables(positions, dim):
    inv_freq = ROPE_THETA ** (-jnp.arange(0, dim, 2, dtype=F32) / dim)
    ang = positions.astype(F32)[:, None] * inv_freq[None, :]
    cos, sin = jnp.cos(ang), jnp.sin(ang)
    pad = jnp.zeros((positions.shape[0], LANES - dim), F32)
    return (jnp.concatenate([cos, cos, pad], axis=-1),
            jnp.concatenate([-sin, sin, pad], axis=-1))


def _rope_full(z, cos2, sin_s):
    return z * cos2 + pltpu.roll(z, 64, 1) * sin_s


def _rope_64(z, cos2, sin_s):
    lane = lax.broadcasted_iota(jnp.int32, z.shape, 1)
    partner = jnp.where(lane < 32, pltpu.roll(z, 96, 1), pltpu.roll(z, 32, 1))
    return z * cos2 + partner * sin_s


def _mla_prep_kernel(p_ref, qg_ref, kvg_ref, wuq_ref, wukv_ref, qhg_ref, khg_ref,
                     cos_ref, sin_ref, q_ref, k_ref, v_ref):
    p = p_ref[...].astype(F32)
    c_q = p[:, COL_CQ:COL_CQ + MLA_Q_RANK]
    c_kv = p[:, COL_CKV:COL_CKV + MLA_KV_RANK]
    k_pe = p[:, COL_KPE:COL_KPE + LANES]
    q = _dot((_rms(c_q) * qg_ref[...]).astype(BF16), wuq_ref[...])
    kv = _dot((_rms(c_kv) * kvg_ref[...]).astype(BF16), wukv_ref[...])
    cos2, sin_s = cos_ref[...], sin_ref[...]
    qhg, khg = qhg_ref[...], khg_ref[...]
    scale = MLA_QK_DIM ** -0.5
    pe_ss = jnp.sum(k_pe * k_pe, axis=-1, keepdims=True)
    k_rope = _rope_64(k_pe * khg[:, LANES:], cos2, sin_s)
    for h in range(GROUP_HEADS):
        qh = q[:, h * MLA_QK_PAD:(h + 1) * MLA_QK_PAD]
        qh = _rms(qh, MLA_QK_DIM) * qhg * scale
        q_ref[h, :, :LANES] = qh[:, :LANES].astype(q_ref.dtype)
        q_ref[h, :, LANES:] = _rope_64(qh[:, LANES:], cos2, sin_s).astype(q_ref.dtype)
        k_nope = kv[:, 2 * h * LANES:(2 * h + 1) * LANES]
        ss = (jnp.sum(k_nope * k_nope, axis=-1, keepdims=True) + pe_ss) * (1.0 / MLA_QK_DIM)
        r = lax.rsqrt(ss + NORM_EPS)
        k_ref[h, :, :LANES] = (k_nope * r * khg[:, :LANES]).astype(k_ref.dtype)
        k_ref[h, :, LANES:] = (k_rope * r).astype(k_ref.dtype)
        v_ref[h] = kv[:, (2 * h + 1) * LANES:(2 * h + 2) * LANES].astype(v_ref.dtype)


def mla_prep(proj3, q_norm_g, kv_norm_g, w_uq_pad, w_ukv, q_head_g_pad, k_head_g_pad,
             cos_pe, sin_pe, tm=512):
    bsz, seq, _ = proj3.shape
    h = GROUP_HEADS
    n_in = COL_KPE + LANES
    const = lambda shape: pl.BlockSpec(shape, lambda b, i: (0,) * len(shape))
    return pl.pallas_call(
        _mla_prep_kernel,
        grid=(bsz, seq // tm),
        in_specs=[
            pl.BlockSpec((None, tm, n_in), lambda b, i: (b, i, 0)),
            const((1, MLA_Q_RANK)), const((1, MLA_KV_RANK)),
            const((MLA_Q_RANK, h * MLA_QK_PAD)), const((MLA_KV_RANK, h * 2 * LANES)),
            const((1, MLA_QK_PAD)), const((1, MLA_QK_PAD)),
            pl.BlockSpec((tm, LANES), lambda b, i: (i, 0)),
            pl.BlockSpec((tm, LANES), lambda b, i: (i, 0)),
        ],
        out_specs=[
            pl.BlockSpec((None, h, tm, MLA_QK_PAD), lambda b, i: (b, 0, i, 0)),
            pl.BlockSpec((None, h, tm, MLA_QK_PAD), lambda b, i: (b, 0, i, 0)),
            pl.BlockSpec((None, h, tm, LANES), lambda b, i: (b, 0, i, 0)),
        ],
        out_shape=[
            jax.ShapeDtypeStruct((bsz, h, seq, MLA_QK_PAD), BF16),
            jax.ShapeDtypeStruct((bsz, h, seq, MLA_QK_PAD), BF16),
            jax.ShapeDtypeStruct((bsz, h, seq, LANES), BF16),
        ],
        compiler_params=_cparams(("parallel", "parallel")),
        name="mla_prep",
    )(proj3, q_norm_g.reshape(1, -1), kv_norm_g.reshape(1, -1), w_uq_pad, w_ukv,
      q_head_g_pad, k_head_g_pad, cos_pe, sin_pe)


def _moba_gates(q_ref, kmean_ref):
    km = kmean_ref[...]
    nb = km.shape[0]
    hi = km.astype(BF16).astype(F32)
    mid = (km - hi).astype(BF16).astype(F32)
    lo = (km - hi - mid).astype(BF16).astype(F32)
    pieces = jnp.concatenate([hi, mid, lo, jnp.zeros_like(km)], axis=0).astype(BF16)
    r = _dot_nt(pieces, q_ref[...])
    return r[:nb] + r[nb:2 * nb] + r[2 * nb:3 * nb]


def _moba_block_choice(gate, n_past):
    blk = lax.broadcasted_iota(jnp.int32, gate.shape, 0)
    rank = jnp.zeros(gate.shape, jnp.int32)
    for o in range(n_past):
        g_o = gate[o:o + 1, :]
        tie = jnp.where(o < blk, 1, 0)
        rank = rank + jnp.where(g_o > gate, 1, jnp.where(g_o == gate, tie, 0))
    return jnp.where(rank < MOBA_TOPK, 1.0, 0.0)


def _attend(q_ref, k_ref, v_ref, kmean_ref, o_ref, t=ATTN_BLOCK):
    moba = kmean_ref is not None
    if moba:
        gates = _moba_gates(q_ref, kmean_ref)
    seq = q_ref.shape[0]
    v_t = jnp.transpose(v_ref[...].astype(F32)).astype(BF16)
    key = lax.broadcasted_iota(jnp.int32, (t, t), 0)
    qry = lax.broadcasted_iota(jnp.int32, (t, t), 1)
    causal = key <= qry
    for i in range(seq // t):
        own = slice(i * t, (i + 1) * t)
        q = q_ref[own, :]
        s_own = jnp.where(causal, _dot_nt(k_ref[own, :], q), NEG)
        m = jnp.max(s_own, axis=0, keepdims=True)
        s_past = []
        if i > 0:
            s_all = _dot_nt(k_ref[:i * t, :], q)
            if moba and i > MOBA_TOPK:
                sel = _moba_block_choice(gates[:, own], i)
                s_past = [jnp.where(sel[n:n + 1, :] > 0.5, s_all[n * t:(n + 1) * t, :], NEG)
                          for n in range(i)]
            else:
                s_past = [s_all]
            for s in s_past:
                m = jnp.maximum(m, jnp.max(s, axis=0, keepdims=True))
        p = jnp.exp(s_own - m)
        l = jnp.sum(p, axis=0, keepdims=True)
        acc_t = _dot(v_t[:, own], p.astype(BF16))
        start = 0
        for s in s_past:
            p = jnp.exp(s - m)
            l = l + jnp.sum(p, axis=0, keepdims=True)
            acc_t = acc_t + _dot(v_t[:, start:start + s.shape[0]], p.astype(BF16))
            start += s.shape[0]
        o_ref[own, :] = jnp.transpose(acc_t / l).astype(o_ref.dtype)


def _head_spec(seq, d):
    return pl.BlockSpec((None, None, seq, d), lambda b, hh: (b, hh, 0, 0))


def _proj_col_spec(seq, col):
    cb = col // LANES
    return pl.BlockSpec((None, seq, LANES), lambda b, hh: (b, 0, cb + hh))


def _mla_attn_kernel(q_ref, k_ref, v_ref, o_ref):
    _attend(q_ref, k_ref, v_ref, None, o_ref, t=MLA_ATTN_BLOCK)


def mla_attention(q, k, v):
    bsz, h, seq, dk = q.shape
    return pl.pallas_call(
        _mla_attn_kernel,
        grid=(bsz, h),
        in_specs=[_head_spec(seq, dk), _head_spec(seq, dk), _head_spec(seq, LANES)],
        out_specs=pl.BlockSpec((None, seq, LANES), lambda b, hh: (b, 0, hh)),
        out_shape=jax.ShapeDtypeStruct((bsz, seq, h * LANES), BF16),
        compiler_params=_cparams(("parallel", "parallel")),
        name="mla_attention",
    )(q, k, v)


def _moba_prep(q_ref, k_ref, qg_ref, kg_ref, cos_ref, sin_ref, qo_ref, ko_ref, km_ref):
    seq = q_ref.shape[0]
    scale = HEAD_DIM ** -0.5
    for blk in range(seq // MOBA_BLOCK):
        sl = slice(blk * MOBA_BLOCK, (blk + 1) * MOBA_BLOCK)
        cos2, sin_s = cos_ref[sl, :], sin_ref[sl, :]
        qn = _rms(q_ref[sl, :].astype(F32)) * qg_ref[...]
        qo_ref[sl, :] = (_rope_full(qn, cos2, sin_s) * scale).astype(qo_ref.dtype)
        kn = _rope_full(_rms(k_ref[sl, :].astype(F32)) * kg_ref[...], cos2, sin_s)
        ko_ref[sl, :] = kn.astype(ko_ref.dtype)
        km_ref[blk:blk + 1, :] = jnp.mean(kn, axis=0, keepdims=True)


def _moba_attn_kernel(q_ref, k_ref, v_ref, qg_ref, kg_ref, cos_ref, sin_ref, o_ref,
                      qn_ref, kn_ref, km_ref):
    _moba_prep(q_ref, k_ref, qg_ref, kg_ref, cos_ref, sin_ref, qn_ref, kn_ref, km_ref)
    _attend(qn_ref, kn_ref, v_ref, km_ref, o_ref)


def moba_attention(proj3, q_head_g, k_head_g, cos_full, sin_full):
    bsz, seq, _ = proj3.shape
    h = GROUP_HEADS
    const = lambda shape: pl.BlockSpec(shape, lambda b, hh: (0,) * len(shape))
    return pl.pallas_call(
        _moba_attn_kernel,
        grid=(bsz, h),
        in_specs=[
            _proj_col_spec(seq, COL_MOBA),
            _proj_col_spec(seq, COL_MOBA + GROUP_WIDTH),
            _proj_col_spec(seq, COL_MOBA + 2 * GROUP_WIDTH),
            const((1, LANES)), const((1, LANES)),
            const((seq, LANES)), const((seq, LANES)),
        ],
        out_specs=pl.BlockSpec((None, seq, LANES), lambda b, hh: (b, 0, hh)),
        out_shape=jax.ShapeDtypeStruct((bsz, seq, h * LANES), BF16),
        scratch_shapes=[pltpu.VMEM((seq, LANES), BF16), pltpu.VMEM((seq, LANES), BF16),
                        pltpu.VMEM((seq // MOBA_BLOCK, LANES), F32)],
        compiler_params=_cparams(("parallel", "parallel")),
        name="moba_attention",
    )(proj3, proj3, proj3, q_head_g.reshape(1, -1), k_head_g.reshape(1, -1), cos_full, sin_full)


SB_DEAD_TAIL = -104.0


def _sb_tile(q, k_blk, v_blk, tail, later, strict):
    z = _dot_nt(q, k_blk) * (HEAD_DIM ** -0.5)
    log_1m = -_softplus(z)
    if strict is not None:
        log_1m = jnp.where(strict, log_1m, 0.0)
    hi = log_1m.astype(BF16)
    lo = (log_1m - hi.astype(F32)).astype(BF16)
    t = hi.shape[0]
    both = _dot(jnp.concatenate([hi, lo], axis=0), later)
    suffix = both[:t] + both[t:]
    a = jnp.exp(z + log_1m + suffix + tail)
    if strict is not None:
        a = jnp.where(strict, a, 0.0)
    return _dot(a.astype(v_blk.dtype), v_blk), jnp.sum(log_1m, axis=-1, keepdims=True)


def _sb_kernel(q_ref, k_ref, v_ref, o_ref, acc_ref, tail_ref):
    seq = q_ref.shape[0]
    t = ATTN_BLOCK
    row = lax.broadcasted_iota(jnp.int32, (t, t), 0)
    col = lax.broadcasted_iota(jnp.int32, (t, t), 1)
    strict = col < row
    later = jnp.where(row > col, 1.0, 0.0).astype(BF16)
    blk = lambda n: slice(n * t, (n + 1) * t)

    for i in range(seq // t):
        q = q_ref[blk(i), :]
        acc, tail = _sb_tile(q, k_ref[blk(i), :], v_ref[blk(i), :], 0.0, later, strict)
        if i > 0:
            y, s = _sb_tile(q, k_ref[blk(i - 1), :], v_ref[blk(i - 1), :], tail, later, None)
            acc, tail = acc + y, tail + s
        acc_ref[blk(i), :] = acc
        tail_ref[blk(i), :] = tail

    for i in range(2, seq // t):
        alive = jnp.max(tail_ref[blk(i), :], axis=0, keepdims=True)[0, 0] > SB_DEAD_TAIL

        @pl.when(alive)
        def _():
            q = q_ref[blk(i), :]
            acc, tail = acc_ref[blk(i), :], tail_ref[blk(i), :]
            for n in range(i - 2, -1, -1):
                y, s = _sb_tile(q, k_ref[blk(n), :], v_ref[blk(n), :], tail, later, None)
                acc, tail = acc + y, tail + s
            acc_ref[blk(i), :] = acc

    o_ref[...] = acc_ref[...].astype(o_ref.dtype)


def sb_attention(proj3):
    bsz, seq, _ = proj3.shape
    return pl.pallas_call(
        _sb_kernel,
        grid=(bsz, GROUP_HEADS),
        in_specs=[
            _proj_col_spec(seq, COL_SB),
            _proj_col_spec(seq, COL_SB + GROUP_WIDTH),
            _proj_col_spec(seq, COL_SB + 2 * GROUP_WIDTH),
        ],
        out_specs=pl.BlockSpec((None, seq, LANES), lambda b, hh: (b, 0, hh)),
        out_shape=jax.ShapeDtypeStruct((bsz, seq, GROUP_WIDTH), BF16),
        scratch_shapes=[pltpu.VMEM((seq, LANES), F32), pltpu.VMEM((seq, 1), F32)],
        compiler_params=_cparams(("parallel", "parallel")),
        name="sb_attention",
    )(proj3, proj3, proj3)


def _ret_kernel(q_ref, k_ref, v_ref, g_ref, cos_ref, sin_ref, lg_ref, o_ref):
    seq = q_ref.shape[0]
    c = RET_CHUNK
    log_gamma = lg_ref[...]
    ri = lax.broadcasted_iota(jnp.int32, (c, c), 0).astype(F32)
    ci = lax.broadcasted_iota(jnp.int32, (c, c), 1).astype(F32)
    rel = ri - ci
    intra_decay = jnp.where(rel >= 0, jnp.exp(jnp.maximum(rel, 0.0) * log_gamma), 0.0)
    idx = lax.broadcasted_iota(jnp.int32, (c, 1), 0).astype(F32)
    query_decay = jnp.exp((idx + 1.0) * log_gamma)
    key_decay = jnp.exp((c - 1.0 - idx) * log_gamma)
    chunk_decay = jnp.exp(c * log_gamma)
    k_scale = HEAD_DIM ** -0.5

    state = jnp.zeros((HEAD_DIM, HEAD_DIM), F32)
    for n in range(seq // c):
        sl = slice(n * c, (n + 1) * c)
        cos2, sin_s = cos_ref[sl, :], sin_ref[sl, :]
        q = _rope_full(q_ref[sl, :].astype(F32), cos2, sin_s)
        k = _rope_full(k_ref[sl, :].astype(F32), cos2, sin_s) * k_scale
        vb = v_ref[sl, :]
        qb = q.astype(BF16)
        scores = _dot_nt(qb, k.astype(BF16)) * intra_decay
        y = _dot(scores.astype(BF16), vb)
        y = y + _dot(qb, state.astype(BF16)) * query_decay
        kd_t = jnp.transpose(k * key_decay).astype(BF16)
        state = state * chunk_decay + _dot(kd_t, vb)
        mu = jnp.mean(y, axis=-1, keepdims=True)
        yc = y - mu
        var = jnp.mean(yc * yc, axis=-1, keepdims=True)
        yn = yc * lax.rsqrt(var + NORM_EPS)
        g = g_ref[sl, :].astype(F32)
        o_ref[sl, :] = (g * _sigmoid(g) * yn).astype(o_ref.dtype)


def retention(proj3, cos_full, sin_full):
    bsz, seq, _ = proj3.shape
    h = GROUP_HEADS
    log_gamma = jnp.log(1.0 - 2.0 ** (-5.0 - jnp.arange(h, dtype=F32))).reshape(h, 1, 1)
    col = lambda k: pl.BlockSpec((None, seq, LANES),
                                 lambda b, hh: (b, 0, (COL_RET + k * GROUP_WIDTH) // LANES + hh))
    const = lambda shape: pl.BlockSpec(shape, lambda b, hh: (0,) * len(shape))
    return pl.pallas_call(
        _ret_kernel,
        grid=(bsz, h),
        in_specs=[col(0), col(1), col(2), col(3), const((seq, LANES)), const((seq, LANES)),
                  pl.BlockSpec((None, 1, 1), lambda b, hh: (hh, 0, 0))],
        out_specs=pl.BlockSpec((None, seq, LANES), lambda b, hh: (b, 0, hh)),
        out_shape=jax.ShapeDtypeStruct((bsz, seq, GROUP_WIDTH), BF16),
        compiler_params=_cparams(("parallel", "parallel")),
        name="retention",
    )(proj3, proj3, proj3, proj3, cos_full, sin_full, log_gamma)


def _out_proj_kernel(y0_ref, y1_ref, y2_ref, y3_ref, gg_ref, w_ref, x_ref, gate_ref, o_ref, h_ref):
    @pl.when(pl.program_id(1) == 0)
    def _():
        for grp, y_ref in enumerate((y0_ref, y1_ref, y2_ref, y3_ref)):
            yn = _rms(y_ref[...].astype(F32)) * gg_ref[grp:grp + 1, :]
            h_ref[:, grp * GROUP_WIDTH:(grp + 1) * GROUP_WIDTH] = yn.astype(h_ref.dtype)

    o_ref[...] = x_ref[...] + gate_ref[...] * _dot(h_ref[...], w_ref[...])


def out_proj_residual(ys, group_g, w_out, x2, gate, seq, tm=1024, tn=1024):
    t, d = x2.shape
    tm = min(tm, seq)
    per_b = seq // tm
    bsz = gate.shape[0]
    y_spec = pl.BlockSpec((tm, GROUP_WIDTH), lambda i, j: (i, 0))
    return pl.pallas_call(
        _out_proj_kernel,
        grid=(t // tm, d // tn),
        in_specs=[y_spec, y_spec, y_spec, y_spec,
                  pl.BlockSpec((4, GROUP_WIDTH), lambda i, j: (0, 0)),
                  pl.BlockSpec((4 * GROUP_WIDTH, tn), lambda i, j: (0, j)),
                  pl.BlockSpec((tm, tn), lambda i, j: (i, j)),
                  pl.BlockSpec((None, 1, tn), lambda i, j: (i // per_b, 0, j))],
        out_specs=pl.BlockSpec((tm, tn), lambda i, j: (i, j)),
        out_shape=jax.ShapeDtypeStruct((t, d), F32),
        scratch_shapes=[pltpu.VMEM((tm, 4 * GROUP_WIDTH), BF16)],
        compiler_params=_cparams(("parallel", "arbitrary")),
        name="out_proj_residual",
    )(*[y.reshape(t, GROUP_WIDTH) for y in ys], group_g, w_out, x2, gate.reshape(bsz, 1, d))


def _swiglu_act(h, w1_ref, w3_ref):
    a = _dot(h, w1_ref[...].astype(BF16))
    return (a * _sigmoid(a) * _dot(h, w3_ref[...].astype(BF16))).astype(BF16)


FFN_UP_STEPS = 11
FFN_DOWN_STEPS = 4


def _down_proj(a_ref, w2_ref):
    n, _, tf = a_ref.shape
    acc = _dot(a_ref[0], w2_ref[:tf, :])
    for c in range(1, n):
        acc = acc + _dot(a_ref[c], w2_ref[c * tf:(c + 1) * tf, :])
    return acc


def _ffn_kernel(x_ref, g_ref, sh_ref, sc_ref, w1_ref, w3_ref, w2_ref, xc_ref, gate_ref, o_ref,
                h_ref, a_ref):
    s = pl.program_id(1)

    @pl.when(s == 0)
    def _():
        _norm_mod_rows(x_ref, g_ref, sh_ref, sc_ref, h_ref)

    @pl.when(s < FFN_UP_STEPS)
    def _():
        a_ref[s] = _swiglu_act(h_ref[...], w1_ref, w3_ref)

    @pl.when(s >= FFN_UP_STEPS)
    def _():
        o_ref[...] = xc_ref[...] + gate_ref[...] * _down_proj(a_ref, w2_ref)


def _ffn_chunks(ffn, d):
    return ffn // FFN_UP_STEPS, d // FFN_DOWN_STEPS


def _up_idx(s):
    return jnp.minimum(s, FFN_UP_STEPS - 1)


def _down_idx(s):
    return jnp.maximum(s - FFN_UP_STEPS, 0)


def dense_ffn_residual(x2, g, shift, scale, gate, w1, w3, w2, seq, tm=1024):
    t, d = x2.shape
    ffn = w1.shape[1]
    tf, tn = _ffn_chunks(ffn, d)
    w1, w3, w2 = w1.astype(BF16), w3.astype(BF16), w2.astype(BF16)
    tm = min(tm, seq)
    per_b = seq // tm
    bsz = gate.shape[0]
    row = lambda: pl.BlockSpec((None, 1, d), lambda i, s: (i // per_b, 0, 0))
    return pl.pallas_call(
        _ffn_kernel,
        grid=(t // tm, FFN_UP_STEPS + FFN_DOWN_STEPS),
        in_specs=[pl.BlockSpec((tm, d), lambda i, s: (i, 0), pipeline_mode=pl.Buffered(1)),
                  pl.BlockSpec((1, d), lambda i, s: (0, 0)),
                  row(), row(),
                  pl.BlockSpec((d, tf), lambda i, s: (0, _up_idx(s))),
                  pl.BlockSpec((d, tf), lambda i, s: (0, _up_idx(s))),
                  pl.BlockSpec((ffn, tn), lambda i, s: (0, _down_idx(s))),
                  pl.BlockSpec((tm, tn), lambda i, s: (i, _down_idx(s))),
                  pl.BlockSpec((None, 1, tn), lambda i, s: (i // per_b, 0, _down_idx(s)))],
        out_specs=pl.BlockSpec((tm, tn), lambda i, s: (i, _down_idx(s))),
        out_shape=jax.ShapeDtypeStruct((t, d), F32),
        scratch_shapes=[pltpu.VMEM((tm, d), BF16), pltpu.VMEM((FFN_UP_STEPS, tm, tf), BF16)],
        compiler_params=_cparams(("parallel", "arbitrary")),
        name="dense_ffn_residual",
    )(x2, g.reshape(1, d), shift.reshape(bsz, 1, d), scale.reshape(bsz, 1, d),
      w1, w3, w2, x2, gate.reshape(bsz, 1, d))


def _router_kernel(x_ref, g_ref, sh_ref, sc_ref, rw_ref, h_ref, idx_ref, gates_ref):
    _norm_mod_rows(x_ref, g_ref, sh_ref, sc_ref, h_ref)
    hf, rw = h_ref[...], rw_ref[...]
    h_hi, w_hi = hf.astype(BF16), rw.astype(BF16)
    h_lo = (hf - h_hi.astype(F32)).astype(BF16)
    w_lo = (rw - w_hi.astype(F32)).astype(BF16)
    logits = _dot(h_hi, w_hi) + _dot(h_hi, w_lo) + _dot(h_lo, w_hi)
    lane = lax.broadcasted_iota(jnp.int32, logits.shape, 1)
    lane_f = lane.astype(F32)
    logits = jnp.where(lane < N_EXPERTS, logits, -jnp.inf)
    m0 = jnp.max(logits, axis=-1, keepdims=True)
    e0 = jnp.min(jnp.where(logits == m0, lane_f, float(LANES)), axis=-1, keepdims=True)
    rest = jnp.where(lane_f == e0, -jnp.inf, logits)
    m1 = jnp.max(rest, axis=-1, keepdims=True)
    e1 = jnp.min(jnp.where(rest == m1, lane_f, float(LANES)), axis=-1, keepdims=True)
    p1 = jnp.exp(m1 - m0)
    g0 = 1.0 / (1.0 + p1)
    idx_ref[...] = jnp.where(lane == 0, e0, jnp.where(lane == 1, e1, 0.0)).astype(jnp.int32)
    gates_ref[...] = jnp.where(lane == 0, g0, jnp.where(lane == 1, p1 * g0, 0.0))


def moe_router(x2, g, shift, scale, router_w, seq, tm=256):
    t, d = x2.shape
    tm = min(tm, seq)
    per_b = seq // tm
    bsz = shift.shape[0]
    rw = jnp.zeros((d, LANES), F32).at[:, :N_EXPERTS].set(router_w)
    row = lambda: pl.BlockSpec((None, 1, d), lambda i: (i // per_b, 0, 0))
    return pl.pallas_call(
        _router_kernel,
        grid=(t // tm,),
        in_specs=[pl.BlockSpec((tm, d), lambda i: (i, 0)),
                  pl.BlockSpec((1, d), lambda i: (0, 0)),
                  row(), row(),
                  pl.BlockSpec((d, LANES), lambda i: (0, 0))],
        out_specs=[pl.BlockSpec((tm, d), lambda i: (i, 0)),
                   pl.BlockSpec((tm, LANES), lambda i: (i, 0)),
                   pl.BlockSpec((tm, LANES), lambda i: (i, 0))],
        out_shape=[jax.ShapeDtypeStruct((t, d), F32),
                   jax.ShapeDtypeStruct((t, LANES), jnp.int32),
                   jax.ShapeDtypeStruct((t, LANES), F32)],
        compiler_params=_cparams(("parallel",)),
        name="moe_router",
    )(x2, g.reshape(1, d), shift.reshape(bsz, 1, d), scale.reshape(bsz, 1, d), rw)


GATHER_UNROLL = 8
MOE_TILE_ROWS = 1056


def _row_copy(src_ref, dst_ref, sem, src_row, dst_row):
    return pltpu.make_async_copy(src_ref.at[pl.ds(src_row, 1), :],
                                 dst_ref.at[pl.ds(dst_row, 1), :], sem)


def _start_row_gather(src_ref, dst_ref, sem, row_of, n_rows):
    def start(r, _):
        _row_copy(src_ref, dst_ref, sem, row_of(r), r).start()
        return 0

    lax.fori_loop(0, n_rows, start, 0, unroll=GATHER_UNROLL)


def _wait_row_gather(dst_ref, sem):
    pltpu.make_async_copy(dst_ref, dst_ref, sem).wait()


def _moe_ffn_kernel(tile_e_ref, tile_ok_ref, buf_t_ref, h_hbm, w1_ref, w3_ref, w2_ref, o_ref,
                    hf_ref, hb_ref, a_ref, sem):
    i = pl.program_id(0)
    f = pl.program_id(1)
    n_tiles = pl.num_programs(0)
    nf = FFN_UP_STEPS
    tm = hb_ref.shape[0]
    rows_per_step = tm // nf
    ok = tile_ok_ref[i] > 0
    last = (i == n_tiles - 1) & (f == nf - 1)
    nxt = jnp.minimum(i + 1, n_tiles - 1)
    started = (i == 0) | (tile_ok_ref[jnp.maximum(i - 1, 0)] > 0)

    @pl.when(f == 0)
    def _():
        @pl.when(i == 0)
        def _():
            _start_row_gather(h_hbm, hf_ref, sem, lambda r: buf_t_ref[r], tm)

        @pl.when(started)
        def _():
            _wait_row_gather(hf_ref, sem)

        @pl.when(ok)
        def _():
            hb_ref[...] = hf_ref[...].astype(hb_ref.dtype)

    @pl.when(ok & (f < nf))
    def _():
        base = f * rows_per_step
        for r in range(rows_per_step):
            _row_copy(h_hbm, hf_ref, sem, buf_t_ref[nxt * tm + base + r], base + r).start()
        a_ref[f] = _swiglu_act(hb_ref[...], w1_ref, w3_ref)

    @pl.when(ok & last)
    def _():
        _wait_row_gather(hf_ref, sem)

    @pl.when(f >= nf)
    def _():
        @pl.when(ok)
        def _():
            o_ref[...] = _down_proj(a_ref, w2_ref)

        @pl.when(jnp.logical_not(ok))
        def _():
            o_ref[...] = jnp.zeros_like(o_ref)


def moe_expert_ffn(h2, tile_e, tile_ok, buf_t, w1, w3, w2, tm):
    t, d = h2.shape
    ffn = w1.shape[2]
    tf, tn = _ffn_chunks(ffn, d)
    w2 = w2.astype(BF16)
    n_tiles = tile_e.shape[0]

    def up(i, s, ok):
        return jnp.where(ok[i] > 0, _up_idx(s), FFN_UP_STEPS - 1)

    def down(i, s, ok):
        return jnp.where(ok[i] > 0, _down_idx(s), 0)

    grid_spec = pltpu.PrefetchScalarGridSpec(
        num_scalar_prefetch=3,
        grid=(n_tiles, FFN_UP_STEPS + FFN_DOWN_STEPS),
        in_specs=[pl.BlockSpec(memory_space=pl.ANY),
                  pl.BlockSpec((None, d, tf), lambda i, s, te, ok, bt: (te[i], 0, up(i, s, ok))),
                  pl.BlockSpec((None, d, tf), lambda i, s, te, ok, bt: (te[i], 0, up(i, s, ok))),
                  pl.BlockSpec((None, ffn, tn), lambda i, s, te, ok, bt: (te[i], 0, down(i, s, ok)))],
        out_specs=pl.BlockSpec((tm, tn), lambda i, s, te, ok, bt: (i, _down_idx(s))),
        scratch_shapes=[pltpu.VMEM((tm, d), F32), pltpu.VMEM((tm, d), BF16),
                        pltpu.VMEM((FFN_UP_STEPS, tm, tf), BF16), pltpu.SemaphoreType.DMA(())],
    )
    return pl.pallas_call(
        _moe_ffn_kernel,
        grid_spec=grid_spec,
        out_shape=jax.ShapeDtypeStruct((n_tiles * tm, d), F32),
        compiler_params=_cparams(("arbitrary", "arbitrary"), VMEM_LIMIT_MOE),
        name="moe_expert_ffn",
    )(tile_e, tile_ok, buf_t, h2, w1, w3, w2)


def _moe_combine_kernel(d0_ref, d1_ref, y_hbm, x_ref, gate_ref, gates_ref, o_ref,
                        y0_ref, y1_ref, sems0, sems1):
    i = pl.program_id(0)
    tm = x_ref.shape[0]
    slot = i % 2

    def start_gather(tile, into):
        def start(r, _):
            _row_copy(y_hbm, y0_ref.at[into], sems0.at[into], d0_ref[tile * tm + r], r).start(priority=0)
            _row_copy(y_hbm, y1_ref.at[into], sems1.at[into], d1_ref[tile * tm + r], r).start(priority=1)
            return 0

        lax.fori_loop(0, tm, start, 0, unroll=GATHER_UNROLL)

    @pl.when(i == 0)
    def _():
        start_gather(0, 0)

    @pl.when(i + 1 < pl.num_programs(0))
    def _():
        start_gather(i + 1, 1 - slot)

    _wait_row_gather(y0_ref.at[slot], sems0.at[slot])
    _wait_row_gather(y1_ref.at[slot], sems1.at[slot])
    gates = gates_ref[...]
    y = gates[:, 0:1] * y0_ref[slot] + gates[:, 1:2] * y1_ref[slot]
    o_ref[...] = x_ref[...] + gate_ref[...] * y


def moe_combine_residual(dest0, dest1, y_buf, x2, gate, gates, seq, tm=256):
    t, d = x2.shape
    tm = min(tm, seq)
    per_b = seq // tm
    bsz = gate.shape[0]
    grid_spec = pltpu.PrefetchScalarGridSpec(
        num_scalar_prefetch=2,
        grid=(t // tm,),
        in_specs=[pl.BlockSpec(memory_space=pl.ANY),
                  pl.BlockSpec((tm, d), lambda i, d0, d1: (i, 0)),
                  pl.BlockSpec((None, 1, d), lambda i, d0, d1: (i // per_b, 0, 0)),
                  pl.BlockSpec((tm, LANES), lambda i, d0, d1: (i, 0))],
        out_specs=pl.BlockSpec((tm, d), lambda i, d0, d1: (i, 0)),
        scratch_shapes=[pltpu.VMEM((2, tm, d), F32), pltpu.VMEM((2, tm, d), F32),
                        pltpu.SemaphoreType.DMA((2,)), pltpu.SemaphoreType.DMA((2,))],
    )
    return pl.pallas_call(
        _moe_combine_kernel,
        grid_spec=grid_spec,
        out_shape=jax.ShapeDtypeStruct((t, d), F32),
        compiler_params=_cparams(("arbitrary",)),
        name="moe_combine_residual",
    )(dest0, dest1, y_buf, x2, gate.reshape(bsz, 1, d), gates)


def _moe_routing_tables(idx, tm):
    t = idx.shape[0]
    n_slots = t * MOE_TOPK
    n_tiles = n_slots // tm + N_EXPERTS
    flat_e = idx[:, :MOE_TOPK].reshape(-1)
    onehot = (flat_e[:, None] == jnp.arange(N_EXPERTS)[None, :]).astype(jnp.int32)
    counts = jnp.sum(onehot, axis=0)
    rank = jnp.sum((jnp.cumsum(onehot, axis=0) - onehot) * onehot, axis=1)
    padded = (counts + tm - 1) // tm * tm
    pad_ends = jnp.cumsum(padded)
    pad_starts = pad_ends - padded
    dest = (pad_starts[flat_e] + rank).astype(jnp.int32)
    buf_t = jnp.zeros((n_tiles * tm,), jnp.int32).at[dest].set(
        (jnp.arange(n_slots) // MOE_TOPK).astype(jnp.int32))
    tile_start = jnp.arange(n_tiles) * tm
    tile_e = jnp.minimum(jnp.searchsorted(pad_ends, tile_start, side="right"),
                         N_EXPERTS - 1).astype(jnp.int32)
    tile_ok = (tile_start < pad_ends[-1]).astype(jnp.int32)
    dest2 = dest.reshape(t, MOE_TOPK)
    return tile_e, tile_ok, buf_t, dest2[:, 0], dest2[:, 1]


def moe_ffn_residual(x2, g, shift, scale, gate, router_w, w1, w3, w2, seq, tile_rows=MOE_TILE_ROWS):
    h2, idx, gates = moe_router(x2, g, shift, scale, router_w, seq)
    tile_e, tile_ok, buf_t, dest0, dest1 = _moe_routing_tables(idx, tile_rows)
    y_buf = moe_expert_ffn(h2, tile_e, tile_ok, buf_t, w1, w3, w2, tile_rows)
    return moe_combine_residual(dest0, dest1, y_buf, x2, gate, gates, seq)


def _pad_w_in_kernel(w_ref, o_ref):
    rows = w_ref.shape[0]
    kpe_end = MLA_Q_RANK + MLA_KV_RANK + MLA_ROPE_DIM
    tail = PROJ_WIDTH - (COL_MOBA + 3 * GROUP_WIDTH)
    w = w_ref[...]
    o_ref[...] = jnp.concatenate([
        w[:, :kpe_end], jnp.zeros((rows, LANES - MLA_ROPE_DIM), F32),
        w[:, kpe_end:], jnp.zeros((rows, tail), F32)], axis=1).astype(o_ref.dtype)


def _pad_w_in(w_in, rows=256):
    d, n = w_in.shape
    return pl.pallas_call(
        _pad_w_in_kernel,
        grid=(d // rows,),
        in_specs=[pl.BlockSpec((rows, n), lambda i: (i, 0))],
        out_specs=pl.BlockSpec((rows, PROJ_WIDTH), lambda i: (i, 0)),
        out_shape=jax.ShapeDtypeStruct((d, PROJ_WIDTH), BF16),
        compiler_params=_cparams(("parallel",)),
        name="pad_w_in",
    )(w_in)


def _pad_heads_192(a):
    lead = a.shape[:-1]
    a = a.reshape(lead + (GROUP_HEADS, MLA_QK_DIM))
    a = jnp.pad(a, [(0, 0)] * len(lead) + [(0, 0), (0, MLA_QK_PAD - MLA_QK_DIM)])
    return a.reshape(lead + (GROUP_HEADS * MLA_QK_PAD,))


def kernel(x, c, positions, ada_w, ada_b, norm_mix_g, norm_ffn_g, w_in, mla_q_norm_g, mla_kv_norm_g, mla_w_uq, mla_w_ukv, mla_q_head_g, mla_k_head_g, moba_q_head_g, moba_k_head_g, group_norm_g, w_out, ffn_w1, ffn_w3, ffn_w2, router_w, moe_w1, moe_w3, moe_w2):
    bsz, seq, d = x.shape
    depth = ada_w.shape[0]
    cos_pe, sin_pe = _rope_tables(positions, MLA_ROPE_DIM)
    cos_full, sin_full = _rope_tables(positions, HEAD_DIM)
    mod = ada_modulation(c, ada_w, ada_b)
    x2 = x.reshape(bsz * seq, d)
    for l in range(depth):
        shift_m, scale_m, gate_m, shift_f, scale_f, gate_f = jnp.split(mod[l], 6, axis=-1)
        proj = norm_mod_proj(x2, norm_mix_g[l], shift_m, scale_m, _pad_w_in(w_in[l]), seq)
        proj3 = proj.reshape(bsz, seq, PROJ_WIDTH)
        pad_g = lambda g: jnp.pad(g, (0, MLA_QK_PAD - MLA_QK_DIM)).reshape(1, MLA_QK_PAD)
        q_mla, k_mla, v_mla = mla_prep(
            proj3, mla_q_norm_g[l], mla_kv_norm_g[l],
            _pad_heads_192(mla_w_uq[l]).astype(BF16), mla_w_ukv[l].astype(BF16),
            pad_g(mla_q_head_g[l]), pad_g(mla_k_head_g[l]), cos_pe, sin_pe)
        y_mla = mla_attention(q_mla, k_mla, v_mla)
        y_ret = retention(proj3, cos_full, sin_full)
        y_sb = sb_attention(proj3)
        y_moba = moba_attention(proj3, moba_q_head_g[l], moba_k_head_g[l], cos_full, sin_full)
        x2 = out_proj_residual((y_mla, y_ret, y_sb, y_moba), group_norm_g[l],
                               w_out[l].astype(BF16), x2, gate_m, seq)
        j = l // 2
        if l % 2 == 0:
            x2 = dense_ffn_residual(x2, norm_ffn_g[l], shift_f, scale_f, gate_f,
                                    ffn_w1[j], ffn_w3[j], ffn_w2[j], seq)
        else:
            x2 = moe_ffn_residual(x2, norm_ffn_g[l], shift_f, scale_f, gate_f, router_w[j],
                                  moe_w1[j], moe_w3[j], moe_w2[j], seq)
    return x2.reshape(bsz, seq, d)
```

```python
import jax
import jax.numpy as jnp
from jax import lax
from jax.experimental import pallas as pl
from jax.experimental.pallas import tpu as pltpu

F32 = jnp.float32
BF16 = jnp.bfloat16

HEAD_DIM = 128
GROUP_HEADS = 4
GROUP_WIDTH = 512
MLA_Q_RANK = 512
MLA_KV_RANK = 256
MLA_NOPE_DIM = 128
MLA_ROPE_DIM = 64
MLA_QK_DIM = MLA_NOPE_DIM + MLA_ROPE_DIM
MLA_QK_PAD = 256
RET_CHUNK = 128
MOBA_BLOCK = 256
MOBA_TOPK = 3
ROPE_THETA = 10000.0
NORM_EPS = 1e-6
NEG = -1e30
N_EXPERTS = 8
MOE_TOPK = 2

LANES = 128
ATTN_BLOCK = 256
MLA_ATTN_BLOCK = 1024
VMEM_LIMIT = 56 * 1024 * 1024
VMEM_LIMIT_MOE = 62 * 1024 * 1024

COL_CQ = 0
COL_CKV = 512
COL_KPE = 768
COL_RET = 896
COL_SB = COL_RET + 4 * GROUP_WIDTH
COL_MOBA = COL_SB + 3 * GROUP_WIDTH
PROJ_WIDTH = 6144


def _cparams(sem, vmem=VMEM_LIMIT):
    return pltpu.CompilerParams(dimension_semantics=sem, vmem_limit_bytes=vmem)


def _dot(a, b):
    return jnp.dot(a, b, preferred_element_type=F32)


def _dot_nt(a, b):
    return lax.dot_general(a, b, (((1,), (1,)), ((), ())), preferred_element_type=F32)


def _sigmoid(x):
    return 1.0 / (1.0 + jnp.exp(-x))


def _softplus(z):
    return jnp.maximum(z, 0.0) + jnp.log(1.0 + jnp.exp(-jnp.abs(z)))


def _rms(xf, width=None):
    width = xf.shape[-1] if width is None else width
    ss = jnp.sum(xf * xf, axis=-1, keepdims=True) * (1.0 / width)
    return xf * lax.rsqrt(ss + NORM_EPS)


def _ada_kernel(c_ref, w_ref, b_ref, o_ref):
    c = c_ref[...]
    cond = c * _sigmoid(c)
    o_ref[...] = _dot(cond.astype(BF16), w_ref[...].astype(BF16)) + b_ref[...]


def ada_modulation(c, ada_w, ada_b, tn=1024):
    depth, d, n = ada_w.shape
    b = c.shape[0]
    return pl.pallas_call(
        _ada_kernel,
        grid=(depth, n // tn),
        in_specs=[
            pl.BlockSpec((b, d), lambda l, j: (0, 0)),
            pl.BlockSpec((None, d, tn), lambda l, j: (l, 0, j)),
            pl.BlockSpec((None, 1, tn), lambda l, j: (l, 0, j)),
        ],
        out_specs=pl.BlockSpec((None, b, tn), lambda l, j: (l, 0, j)),
        out_shape=jax.ShapeDtypeStruct((depth, b, n), F32),
        compiler_params=_cparams(("parallel", "parallel")),
        name="ada_modulation",
    )(c, ada_w, ada_b.reshape(depth, 1, n))


def _norm_mod_rows(x_ref, g_ref, sh_ref, sc_ref, dst_ref, rows=32):
    tm = x_ref.shape[0]
    mul = g_ref[...] * (1.0 + sc_ref[...])
    sh = sh_ref[...]

    def body(r, _):
        sl = pl.ds(pl.multiple_of(r * rows, rows), rows)
        dst_ref[sl, :] = (_rms(x_ref[sl, :]) * mul + sh).astype(dst_ref.dtype)
        return 0

    lax.fori_loop(0, tm // rows, body, 0, unroll=2)


def _proj_kernel(x_ref, g_ref, sh_ref, sc_ref, w_ref, o_ref, h_ref):
    @pl.when(pl.program_id(1) == 0)
    def _():
        _norm_mod_rows(x_ref, g_ref, sh_ref, sc_ref, h_ref)

    o_ref[...] = _dot(h_ref[...], w_ref[...]).astype(o_ref.dtype)


def norm_mod_proj(x2, g, shift, scale, w, seq, tm=1024, tn=2048):
    t, d = x2.shape
    n = w.shape[1]
    tm = min(tm, seq)
    per_b = seq // tm
    bsz = shift.shape[0]
    return pl.pallas_call(
        _proj_kernel,
        grid=(t // tm, n // tn),
        in_specs=[
            pl.BlockSpec((tm, d), lambda i, j: (i, 0)),
            pl.BlockSpec((1, d), lambda i, j: (0, 0)),
            pl.BlockSpec((None, 1, d), lambda i, j: (i // per_b, 0, 0)),
            pl.BlockSpec((None, 1, d), lambda i, j: (i // per_b, 0, 0)),
            pl.BlockSpec((d, tn), lambda i, j: (0, j)),
        ],
        out_specs=pl.BlockSpec((tm, tn), lambda i, j: (i, j)),
        out_shape=jax.ShapeDtypeStruct((t, n), BF16),
        scratch_shapes=[pltpu.VMEM((tm, d), BF16)],
        compiler_params=_cparams(("parallel", "arbitrary")),
        name="norm_mod_proj",
    )(x2, g.reshape(1, d), shift.reshape(bsz, 1, d), scale.reshape(bsz, 1, d), w)


def _rope_tables(positions, dim):
    inv_freq = ROPE_THETA ** (-jnp.arange(0, dim, 2, dtype=F32) / dim)
    ang = positions.astype(F32)[:, None] * inv_freq[None, :]
    cos, sin = jnp.cos(ang), jnp.sin(ang)
    pad = jnp.zeros((positions.shape[0], LANES - dim), F32)
    return (jnp.concatenate([cos, cos, pad], axis=-1),
            jnp.concatenate([-sin, sin, pad], axis=-1))


def _rope_full(z, cos2, sin_s):
    return z * cos2 + pltpu.roll(z, 64, 1) * sin_s


def _rope_64(z, cos2, sin_s):
    lane = lax.broadcasted_iota(jnp.int32, z.shape, 1)
    partner = jnp.where(lane < 32, pltpu.roll(z, 96, 1), pltpu.roll(z, 32, 1))
    return z * cos2 + partner * sin_s


def _mla_prep_kernel(p_ref, qg_ref, kvg_ref, wuq_ref, wukv_ref, qhg_ref, khg_ref,
                     cos_ref, sin_ref, q_ref, k_ref, v_ref):
    p = p_ref[...].astype(F32)
    c_q = p[:, COL_CQ:COL_CQ + MLA_Q_RANK]
    c_kv = p[:, COL_CKV:COL_CKV + MLA_KV_RANK]
    k_pe = p[:, COL_KPE:COL_KPE + LANES]
    q = _dot((_rms(c_q) * qg_ref[...]).astype(BF16), wuq_ref[...])
    kv = _dot((_rms(c_kv) * kvg_ref[...]).astype(BF16), wukv_ref[...])
    cos2, sin_s = cos_ref[...], sin_ref[...]
    qhg, khg = qhg_ref[...], khg_ref[...]
    scale = MLA_QK_DIM ** -0.5
    pe_ss = jnp.sum(k_pe * k_pe, axis=-1, keepdims=True)
    k_rope = _rope_64(k_pe * khg[:, LANES:], cos2, sin_s)
    for h in range(GROUP_HEADS):
        qh = q[:, h * MLA_QK_PAD:(h + 1) * MLA_QK_PAD]
        qh = _rms(qh, MLA_QK_DIM) * qhg * scale
        q_ref[h, :, :LANES] = qh[:, :LANES].astype(q_ref.dtype)
        q_ref[h, :, LANES:] = _rope_64(qh[:, LANES:], cos2, sin_s).astype(q_ref.dtype)
        k_nope = kv[:, 2 * h * LANES:(2 * h + 1) * LANES]
        ss = (jnp.sum(k_nope * k_nope, axis=-1, keepdims=True) + pe_ss) * (1.0 / MLA_QK_DIM)
        r = lax.rsqrt(ss + NORM_EPS)
        k_ref[h, :, :LANES] = (k_nope * r * khg[:, :LANES]).astype(k_ref.dtype)
        k_ref[h, :, LANES:] = (k_rope * r).astype(k_ref.dtype)
        v_ref[h] = kv[:, (2 * h + 1) * LANES:(2 * h + 2) * LANES].astype(v_ref.dtype)


def mla_prep(proj3, q_norm_g, kv_norm_g, w_uq_pad, w_ukv, q_head_g_pad, k_head_g_pad,
             cos_pe, sin_pe, tm=512):
    bsz, seq, _ = proj3.shape
    h = GROUP_HEADS
    n_in = COL_KPE + LANES
    const = lambda shape: pl.BlockSpec(shape, lambda b, i: (0,) * len(shape))
    return pl.pallas_call(
        _mla_prep_kernel,
        grid=(bsz, seq // tm),
        in_specs=[
            pl.BlockSpec((None, tm, n_in), lambda b, i: (b, i, 0)),
            const((1, MLA_Q_RANK)), const((1, MLA_KV_RANK)),
            const((MLA_Q_RANK, h * MLA_QK_PAD)), const((MLA_KV_RANK, h * 2 * LANES)),
            const((1, MLA_QK_PAD)), const((1, MLA_QK_PAD)),
            pl.BlockSpec((tm, LANES), lambda b, i: (i, 0)),
            pl.BlockSpec((tm, LANES), lambda b, i: (i, 0)),
        ],
        out_specs=[
            pl.BlockSpec((None, h, tm, MLA_QK_PAD), lambda b, i: (b, 0, i, 0)),
            pl.BlockSpec((None, h, tm, MLA_QK_PAD), lambda b, i: (b, 0, i, 0)),
            pl.BlockSpec((None, h, tm, LANES), lambda b, i: (b, 0, i, 0)),
        ],
        out_shape=[
            jax.ShapeDtypeStruct((bsz, h, seq, MLA_QK_PAD), BF16),
            jax.ShapeDtypeStruct((bsz, h, seq, MLA_QK_PAD), BF16),
            jax.ShapeDtypeStruct((bsz, h, seq, LANES), BF16),
        ],
        compiler_params=_cparams(("parallel", "parallel")),
        name="mla_prep",
    )(proj3, q_norm_g.reshape(1, -1), kv_norm_g.reshape(1, -1), w_uq_pad, w_ukv,
      q_head_g_pad, k_head_g_pad, cos_pe, sin_pe)


def _moba_gates(q_ref, kmean_ref):
    km = kmean_ref[...]
    nb = km.shape[0]
    hi = km.astype(BF16).astype(F32)
    mid = (km - hi).astype(BF16).astype(F32)
    lo = (km - hi - mid).astype(BF16).astype(F32)
    pieces = jnp.concatenate([hi, mid, lo, jnp.zeros_like(km)], axis=0).astype(BF16)
    r = _dot_nt(pieces, q_ref[...])
    return r[:nb] + r[nb:2 * nb] + r[2 * nb:3 * nb]


def _moba_block_choice(gate, n_past):
    blk = lax.broadcasted_iota(jnp.int32, gate.shape, 0)
    rank = jnp.zeros(gate.shape, jnp.int32)
    for o in range(n_past):
        g_o = gate[o:o + 1, :]
        tie = jnp.where(o < blk, 1, 0)
        rank = rank + jnp.where(g_o > gate, 1, jnp.where(g_o == gate, tie, 0))
    return jnp.where(rank < MOBA_TOPK, 1.0, 0.0)


def _attend(q_ref, k_ref, v_ref, kmean_ref, o_ref, t=ATTN_BLOCK):
    moba = kmean_ref is not None
    if moba:
        gates = _moba_gates(q_ref, kmean_ref)
    seq = q_ref.shape[0]
    v_t = jnp.transpose(v_ref[...].astype(F32)).astype(BF16)
    key = lax.broadcasted_iota(jnp.int32, (t, t), 0)
    qry = lax.broadcasted_iota(jnp.int32, (t, t), 1)
    causal = key <= qry
    for i in range(seq // t):
        own = slice(i * t, (i + 1) * t)
        q = q_ref[own, :]
        s_own = jnp.where(causal, _dot_nt(k_ref[own, :], q), NEG)
        m = jnp.max(s_own, axis=0, keepdims=True)
        s_past = []
        if i > 0:
            s_all = _dot_nt(k_ref[:i * t, :], q)
            if moba and i > MOBA_TOPK:
                sel = _moba_block_choice(gates[:, own], i)
                s_past = [jnp.where(sel[n:n + 1, :] > 0.5, s_all[n * t:(n + 1) * t, :], NEG)
                          for n in range(i)]
            else:
                s_past = [s_all]
            for s in s_past:
                m = jnp.maximum(m, jnp.max(s, axis=0, keepdims=True))
        p = jnp.exp(s_own - m)
        l = jnp.sum(p, axis=0, keepdims=True)
        acc_t = _dot(v_t[:, own], p.astype(BF16))
        start = 0
        for s in s_past:
            p = jnp.exp(s - m)
            l = l + jnp.sum(p, axis=0, keepdims=True)
            acc_t = acc_t + _dot(v_t[:, start:start + s.shape[0]], p.astype(BF16))
            start += s.shape[0]
        o_ref[own, :] = jnp.transpose(acc_t / l).astype(o_ref.dtype)


def _head_spec(seq, d):
    return pl.BlockSpec((None, None, seq, d), lambda b, hh: (b, hh, 0, 0))


def _proj_col_spec(seq, col):
    cb = col // LANES
    return pl.BlockSpec((None, seq, LANES), lambda b, hh: (b, 0, cb + hh))


def _mla_attn_kernel(q_ref, k_ref, v_ref, o_ref):
    _attend(q_ref, k_ref, v_ref, None, o_ref, t=MLA_ATTN_BLOCK)


def mla_attention(q, k, v):
    bsz, h, seq, dk = q.shape
    return pl.pallas_call(
        _mla_attn_kernel,
        grid=(bsz, h),
        in_specs=[_head_spec(seq, dk), _head_spec(seq, dk), _head_spec(seq, LANES)],
        out_specs=pl.BlockSpec((None, seq, LANES), lambda b, hh: (b, 0, hh)),
        out_shape=jax.ShapeDtypeStruct((bsz, seq, h * LANES), BF16),
        compiler_params=_cparams(("parallel", "parallel")),
        name="mla_attention",
    )(q, k, v)


def _moba_prep(q_ref, k_ref, qg_ref, kg_ref, cos_ref, sin_ref, qo_ref, ko_ref, km_ref):
    seq = q_ref.shape[0]
    scale = HEAD_DIM ** -0.5
    for blk in range(seq // MOBA_BLOCK):
        sl = slice(blk * MOBA_BLOCK, (blk + 1) * MOBA_BLOCK)
        cos2, sin_s = cos_ref[sl, :], sin_ref[sl, :]
        qn = _rms(q_ref[sl, :].astype(F32)) * qg_ref[...]
        qo_ref[sl, :] = (_rope_full(qn, cos2, sin_s) * scale).astype(qo_ref.dtype)
        kn = _rope_full(_rms(k_ref[sl, :].astype(F32)) * kg_ref[...], cos2, sin_s)
        ko_ref[sl, :] = kn.astype(ko_ref.dtype)
        km_ref[blk:blk + 1, :] = jnp.mean(kn, axis=0, keepdims=True)


def _moba_attn_kernel(q_ref, k_ref, v_ref, qg_ref, kg_ref, cos_ref, sin_ref, o_ref,
                      qn_ref, kn_ref, km_ref):
    _moba_prep(q_ref, k_ref, qg_ref, kg_ref, cos_ref, sin_ref, qn_ref, kn_ref, km_ref)
    _attend(qn_ref, kn_ref, v_ref, km_ref, o_ref)


def moba_attention(proj3, q_head_g, k_head_g, cos_full, sin_full):
    bsz, seq, _ = proj3.shape
    h = GROUP_HEADS
    const = lambda shape: pl.BlockSpec(shape, lambda b, hh: (0,) * len(shape))
    return pl.pallas_call(
        _moba_attn_kernel,
        grid=(bsz, h),
        in_specs=[
            _proj_col_spec(seq, COL_MOBA),
            _proj_col_spec(seq, COL_MOBA + GROUP_WIDTH),
            _proj_col_spec(seq, COL_MOBA + 2 * GROUP_WIDTH),
            const((1, LANES)), const((1, LANES)),
            const((seq, LANES)), const((seq, LANES)),
        ],
        out_specs=pl.BlockSpec((None, seq, LANES), lambda b, hh: (b, 0, hh)),
        out_shape=jax.ShapeDtypeStruct((bsz, seq, h * LANES), BF16),
        scratch_shapes=[pltpu.VMEM((seq, LANES), BF16), pltpu.VMEM((seq, LANES), BF16),
                        pltpu.VMEM((seq // MOBA_BLOCK, LANES), F32)],
        compiler_params=_cparams(("parallel", "parallel")),
        name="moba_attention",
    )(proj3, proj3, proj3, q_head_g.reshape(1, -1), k_head_g.reshape(1, -1), cos_full, sin_full)


SB_DEAD_TAIL = -104.0


def _sb_tile(q, k_blk, v_blk, tail, later, strict):
    z = _dot_nt(q, k_blk) * (HEAD_DIM ** -0.5)
    log_1m = -_softplus(z)
    if strict is not None:
        log_1m = jnp.where(strict, log_1m, 0.0)
    hi = log_1m.astype(BF16)
    lo = (log_1m - hi.astype(F32)).astype(BF16)
    t = hi.shape[0]
    both = _dot(jnp.concatenate([hi, lo], axis=0), later)
    suffix = both[:t] + both[t:]
    a = jnp.exp(z + log_1m + suffix + tail)
    if strict is not None:
        a = jnp.where(strict, a, 0.0)
    return _dot(a.astype(v_blk.dtype), v_blk), jnp.sum(log_1m, axis=-1, keepdims=True)


def _sb_kernel(q_ref, k_ref, v_ref, o_ref, acc_ref, tail_ref):
    seq = q_ref.shape[0]
    t = ATTN_BLOCK
    row = lax.broadcasted_iota(jnp.int32, (t, t), 0)
    col = lax.broadcasted_iota(jnp.int32, (t, t), 1)
    strict = col < row
    later = jnp.where(row > col, 1.0, 0.0).astype(BF16)
    blk = lambda n: slice(n * t, (n + 1) * t)

    for i in range(seq // t):
        q = q_ref[blk(i), :]
        acc, tail = _sb_tile(q, k_ref[blk(i), :], v_ref[blk(i), :], 0.0, later, strict)
        if i > 0:
            y, s = _sb_tile(q, k_ref[blk(i - 1), :], v_ref[blk(i - 1), :], tail, later, None)
            acc, tail = acc + y, tail + s
        acc_ref[blk(i), :] = acc
        tail_ref[blk(i), :] = tail

    for i in range(2, seq // t):
        alive = jnp.max(tail_ref[blk(i), :], axis=0, keepdims=True)[0, 0] > SB_DEAD_TAIL

        @pl.when(alive)
        def _():
            q = q_ref[blk(i), :]
            acc, tail = acc_ref[blk(i), :], tail_ref[blk(i), :]
            for n in range(i - 2, -1, -1):
                y, s = _sb_tile(q, k_ref[blk(n), :], v_ref[blk(n), :], tail, later, None)
                acc, tail = acc + y, tail + s
            acc_ref[blk(i), :] = acc

    o_ref[...] = acc_ref[...].astype(o_ref.dtype)


def sb_attention(proj3):
    bsz, seq, _ = proj3.shape
    return pl.pallas_call(
        _sb_kernel,
        grid=(bsz, GROUP_HEADS),
        in_specs=[
            _proj_col_spec(seq, COL_SB),
            _proj_col_spec(seq, COL_SB + GROUP_WIDTH),
            _proj_col_spec(seq, COL_SB + 2 * GROUP_WIDTH),
        ],
        out_specs=pl.BlockSpec((None, seq, LANES), lambda b, hh: (b, 0, hh)),
        out_shape=jax.ShapeDtypeStruct((bsz, seq, GROUP_WIDTH), BF16),
        scratch_shapes=[pltpu.VMEM((seq, LANES), F32), pltpu.VMEM((seq, 1), F32)],
        compiler_params=_cparams(("parallel", "parallel")),
        name="sb_attention",
    )(proj3, proj3, proj3)


def _ret_kernel(q_ref, k_ref, v_ref, g_ref, cos_ref, sin_ref, lg_ref, o_ref):
    seq = q_ref.shape[0]
    c = RET_CHUNK
    log_gamma = lg_ref[...]
    ri = lax.broadcasted_iota(jnp.int32, (c, c), 0).astype(F32)
    ci = lax.broadcasted_iota(jnp.int32, (c, c), 1).astype(F32)
    rel = ri - ci
    intra_decay = jnp.where(rel >= 0, jnp.exp(jnp.maximum(rel, 0.0) * log_gamma), 0.0)
    idx = lax.broadcasted_iota(jnp.int32, (c, 1), 0).astype(F32)
    query_decay = jnp.exp((idx + 1.0) * log_gamma)
    key_decay = jnp.exp((c - 1.0 - idx) * log_gamma)
    chunk_decay = jnp.exp(c * log_gamma)
    k_scale = HEAD_DIM ** -0.5

    state = jnp.zeros((HEAD_DIM, HEAD_DIM), F32)
    for n in range(seq // c):
        sl = slice(n * c, (n + 1) * c)
        cos2, sin_s = cos_ref[sl, :], sin_ref[sl, :]
        q = _rope_full(q_ref[sl, :].astype(F32), cos2, sin_s)
        k = _rope_full(k_ref[sl, :].astype(F32), cos2, sin_s) * k_scale
        vb = v_ref[sl, :]
        qb = q.astype(BF16)
        scores = _dot_nt(qb, k.astype(BF16)) * intra_decay
        y = _dot(scores.astype(BF16), vb)
        y = y + _dot(qb, state.astype(BF16)) * query_decay
        kd_t = jnp.transpose(k * key_decay).astype(BF16)
        state = state * chunk_decay + _dot(kd_t, vb)
        mu = jnp.mean(y, axis=-1, keepdims=True)
        yc = y - mu
        var = jnp.mean(yc * yc, axis=-1, keepdims=True)
        yn = yc * lax.rsqrt(var + NORM_EPS)
        g = g_ref[sl, :].astype(F32)
        o_ref[sl, :] = (g * _sigmoid(g) * yn).astype(o_ref.dtype)


def retention(proj3, cos_full, sin_full):
    bsz, seq, _ = proj3.shape
    h = GROUP_HEADS
    log_gamma = jnp.log(1.0 - 2.0 ** (-5.0 - jnp.arange(h, dtype=F32))).reshape(h, 1, 1)
    col = lambda k: pl.BlockSpec((None, seq, LANES),
                                 lambda b, hh: (b, 0, (COL_RET + k * GROUP_WIDTH) // LANES + hh))
    const = lambda shape: pl.BlockSpec(shape, lambda b, hh: (0,) * len(shape))
    return pl.pallas_call(
        _ret_kernel,
        grid=(bsz, h),
        in_specs=[col(0), col(1), col(2), col(3), const((seq, LANES)), const((seq, LANES)),
                  pl.BlockSpec((None, 1, 1), lambda b, hh: (hh, 0, 0))],
        out_specs=pl.BlockSpec((None, seq, LANES), lambda b, hh: (b, 0, hh)),
        out_shape=jax.ShapeDtypeStruct((bsz, seq, GROUP_WIDTH), BF16),
        compiler_params=_cparams(("parallel", "parallel")),
        name="retention",
    )(proj3, proj3, proj3, proj3, cos_full, sin_full, log_gamma)


def _out_proj_kernel(y0_ref, y1_ref, y2_ref, y3_ref, gg_ref, w_ref, x_ref, gate_ref, o_ref, h_ref):
    @pl.when(pl.program_id(1) == 0)
    def _():
        for grp, y_ref in enumerate((y0_ref, y1_ref, y2_ref, y3_ref)):
            yn = _rms(y_ref[...].astype(F32)) * gg_ref[grp:grp + 1, :]
            h_ref[:, grp * GROUP_WIDTH:(grp + 1) * GROUP_WIDTH] = yn.astype(h_ref.dtype)

    o_ref[...] = x_ref[...] + gate_ref[...] * _dot(h_ref[...], w_ref[...])


def out_proj_residual(ys, group_g, w_out, x2, gate, seq, tm=1024, tn=1024):
    t, d = x2.shape
    tm = min(tm, seq)
    per_b = seq // tm
    bsz = gate.shape[0]
    y_spec = pl.BlockSpec((tm, GROUP_WIDTH), lambda i, j: (i, 0))
    return pl.pallas_call(
        _out_proj_kernel,
        grid=(t // tm, d // tn),
        in_specs=[y_spec, y_spec, y_spec, y_spec,
                  pl.BlockSpec((4, GROUP_WIDTH), lambda i, j: (0, 0)),
                  pl.BlockSpec((4 * GROUP_WIDTH, tn), lambda i, j: (0, j)),
                  pl.BlockSpec((tm, tn), lambda i, j: (i, j)),
                  pl.BlockSpec((None, 1, tn), lambda i, j: (i // per_b, 0, j))],
        out_specs=pl.BlockSpec((tm, tn), lambda i, j: (i, j)),
        out_shape=jax.ShapeDtypeStruct((t, d), F32),
        scratch_shapes=[pltpu.VMEM((tm, 4 * GROUP_WIDTH), BF16)],
        compiler_params=_cparams(("parallel", "arbitrary")),
        name="out_proj_residual",
    )(*[y.reshape(t, GROUP_WIDTH) for y in ys], group_g, w_out, x2, gate.reshape(bsz, 1, d))


def _swiglu_act(h, w1_ref, w3_ref):
    a = _dot(h, w1_ref[...].astype(BF16))
    return (a * _sigmoid(a) * _dot(h, w3_ref[...].astype(BF16))).astype(BF16)


FFN_UP_STEPS = 11
FFN_DOWN_STEPS = 4


def _down_proj(a_ref, w2_ref):
    n, _, tf = a_ref.shape
    acc = _dot(a_ref[0], w2_ref[:tf, :])
    for c in range(1, n):
        acc = acc + _dot(a_ref[c], w2_ref[c * tf:(c + 1) * tf, :])
    return acc


def _ffn_kernel(x_ref, g_ref, sh_ref, sc_ref, w1_ref, w3_ref, w2_ref, xc_ref, gate_ref, o_ref,
                h_ref, a_ref):
    s = pl.program_id(1)

    @pl.when(s == 0)
    def _():
        _norm_mod_rows(x_ref, g_ref, sh_ref, sc_ref, h_ref)

    @pl.when(s < FFN_UP_STEPS)
    def _():
        a_ref[s] = _swiglu_act(h_ref[...], w1_ref, w3_ref)

    @pl.when(s >= FFN_UP_STEPS)
    def _():
        o_ref[...] = xc_ref[...] + gate_ref[...] * _down_proj(a_ref, w2_ref)


def _ffn_chunks(ffn, d):
    return ffn // FFN_UP_STEPS, d // FFN_DOWN_STEPS


def _up_idx(s):
    return jnp.minimum(s, FFN_UP_STEPS - 1)


def _down_idx(s):
    return jnp.maximum(s - FFN_UP_STEPS, 0)


def dense_ffn_residual(x2, g, shift, scale, gate, w1, w3, w2, seq, tm=1024):
    t, d = x2.shape
    ffn = w1.shape[1]
    tf, tn = _ffn_chunks(ffn, d)
    w3, w2 = w3.astype(BF16), w2.astype(BF16)
    tm = min(tm, seq)
    per_b = seq // tm
    bsz = gate.shape[0]
    row = lambda: pl.BlockSpec((None, 1, d), lambda i, s: (i // per_b, 0, 0))
    return pl.pallas_call(
        _ffn_kernel,
        grid=(t // tm, FFN_UP_STEPS + FFN_DOWN_STEPS),
        in_specs=[pl.BlockSpec((tm, d), lambda i, s: (i, 0), pipeline_mode=pl.Buffered(1)),
                  pl.BlockSpec((1, d), lambda i, s: (0, 0)),
                  row(), row(),
                  pl.BlockSpec((d, tf), lambda i, s: (0, _up_idx(s))),
                  pl.BlockSpec((d, tf), lambda i, s: (0, _up_idx(s))),
                  pl.BlockSpec((ffn, tn), lambda i, s: (0, _down_idx(s))),
                  pl.BlockSpec((tm, tn), lambda i, s: (i, _down_idx(s))),
                  pl.BlockSpec((None, 1, tn), lambda i, s: (i // per_b, 0, _down_idx(s)))],
        out_specs=pl.BlockSpec((tm, tn), lambda i, s: (i, _down_idx(s))),
        out_shape=jax.ShapeDtypeStruct((t, d), F32),
        scratch_shapes=[pltpu.VMEM((tm, d), BF16), pltpu.VMEM((FFN_UP_STEPS, tm, tf), BF16)],
        compiler_params=_cparams(("parallel", "arbitrary"), VMEM_LIMIT_MOE),
        name="dense_ffn_residual",
    )(x2, g.reshape(1, d), shift.reshape(bsz, 1, d), scale.reshape(bsz, 1, d),
      w1, w3, w2, x2, gate.reshape(bsz, 1, d))


def _router_kernel(x_ref, g_ref, sh_ref, sc_ref, rw_ref, h_ref, idx_ref, gates_ref):
    _norm_mod_rows(x_ref, g_ref, sh_ref, sc_ref, h_ref)
    hf, rw = h_ref[...], rw_ref[...]
    h_hi, w_hi = hf.astype(BF16), rw.astype(BF16)
    h_lo = (hf - h_hi.astype(F32)).astype(BF16)
    w_lo = (rw - w_hi.astype(F32)).astype(BF16)
    logits = _dot(h_hi, w_hi) + _dot(h_hi, w_lo) + _dot(h_lo, w_hi)
    lane = lax.broadcasted_iota(jnp.int32, logits.shape, 1)
    lane_f = lane.astype(F32)
    logits = jnp.where(lane < N_EXPERTS, logits, -jnp.inf)
    m0 = jnp.max(logits, axis=-1, keepdims=True)
    e0 = jnp.min(jnp.where(logits == m0, lane_f, float(LANES)), axis=-1, keepdims=True)
    rest = jnp.where(lane_f == e0, -jnp.inf, logits)
    m1 = jnp.max(rest, axis=-1, keepdims=True)
    e1 = jnp.min(jnp.where(rest == m1, lane_f, float(LANES)), axis=-1, keepdims=True)
    p1 = jnp.exp(m1 - m0)
    g0 = 1.0 / (1.0 + p1)
    idx_ref[...] = jnp.where(lane == 0, e0, jnp.where(lane == 1, e1, 0.0)).astype(jnp.int32)
    gates_ref[...] = jnp.where(lane == 0, g0, jnp.where(lane == 1, p1 * g0, 0.0))


def moe_router(x2, g, shift, scale, router_w, seq, tm=256):
    t, d = x2.shape
    tm = min(tm, seq)
    per_b = seq // tm
    bsz = shift.shape[0]
    rw = jnp.zeros((d, LANES), F32).at[:, :N_EXPERTS].set(router_w)
    row = lambda: pl.BlockSpec((None, 1, d), lambda i: (i // per_b, 0, 0))
    return pl.pallas_call(
        _router_kernel,
        grid=(t // tm,),
        in_specs=[pl.BlockSpec((tm, d), lambda i: (i, 0)),
                  pl.BlockSpec((1, d), lambda i: (0, 0)),
                  row(), row(),
                  pl.BlockSpec((d, LANES), lambda i: (0, 0))],
        out_specs=[pl.BlockSpec((tm, d), lambda i: (i, 0)),
                   pl.BlockSpec((tm, LANES), lambda i: (i, 0)),
                   pl.BlockSpec((tm, LANES), lambda i: (i, 0))],
        out_shape=[jax.ShapeDtypeStruct((t, d), F32),
                   jax.ShapeDtypeStruct((t, LANES), jnp.int32),
                   jax.ShapeDtypeStruct((t, LANES), F32)],
        compiler_params=_cparams(("parallel",)),
        name="moe_router",
    )(x2, g.reshape(1, d), shift.reshape(bsz, 1, d), scale.reshape(bsz, 1, d), rw)


GATHER_UNROLL = 8
MOE_TILE_ROWS = 1056


def _row_copy(src_ref, dst_ref, sem, src_row, dst_row):
    return pltpu.make_async_copy(src_ref.at[pl.ds(src_row, 1), :],
                                 dst_ref.at[pl.ds(dst_row, 1), :], sem)


def _start_row_gather(src_ref, dst_ref, sem, row_of, n_rows):
    def start(r, _):
        _row_copy(src_ref, dst_ref, sem, row_of(r), r).start()
        return 0

    lax.fori_loop(0, n_rows, start, 0, unroll=GATHER_UNROLL)


def _wait_row_gather(dst_ref, sem):
    pltpu.make_async_copy(dst_ref, dst_ref, sem).wait()


def _moe_ffn_kernel(tile_e_ref, tile_ok_ref, buf_t_ref, h_hbm, w1_ref, w3_ref, w2_ref, o_ref,
                    hf_ref, hb_ref, a_ref, sem):
    i = pl.program_id(0)
    f = pl.program_id(1)
    n_tiles = pl.num_programs(0)
    nf = FFN_UP_STEPS
    tm = hb_ref.shape[0]
    rows_per_step = tm // nf
    ok = tile_ok_ref[i] > 0
    last = (i == n_tiles - 1) & (f == nf - 1)
    nxt = jnp.minimum(i + 1, n_tiles - 1)
    started = (i == 0) | (tile_ok_ref[jnp.maximum(i - 1, 0)] > 0)

    @pl.when(f == 0)
    def _():
        @pl.when(i == 0)
        def _():
            _start_row_gather(h_hbm, hf_ref, sem, lambda r: buf_t_ref[r], tm)

        @pl.when(started)
        def _():
            _wait_row_gather(hf_ref, sem)

        @pl.when(ok)
        def _():
            hb_ref[...] = hf_ref[...].astype(hb_ref.dtype)

    @pl.when(ok & (f < nf))
    def _():
        base = f * rows_per_step
        for r in range(rows_per_step):
            _row_copy(h_hbm, hf_ref, sem, buf_t_ref[nxt * tm + base + r], base + r).start()
        a_ref[f] = _swiglu_act(hb_ref[...], w1_ref, w3_ref)

    @pl.when(ok & last)
    def _():
        _wait_row_gather(hf_ref, sem)

    @pl.when(f >= nf)
    def _():
        @pl.when(ok)
        def _():
            o_ref[...] = _down_proj(a_ref, w2_ref)

        @pl.when(jnp.logical_not(ok))
        def _():
            o_ref[...] = jnp.zeros_like(o_ref)


def moe_expert_ffn(h2, tile_e, tile_ok, buf_t, w1, w3, w2, tm):
    t, d = h2.shape
    ffn = w1.shape[2]
    tf, tn = _ffn_chunks(ffn, d)
    w2 = w2.astype(BF16)
    n_tiles = tile_e.shape[0]

    def up(i, s, ok):
        return jnp.where(ok[i] > 0, _up_idx(s), FFN_UP_STEPS - 1)

    def down(i, s, ok):
        return jnp.where(ok[i] > 0, _down_idx(s), 0)

    grid_spec = pltpu.PrefetchScalarGridSpec(
        num_scalar_prefetch=3,
        grid=(n_tiles, FFN_UP_STEPS + FFN_DOWN_STEPS),
        in_specs=[pl.BlockSpec(memory_space=pl.ANY),
                  pl.BlockSpec((None, d, tf), lambda i, s, te, ok, bt: (te[i], 0, up(i, s, ok))),
                  pl.BlockSpec((None, d, tf), lambda i, s, te, ok, bt: (te[i], 0, up(i, s, ok))),
                  pl.BlockSpec((None, ffn, tn), lambda i, s, te, ok, bt: (te[i], 0, down(i, s, ok)))],
        out_specs=pl.BlockSpec((tm, tn), lambda i, s, te, ok, bt: (i, _down_idx(s))),
        scratch_shapes=[pltpu.VMEM((tm, d), F32), pltpu.VMEM((tm, d), BF16),
                        pltpu.VMEM((FFN_UP_STEPS, tm, tf), BF16), pltpu.SemaphoreType.DMA(())],
    )
    return pl.pallas_call(
        _moe_ffn_kernel,
        grid_spec=grid_spec,
        out_shape=jax.ShapeDtypeStruct((n_tiles * tm, d), F32),
        compiler_params=_cparams(("arbitrary", "arbitrary"), VMEM_LIMIT_MOE),
        name="moe_expert_ffn",
    )(tile_e, tile_ok, buf_t, h2, w1, w3, w2)


def _moe_combine_kernel(d0_ref, d1_ref, y_hbm, x_ref, gate_ref, gates_ref, o_ref,
                        y0_ref, y1_ref, sems0, sems1):
    i = pl.program_id(0)
    tm = x_ref.shape[0]
    slot = i % 2

    def start_gather(tile, into):
        base = tile * tm
        for r in range(tm):
            _row_copy(y_hbm, y0_ref.at[into], sems0.at[into], d0_ref[base + r], r).start(priority=0)
            _row_copy(y_hbm, y1_ref.at[into], sems1.at[into], d1_ref[base + r], r).start(priority=1)

    @pl.when(i == 0)
    def _():
        start_gather(0, 0)

    for s in (0, 1):
        @pl.when((slot == s) & (i + 1 < pl.num_programs(0)))
        def _():
            start_gather(i + 1, 1 - s)

    _wait_row_gather(y0_ref.at[slot], sems0.at[slot])
    _wait_row_gather(y1_ref.at[slot], sems1.at[slot])
    gates = gates_ref[...]
    y = gates[:, 0:1] * y0_ref[slot] + gates[:, 1:2] * y1_ref[slot]
    o_ref[...] = x_ref[...] + gate_ref[...] * y


def moe_combine_residual(dest0, dest1, y_buf, x2, gate, gates, seq, tm=256):
    t, d = x2.shape
    tm = min(tm, seq)
    per_b = seq // tm
    bsz = gate.shape[0]
    grid_spec = pltpu.PrefetchScalarGridSpec(
        num_scalar_prefetch=2,
        grid=(t // tm,),
        in_specs=[pl.BlockSpec(memory_space=pl.ANY),
                  pl.BlockSpec((tm, d), lambda i, d0, d1: (i, 0)),
                  pl.BlockSpec((None, 1, d), lambda i, d0, d1: (i // per_b, 0, 0)),
                  pl.BlockSpec((tm, LANES), lambda i, d0, d1: (i, 0))],
        out_specs=pl.BlockSpec((tm, d), lambda i, d0, d1: (i, 0)),
        scratch_shapes=[pltpu.VMEM((2, tm, d), F32), pltpu.VMEM((2, tm, d), F32),
                        pltpu.SemaphoreType.DMA((2,)), pltpu.SemaphoreType.DMA((2,))],
    )
    return pl.pallas_call(
        _moe_combine_kernel,
        grid_spec=grid_spec,
        out_shape=jax.ShapeDtypeStruct((t, d), F32),
        compiler_params=_cparams(("arbitrary",)),
        name="moe_combine_residual",
    )(dest0, dest1, y_buf, x2, gate.reshape(bsz, 1, d), gates)


def _moe_routing_tables(idx, tm):
    t = idx.shape[0]
    n_slots = t * MOE_TOPK
    n_tiles = n_slots // tm + N_EXPERTS
    flat_e = idx[:, :MOE_TOPK].reshape(-1)
    onehot = (flat_e[:, None] == jnp.arange(N_EXPERTS)[None, :]).astype(jnp.int32)
    counts = jnp.sum(onehot, axis=0)
    rank = jnp.sum((jnp.cumsum(onehot, axis=0) - onehot) * onehot, axis=1)
    padded = (counts + tm - 1) // tm * tm
    pad_ends = jnp.cumsum(padded)
    pad_starts = pad_ends - padded
    dest = (pad_starts[flat_e] + rank).astype(jnp.int32)
    buf_t = jnp.zeros((n_tiles * tm,), jnp.int32).at[dest].set(
        (jnp.arange(n_slots) // MOE_TOPK).astype(jnp.int32))
    tile_start = jnp.arange(n_tiles) * tm
    tile_e = jnp.minimum(jnp.searchsorted(pad_ends, tile_start, side="right"),
                         N_EXPERTS - 1).astype(jnp.int32)
    tile_ok = (tile_start < pad_ends[-1]).astype(jnp.int32)
    dest2 = dest.reshape(t, MOE_TOPK)
    return tile_e, tile_ok, buf_t, dest2[:, 0], dest2[:, 1]


def moe_ffn_residual(x2, g, shift, scale, gate, router_w, w1, w3, w2, seq, tile_rows=MOE_TILE_ROWS):
    h2, idx, gates = moe_router(x2, g, shift, scale, router_w, seq)
    tile_e, tile_ok, buf_t, dest0, dest1 = _moe_routing_tables(idx, tile_rows)
    y_buf = moe_expert_ffn(h2, tile_e, tile_ok, buf_t, w1, w3, w2, tile_rows)
    return moe_combine_residual(dest0, dest1, y_buf, x2, gate, gates, seq)


def _pad_w_in_kernel(w_ref, o_ref):
    rows = w_ref.shape[0]
    kpe_end = MLA_Q_RANK + MLA_KV_RANK + MLA_ROPE_DIM
    tail = PROJ_WIDTH - (COL_MOBA + 3 * GROUP_WIDTH)
    w = w_ref[...]
    o_ref[...] = jnp.concatenate([
        w[:, :kpe_end], jnp.zeros((rows, LANES - MLA_ROPE_DIM), F32),
        w[:, kpe_end:], jnp.zeros((rows, tail), F32)], axis=1).astype(o_ref.dtype)


def _pad_w_in(w_in, rows=256):
    d, n = w_in.shape
    return pl.pallas_call(
        _pad_w_in_kernel,
        grid=(d // rows,),
        in_specs=[pl.BlockSpec((rows, n), lambda i: (i, 0))],
        out_specs=pl.BlockSpec((rows, PROJ_WIDTH), lambda i: (i, 0)),
        out_shape=jax.ShapeDtypeStruct((d, PROJ_WIDTH), BF16),
        compiler_params=_cparams(("parallel",)),
        name="pad_w_in",
    )(w_in)


def _pad_heads_192(a):
    lead = a.shape[:-1]
    a = a.reshape(lead + (GROUP_HEADS, MLA_QK_DIM))
    a = jnp.pad(a, [(0, 0)] * len(lead) + [(0, 0), (0, MLA_QK_PAD - MLA_QK_DIM)])
    return a.reshape(lead + (GROUP_HEADS * MLA_QK_PAD,))


def kernel(x, c, positions, ada_w, ada_b, norm_mix_g, norm_ffn_g, w_in, mla_q_norm_g, mla_kv_norm_g, mla_w_uq, mla_w_ukv, mla_q_head_g, mla_k_head_g, moba_q_head_g, moba_k_head_g, group_norm_g, w_out, ffn_w1, ffn_w3, ffn_w2, router_w, moe_w1, moe_w3, moe_w2):
    bsz, seq, d = x.shape
    depth = ada_w.shape[0]
    cos_pe, sin_pe = _rope_tables(positions, MLA_ROPE_DIM)
    cos_full, sin_full = _rope_tables(positions, HEAD_DIM)
    mod = ada_modulation(c, ada_w, ada_b)
    x2 = x.reshape(bsz * seq, d)
    for l in range(depth):
        shift_m, scale_m, gate_m, shift_f, scale_f, gate_f = jnp.split(mod[l], 6, axis=-1)
        proj = norm_mod_proj(x2, norm_mix_g[l], shift_m, scale_m, _pad_w_in(w_in[l]), seq)
        proj3 = proj.reshape(bsz, seq, PROJ_WIDTH)
        pad_g = lambda g: jnp.pad(g, (0, MLA_QK_PAD - MLA_QK_DIM)).reshape(1, MLA_QK_PAD)
        q_mla, k_mla, v_mla = mla_prep(
            proj3, mla_q_norm_g[l], mla_kv_norm_g[l],
            _pad_heads_192(mla_w_uq[l]).astype(BF16), mla_w_ukv[l].astype(BF16),
            pad_g(mla_q_head_g[l]), pad_g(mla_k_head_g[l]), cos_pe, sin_pe)
        y_mla = mla_attention(q_mla, k_mla, v_mla)
        y_ret = retention(proj3, cos_full, sin_full)
        y_sb = sb_attention(proj3)
        y_moba = moba_attention(proj3, moba_q_head_g[l], moba_k_head_g[l], cos_full, sin_full)
        x2 = out_proj_residual((y_mla, y_ret, y_sb, y_moba), group_norm_g[l],
                               w_out[l].astype(BF16), x2, gate_m, seq)
        j = l // 2
        if l % 2 == 0:
            x2 = dense_ffn_residual(x2, norm_ffn_g[l], shift_f, scale_f, gate_f,
                                    ffn_w1[j], ffn_w3[j], ffn_w2[j], seq)
        else:
            x2 = moe_ffn_residual(x2, norm_ffn_g[l], shift_f, scale_f, gate_f, router_w[j],
                                  moe_w1[j], moe_w3[j], moe_w2[j], seq)
    return x2.reshape(bsz, seq, d)
```

```python
import jax
import jax.numpy as jnp
from jax import lax
from jax.experimental import pallas as pl
from jax.experimental.pallas import tpu as pltpu

F32 = jnp.float32
BF16 = jnp.bfloat16

HEAD_DIM = 128
GROUP_HEADS = 4
GROUP_WIDTH = 512
MLA_Q_RANK = 512
MLA_KV_RANK = 256
MLA_NOPE_DIM = 128
MLA_ROPE_DIM = 64
MLA_QK_DIM = MLA_NOPE_DIM + MLA_ROPE_DIM
MLA_QK_PAD = 256
RET_CHUNK = 128
MOBA_BLOCK = 256
MOBA_TOPK = 3
ROPE_THETA = 10000.0
NORM_EPS = 1e-6
NEG = -1e30
N_EXPERTS = 8
MOE_TOPK = 2

LANES = 128
ATTN_BLOCK = 256
MLA_ATTN_BLOCK = 1024
VMEM_LIMIT = 56 * 1024 * 1024
VMEM_LIMIT_MOE = 62 * 1024 * 1024

COL_CQ = 0
COL_CKV = 512
COL_KPE = 768
COL_RET = 896
COL_SB = COL_RET + 4 * GROUP_WIDTH
COL_MOBA = COL_SB + 3 * GROUP_WIDTH
PROJ_WIDTH = 6144


def _cparams(sem, vmem=VMEM_LIMIT):
    return pltpu.CompilerParams(dimension_semantics=sem, vmem_limit_bytes=vmem)


def _dot(a, b):
    return jnp.dot(a, b, preferred_element_type=F32)


def _dot_nt(a, b):
    return lax.dot_general(a, b, (((1,), (1,)), ((), ())), preferred_element_type=F32)


def _sigmoid(x):
    return 1.0 / (1.0 + jnp.exp(-x))


def _softplus(z):
    return jnp.maximum(z, 0.0) + jnp.log(1.0 + jnp.exp(-jnp.abs(z)))


def _rms(xf, width=None):
    width = xf.shape[-1] if width is None else width
    ss = jnp.sum(xf * xf, axis=-1, keepdims=True) * (1.0 / width)
    return xf * lax.rsqrt(ss + NORM_EPS)


def _ada_kernel(c_ref, w_ref, b_ref, o_ref):
    c = c_ref[...]
    cond = c * _sigmoid(c)
    o_ref[...] = _dot(cond.astype(BF16), w_ref[...].astype(BF16)) + b_ref[...]


def ada_modulation(c, ada_w, ada_b, tn=2048):
    depth, d, n = ada_w.shape
    b = c.shape[0]
    return pl.pallas_call(
        _ada_kernel,
        grid=(depth, n // tn),
        in_specs=[
            pl.BlockSpec((b, d), lambda l, j: (0, 0)),
            pl.BlockSpec((None, d, tn), lambda l, j: (l, 0, j)),
            pl.BlockSpec((None, 1, tn), lambda l, j: (l, 0, j)),
        ],
        out_specs=pl.BlockSpec((None, b, tn), lambda l, j: (l, 0, j)),
        out_shape=jax.ShapeDtypeStruct((depth, b, n), F32),
        compiler_params=_cparams(("parallel", "parallel")),
        name="ada_modulation",
    )(c, ada_w, ada_b.reshape(depth, 1, n))


def _norm_mod_rows(x_ref, g_ref, sh_ref, sc_ref, dst_ref, rows=32):
    tm = x_ref.shape[0]
    mul = g_ref[...] * (1.0 + sc_ref[...])
    sh = sh_ref[...]

    def body(r, _):
        sl = pl.ds(pl.multiple_of(r * rows, rows), rows)
        dst_ref[sl, :] = (_rms(x_ref[sl, :]) * mul + sh).astype(dst_ref.dtype)
        return 0

    lax.fori_loop(0, tm // rows, body, 0, unroll=2)


def _proj_kernel(x_ref, g_ref, sh_ref, sc_ref, w_ref, o_ref, h_ref):
    @pl.when(pl.program_id(1) == 0)
    def _():
        _norm_mod_rows(x_ref, g_ref, sh_ref, sc_ref, h_ref)

    o_ref[...] = _dot(h_ref[...], w_ref[...]).astype(o_ref.dtype)


def norm_mod_proj(x2, g, shift, scale, w, seq, tm=1024, tn=2048):
    t, d = x2.shape
    n = w.shape[1]
    tm = min(tm, seq)
    per_b = seq // tm
    bsz = shift.shape[0]
    return pl.pallas_call(
        _proj_kernel,
        grid=(t // tm, n // tn),
        in_specs=[
            pl.BlockSpec((tm, d), lambda i, j: (i, 0)),
            pl.BlockSpec((1, d), lambda i, j: (0, 0)),
            pl.BlockSpec((None, 1, d), lambda i, j: (i // per_b, 0, 0)),
            pl.BlockSpec((None, 1, d), lambda i, j: (i // per_b, 0, 0)),
            pl.BlockSpec((d, tn), lambda i, j: (0, j)),
        ],
        out_specs=pl.BlockSpec((tm, tn), lambda i, j: (i, j)),
        out_shape=jax.ShapeDtypeStruct((t, n), BF16),
        scratch_shapes=[pltpu.VMEM((tm, d), BF16)],
        compiler_params=_cparams(("parallel", "arbitrary")),
        name="norm_mod_proj",
    )(x2, g.reshape(1, d), shift.reshape(bsz, 1, d), scale.reshape(bsz, 1, d), w)


def _rope_tables(positions, dim):
    inv_freq = ROPE_THETA ** (-jnp.arange(0, dim, 2, dtype=F32) / dim)
    ang = positions.astype(F32)[:, None] * inv_freq[None, :]
    cos, sin = jnp.cos(ang), jnp.sin(ang)
    pad = jnp.zeros((positions.shape[0], LANES - dim), F32)
    return (jnp.concatenate([cos, cos, pad], axis=-1),
            jnp.concatenate([-sin, sin, pad], axis=-1))


def _rope_full(z, cos2, sin_s):
    return z * cos2 + pltpu.roll(z, 64, 1) * sin_s


def _rope_64(z, cos2, sin_s):
    lane = lax.broadcasted_iota(jnp.int32, z.shape, 1)
    partner = jnp.where(lane < 32, pltpu.roll(z, 96, 1), pltpu.roll(z, 32, 1))
    return z * cos2 + partner * sin_s


def _mla_prep_kernel(p_ref, qg_ref, kvg_ref, wuq_ref, wukv_ref, qhg_ref, khg_ref,
                     cos_ref, sin_ref, q_ref, k_ref, v_ref):
    p = p_ref[...].astype(F32)
    c_q = p[:, COL_CQ:COL_CQ + MLA_Q_RANK]
    c_kv = p[:, COL_CKV:COL_CKV + MLA_KV_RANK]
    k_pe = p[:, COL_KPE:COL_KPE + LANES]
    q = _dot((_rms(c_q) * qg_ref[...]).astype(BF16), wuq_ref[...])
    kv = _dot((_rms(c_kv) * kvg_ref[...]).astype(BF16), wukv_ref[...])
    cos2, sin_s = cos_ref[...], sin_ref[...]
    qhg, khg = qhg_ref[...], khg_ref[...]
    scale = MLA_QK_DIM ** -0.5
    pe_ss = jnp.sum(k_pe * k_pe, axis=-1, keepdims=True)
    k_rope = _rope_64(k_pe * khg[:, LANES:], cos2, sin_s)
    for h in range(GROUP_HEADS):
        qh = q[:, h * MLA_QK_PAD:(h + 1) * MLA_QK_PAD]
        qh = _rms(qh, MLA_QK_DIM) * qhg * scale
        q_ref[h, :, :LANES] = qh[:, :LANES].astype(q_ref.dtype)
        q_ref[h, :, LANES:] = _rope_64(qh[:, LANES:], cos2, sin_s).astype(q_ref.dtype)
        k_nope = kv[:, 2 * h * LANES:(2 * h + 1) * LANES]
        ss = (jnp.sum(k_nope * k_nope, axis=-1, keepdims=True) + pe_ss) * (1.0 / MLA_QK_DIM)
        r = lax.rsqrt(ss + NORM_EPS)
        k_ref[h, :, :LANES] = (k_nope * r * khg[:, :LANES]).astype(k_ref.dtype)
        k_ref[h, :, LANES:] = (k_rope * r).astype(k_ref.dtype)
        v_ref[h] = kv[:, (2 * h + 1) * LANES:(2 * h + 2) * LANES].astype(v_ref.dtype)


def mla_prep(proj3, q_norm_g, kv_norm_g, w_uq_pad, w_ukv, q_head_g_pad, k_head_g_pad,
             cos_pe, sin_pe, tm=512):
    bsz, seq, _ = proj3.shape
    h = GROUP_HEADS
    n_in = COL_KPE + LANES
    const = lambda shape: pl.BlockSpec(shape, lambda b, i: (0,) * len(shape))
    return pl.pallas_call(
        _mla_prep_kernel,
        grid=(bsz, seq // tm),
        in_specs=[
            pl.BlockSpec((None, tm, n_in), lambda b, i: (b, i, 0)),
            const((1, MLA_Q_RANK)), const((1, MLA_KV_RANK)),
            const((MLA_Q_RANK, h * MLA_QK_PAD)), const((MLA_KV_RANK, h * 2 * LANES)),
            const((1, MLA_QK_PAD)), const((1, MLA_QK_PAD)),
            pl.BlockSpec((tm, LANES), lambda b, i: (i, 0)),
            pl.BlockSpec((tm, LANES), lambda b, i: (i, 0)),
        ],
        out_specs=[
            pl.BlockSpec((None, h, tm, MLA_QK_PAD), lambda b, i: (b, 0, i, 0)),
            pl.BlockSpec((None, h, tm, MLA_QK_PAD), lambda b, i: (b, 0, i, 0)),
            pl.BlockSpec((None, h, tm, LANES), lambda b, i: (b, 0, i, 0)),
        ],
        out_shape=[
            jax.ShapeDtypeStruct((bsz, h, seq, MLA_QK_PAD), BF16),
            jax.ShapeDtypeStruct((bsz, h, seq, MLA_QK_PAD), BF16),
            jax.ShapeDtypeStruct((bsz, h, seq, LANES), BF16),
        ],
        compiler_params=_cparams(("parallel", "parallel")),
        name="mla_prep",
    )(proj3, q_norm_g.reshape(1, -1), kv_norm_g.reshape(1, -1), w_uq_pad, w_ukv,
      q_head_g_pad, k_head_g_pad, cos_pe, sin_pe)


def _moba_gates(q_ref, kmean_ref):
    km = kmean_ref[...]
    nb = km.shape[0]
    hi = km.astype(BF16).astype(F32)
    mid = (km - hi).astype(BF16).astype(F32)
    lo = (km - hi - mid).astype(BF16).astype(F32)
    pieces = jnp.concatenate([hi, mid, lo, jnp.zeros_like(km)], axis=0).astype(BF16)
    r = _dot_nt(pieces, q_ref[...])
    return r[:nb] + r[nb:2 * nb] + r[2 * nb:3 * nb]


def _moba_block_choice(gate, n_past):
    blk = lax.broadcasted_iota(jnp.int32, gate.shape, 0)
    rank = jnp.zeros(gate.shape, jnp.int32)
    for o in range(n_past):
        g_o = gate[o:o + 1, :]
        tie = jnp.where(o < blk, 1, 0)
        rank = rank + jnp.where(g_o > gate, 1, jnp.where(g_o == gate, tie, 0))
    return jnp.where(rank < MOBA_TOPK, 1.0, 0.0)


def _attend(q_ref, k_ref, v_ref, kmean_ref, o_ref, t=ATTN_BLOCK):
    moba = kmean_ref is not None
    if moba:
        gates = _moba_gates(q_ref, kmean_ref)
    seq = q_ref.shape[0]
    v_t = jnp.transpose(v_ref[...].astype(F32)).astype(BF16)
    key = lax.broadcasted_iota(jnp.int32, (t, t), 0)
    qry = lax.broadcasted_iota(jnp.int32, (t, t), 1)
    causal = key <= qry
    for i in range(seq // t):
        own = slice(i * t, (i + 1) * t)
        q = q_ref[own, :]
        s_own = jnp.where(causal, _dot_nt(k_ref[own, :], q), NEG)
        m = jnp.max(s_own, axis=0, keepdims=True)
        s_past = []
        if i > 0:
            s_all = _dot_nt(k_ref[:i * t, :], q)
            if moba and i > MOBA_TOPK:
                sel = _moba_block_choice(gates[:, own], i)
                s_past = [jnp.where(sel[n:n + 1, :] > 0.5, s_all[n * t:(n + 1) * t, :], NEG)
                          for n in range(i)]
            else:
                s_past = [s_all]
            for s in s_past:
                m = jnp.maximum(m, jnp.max(s, axis=0, keepdims=True))
        p = jnp.exp(s_own - m)
        l = jnp.sum(p, axis=0, keepdims=True)
        acc_t = _dot(v_t[:, own], p.astype(BF16))
        start = 0
        for s in s_past:
            p = jnp.exp(s - m)
            l = l + jnp.sum(p, axis=0, keepdims=True)
            acc_t = acc_t + _dot(v_t[:, start:start + s.shape[0]], p.astype(BF16))
            start += s.shape[0]
        o_ref[own, :] = jnp.transpose(acc_t / l).astype(o_ref.dtype)


def _head_spec(seq, d):
    return pl.BlockSpec((None, None, seq, d), lambda b, hh: (b, hh, 0, 0))


def _proj_col_spec(seq, col):
    cb = col // LANES
    return pl.BlockSpec((None, seq, LANES), lambda b, hh: (b, 0, cb + hh))


def _mla_attn_kernel(q_ref, k_ref, v_ref, o_ref):
    _attend(q_ref, k_ref, v_ref, None, o_ref, t=MLA_ATTN_BLOCK)


def mla_attention(q, k, v):
    bsz, h, seq, dk = q.shape
    return pl.pallas_call(
        _mla_attn_kernel,
        grid=(bsz, h),
        in_specs=[_head_spec(seq, dk), _head_spec(seq, dk), _head_spec(seq, LANES)],
        out_specs=pl.BlockSpec((None, seq, LANES), lambda b, hh: (b, 0, hh)),
        out_shape=jax.ShapeDtypeStruct((bsz, seq, h * LANES), BF16),
        compiler_params=_cparams(("parallel", "parallel")),
        name="mla_attention",
    )(q, k, v)


def _moba_prep(q_ref, k_ref, qg_ref, kg_ref, cos_ref, sin_ref, qo_ref, ko_ref, km_ref):
    seq = q_ref.shape[0]
    scale = HEAD_DIM ** -0.5
    for blk in range(seq // MOBA_BLOCK):
        sl = slice(blk * MOBA_BLOCK, (blk + 1) * MOBA_BLOCK)
        cos2, sin_s = cos_ref[sl, :], sin_ref[sl, :]
        qn = _rms(q_ref[sl, :].astype(F32)) * qg_ref[...]
        qo_ref[sl, :] = (_rope_full(qn, cos2, sin_s) * scale).astype(qo_ref.dtype)
        kn = _rope_full(_rms(k_ref[sl, :].astype(F32)) * kg_ref[...], cos2, sin_s)
        ko_ref[sl, :] = kn.astype(ko_ref.dtype)
        km_ref[blk:blk + 1, :] = jnp.mean(kn, axis=0, keepdims=True)


def _moba_attn_kernel(q_ref, k_ref, v_ref, qg_ref, kg_ref, cos_ref, sin_ref, o_ref,
                      qn_ref, kn_ref, km_ref):
    _moba_prep(q_ref, k_ref, qg_ref, kg_ref, cos_ref, sin_ref, qn_ref, kn_ref, km_ref)
    _attend(qn_ref, kn_ref, v_ref, km_ref, o_ref)


def moba_attention(proj3, q_head_g, k_head_g, cos_full, sin_full):
    bsz, seq, _ = proj3.shape
    h = GROUP_HEADS
    const = lambda shape: pl.BlockSpec(shape, lambda b, hh: (0,) * len(shape))
    return pl.pallas_call(
        _moba_attn_kernel,
        grid=(bsz, h),
        in_specs=[
            _proj_col_spec(seq, COL_MOBA),
            _proj_col_spec(seq, COL_MOBA + GROUP_WIDTH),
            _proj_col_spec(seq, COL_MOBA + 2 * GROUP_WIDTH),
            const((1, LANES)), const((1, LANES)),
            const((seq, LANES)), const((seq, LANES)),
        ],
        out_specs=pl.BlockSpec((None, seq, LANES), lambda b, hh: (b, 0, hh)),
        out_shape=jax.ShapeDtypeStruct((bsz, seq, h * LANES), BF16),
        scratch_shapes=[pltpu.VMEM((seq, LANES), BF16), pltpu.VMEM((seq, LANES), BF16),
                        pltpu.VMEM((seq // MOBA_BLOCK, LANES), F32)],
        compiler_params=_cparams(("parallel", "parallel")),
        name="moba_attention",
    )(proj3, proj3, proj3, q_head_g.reshape(1, -1), k_head_g.reshape(1, -1), cos_full, sin_full)


SB_DEAD_TAIL = -104.0


def _sb_tile(q, k_blk, v_blk, tail, later, strict):
    z = _dot_nt(q, k_blk) * (HEAD_DIM ** -0.5)
    log_1m = -_softplus(z)
    if strict is not None:
        log_1m = jnp.where(strict, log_1m, 0.0)
    hi = log_1m.astype(BF16)
    lo = (log_1m - hi.astype(F32)).astype(BF16)
    t = hi.shape[0]
    both = _dot(jnp.concatenate([hi, lo], axis=0), later)
    suffix = both[:t] + both[t:]
    a = jnp.exp(z + log_1m + suffix + tail)
    if strict is not None:
        a = jnp.where(strict, a, 0.0)
    return _dot(a.astype(v_blk.dtype), v_blk), jnp.sum(log_1m, axis=-1, keepdims=True)


def _sb_kernel(q_ref, k_ref, v_ref, o_ref, acc_ref, tail_ref):
    seq = q_ref.shape[0]
    t = ATTN_BLOCK
    row = lax.broadcasted_iota(jnp.int32, (t, t), 0)
    col = lax.broadcasted_iota(jnp.int32, (t, t), 1)
    strict = col < row
    later = jnp.where(row > col, 1.0, 0.0).astype(BF16)
    blk = lambda n: slice(n * t, (n + 1) * t)

    for i in range(seq // t):
        q = q_ref[blk(i), :]
        acc, tail = _sb_tile(q, k_ref[blk(i), :], v_ref[blk(i), :], 0.0, later, strict)
        if i > 0:
            y, s = _sb_tile(q, k_ref[blk(i - 1), :], v_ref[blk(i - 1), :], tail, later, None)
            acc, tail = acc + y, tail + s
        acc_ref[blk(i), :] = acc
        tail_ref[blk(i), :] = tail

    for i in range(2, seq // t):
        alive = jnp.max(tail_ref[blk(i), :], axis=0, keepdims=True)[0, 0] > SB_DEAD_TAIL

        @pl.when(alive)
        def _():
            q = q_ref[blk(i), :]
            acc, tail = acc_ref[blk(i), :], tail_ref[blk(i), :]
            for n in range(i - 2, -1, -1):
                y, s = _sb_tile(q, k_ref[blk(n), :], v_ref[blk(n), :], tail, later, None)
                acc, tail = acc + y, tail + s
            acc_ref[blk(i), :] = acc

    o_ref[...] = acc_ref[...].astype(o_ref.dtype)


def sb_attention(proj3):
    bsz, seq, _ = proj3.shape
    return pl.pallas_call(
        _sb_kernel,
        grid=(bsz, GROUP_HEADS),
        in_specs=[
            _proj_col_spec(seq, COL_SB),
            _proj_col_spec(seq, COL_SB + GROUP_WIDTH),
            _proj_col_spec(seq, COL_SB + 2 * GROUP_WIDTH),
        ],
        out_specs=pl.BlockSpec((None, seq, LANES), lambda b, hh: (b, 0, hh)),
        out_shape=jax.ShapeDtypeStruct((bsz, seq, GROUP_WIDTH), BF16),
        scratch_shapes=[pltpu.VMEM((seq, LANES), F32), pltpu.VMEM((seq, 1), F32)],
        compiler_params=_cparams(("parallel", "parallel")),
        name="sb_attention",
    )(proj3, proj3, proj3)


def _ret_kernel(q_ref, k_ref, v_ref, g_ref, cos_ref, sin_ref, lg_ref, o_ref):
    seq = q_ref.shape[0]
    c = RET_CHUNK
    log_gamma = lg_ref[...]
    ri = lax.broadcasted_iota(jnp.int32, (c, c), 0).astype(F32)
    ci = lax.broadcasted_iota(jnp.int32, (c, c), 1).astype(F32)
    rel = ri - ci
    intra_decay = jnp.where(rel >= 0, jnp.exp(jnp.maximum(rel, 0.0) * log_gamma), 0.0)
    idx = lax.broadcasted_iota(jnp.int32, (c, 1), 0).astype(F32)
    query_decay = jnp.exp((idx + 1.0) * log_gamma)
    key_decay = jnp.exp((c - 1.0 - idx) * log_gamma)
    chunk_decay = jnp.exp(c * log_gamma)
    k_scale = HEAD_DIM ** -0.5

    state = jnp.zeros((HEAD_DIM, HEAD_DIM), F32)
    for n in range(seq // c):
        sl = slice(n * c, (n + 1) * c)
        cos2, sin_s = cos_ref[sl, :], sin_ref[sl, :]
        q = _rope_full(q_ref[sl, :].astype(F32), cos2, sin_s)
        k = _rope_full(k_ref[sl, :].astype(F32), cos2, sin_s) * k_scale
        vb = v_ref[sl, :]
        qb = q.astype(BF16)
        scores = _dot_nt(qb, k.astype(BF16)) * intra_decay
        y = _dot(scores.astype(BF16), vb)
        y = y + _dot(qb, state.astype(BF16)) * query_decay
        kd_t = jnp.transpose(k * key_decay).astype(BF16)
        state = state * chunk_decay + _dot(kd_t, vb)
        mu = jnp.mean(y, axis=-1, keepdims=True)
        yc = y - mu
        var = jnp.mean(yc * yc, axis=-1, keepdims=True)
        yn = yc * lax.rsqrt(var + NORM_EPS)
        g = g_ref[sl, :].astype(F32)
        o_ref[sl, :] = (g * _sigmoid(g) * yn).astype(o_ref.dtype)


def retention(proj3, cos_full, sin_full):
    bsz, seq, _ = proj3.shape
    h = GROUP_HEADS
    log_gamma = jnp.log(1.0 - 2.0 ** (-5.0 - jnp.arange(h, dtype=F32))).reshape(h, 1, 1)
    col = lambda k: pl.BlockSpec((None, seq, LANES),
                                 lambda b, hh: (b, 0, (COL_RET + k * GROUP_WIDTH) // LANES + hh))
    const = lambda shape: pl.BlockSpec(shape, lambda b, hh: (0,) * len(shape))
    return pl.pallas_call(
        _ret_kernel,
        grid=(bsz, h),
        in_specs=[col(0), col(1), col(2), col(3), const((seq, LANES)), const((seq, LANES)),
                  pl.BlockSpec((None, 1, 1), lambda b, hh: (hh, 0, 0))],
        out_specs=pl.BlockSpec((None, seq, LANES), lambda b, hh: (b, 0, hh)),
        out_shape=jax.ShapeDtypeStruct((bsz, seq, GROUP_WIDTH), BF16),
        compiler_params=_cparams(("parallel", "parallel")),
        name="retention",
    )(proj3, proj3, proj3, proj3, cos_full, sin_full, log_gamma)


def _out_proj_kernel(y0_ref, y1_ref, y2_ref, y3_ref, gg_ref, w_ref, x_ref, gate_ref, o_ref, h_ref):
    @pl.when(pl.program_id(1) == 0)
    def _():
        for grp, y_ref in enumerate((y0_ref, y1_ref, y2_ref, y3_ref)):
            yn = _rms(y_ref[...].astype(F32)) * gg_ref[grp:grp + 1, :]
            h_ref[:, grp * GROUP_WIDTH:(grp + 1) * GROUP_WIDTH] = yn.astype(h_ref.dtype)

    o_ref[...] = x_ref[...] + gate_ref[...] * _dot(h_ref[...], w_ref[...])


def out_proj_residual(ys, group_g, w_out, x2, gate, seq, tm=1024, tn=1024):
    t, d = x2.shape
    tm = min(tm, seq)
    per_b = seq // tm
    bsz = gate.shape[0]
    y_spec = pl.BlockSpec((tm, GROUP_WIDTH), lambda i, j: (i, 0))
    return pl.pallas_call(
        _out_proj_kernel,
        grid=(t // tm, d // tn),
        in_specs=[y_spec, y_spec, y_spec, y_spec,
                  pl.BlockSpec((4, GROUP_WIDTH), lambda i, j: (0, 0)),
                  pl.BlockSpec((4 * GROUP_WIDTH, tn), lambda i, j: (0, j)),
                  pl.BlockSpec((tm, tn), lambda i, j: (i, j)),
                  pl.BlockSpec((None, 1, tn), lambda i, j: (i // per_b, 0, j))],
        out_specs=pl.BlockSpec((tm, tn), lambda i, j: (i, j)),
        out_shape=jax.ShapeDtypeStruct((t, d), F32),
        scratch_shapes=[pltpu.VMEM((tm, 4 * GROUP_WIDTH), BF16)],
        compiler_params=_cparams(("parallel", "arbitrary")),
        name="out_proj_residual",
    )(*[y.reshape(t, GROUP_WIDTH) for y in ys], group_g, w_out, x2, gate.reshape(bsz, 1, d))


def _swiglu_act(h, w1_ref, w3_ref):
    a = _dot(h, w1_ref[...].astype(BF16))
    return (a * _sigmoid(a) * _dot(h, w3_ref[...].astype(BF16))).astype(BF16)


FFN_UP_STEPS = 11
FFN_DOWN_STEPS = 4


def _down_proj(a_ref, w2_ref):
    n, _, tf = a_ref.shape
    acc = _dot(a_ref[0], w2_ref[:tf, :])
    for c in range(1, n):
        acc = acc + _dot(a_ref[c], w2_ref[c * tf:(c + 1) * tf, :])
    return acc


def _ffn_kernel(x_ref, g_ref, sh_ref, sc_ref, w1_ref, w3_ref, w2_ref, xc_ref, gate_ref, o_ref,
                h_ref, a_ref):
    s = pl.program_id(1)

    @pl.when(s == 0)
    def _():
        _norm_mod_rows(x_ref, g_ref, sh_ref, sc_ref, h_ref)

    @pl.when(s < FFN_UP_STEPS)
    def _():
        a_ref[s] = _swiglu_act(h_ref[...], w1_ref, w3_ref)

    @pl.when(s >= FFN_UP_STEPS)
    def _():
        o_ref[...] = xc_ref[...] + gate_ref[...] * _down_proj(a_ref, w2_ref)


def _ffn_chunks(ffn, d):
    return ffn // FFN_UP_STEPS, d // FFN_DOWN_STEPS


def _up_idx(s):
    return jnp.minimum(s, FFN_UP_STEPS - 1)


def _down_idx(s):
    return jnp.maximum(s - FFN_UP_STEPS, 0)


def dense_ffn_residual(x2, g, shift, scale, gate, w1, w3, w2, seq, tm=1024):
    t, d = x2.shape
    ffn = w1.shape[1]
    tf, tn = _ffn_chunks(ffn, d)
    w3, w2 = w3.astype(BF16), w2.astype(BF16)
    tm = min(tm, seq)
    per_b = seq // tm
    bsz = gate.shape[0]
    row = lambda: pl.BlockSpec((None, 1, d), lambda i, s: (i // per_b, 0, 0))
    return pl.pallas_call(
        _ffn_kernel,
        grid=(t // tm, FFN_UP_STEPS + FFN_DOWN_STEPS),
        in_specs=[pl.BlockSpec((tm, d), lambda i, s: (i, 0), pipeline_mode=pl.Buffered(1)),
                  pl.BlockSpec((1, d), lambda i, s: (0, 0)),
                  row(), row(),
                  pl.BlockSpec((d, tf), lambda i, s: (0, _up_idx(s))),
                  pl.BlockSpec((d, tf), lambda i, s: (0, _up_idx(s))),
                  pl.BlockSpec((ffn, tn), lambda i, s: (0, _down_idx(s))),
                  pl.BlockSpec((tm, tn), lambda i, s: (i, _down_idx(s))),
                  pl.BlockSpec((None, 1, tn), lambda i, s: (i // per_b, 0, _down_idx(s)))],
        out_specs=pl.BlockSpec((tm, tn), lambda i, s: (i, _down_idx(s))),
        out_shape=jax.ShapeDtypeStruct((t, d), F32),
        scratch_shapes=[pltpu.VMEM((tm, d), BF16), pltpu.VMEM((FFN_UP_STEPS, tm, tf), BF16)],
        compiler_params=_cparams(("parallel", "arbitrary"), VMEM_LIMIT_MOE),
        name="dense_ffn_residual",
    )(x2, g.reshape(1, d), shift.reshape(bsz, 1, d), scale.reshape(bsz, 1, d),
      w1, w3, w2, x2, gate.reshape(bsz, 1, d))


def _router_kernel(x_ref, g_ref, sh_ref, sc_ref, rw_ref, h_ref, idx_ref, gates_ref):
    _norm_mod_rows(x_ref, g_ref, sh_ref, sc_ref, h_ref)
    hf, rw = h_ref[...], rw_ref[...]
    h_hi, w_hi = hf.astype(BF16), rw.astype(BF16)
    h_lo = (hf - h_hi.astype(F32)).astype(BF16)
    w_lo = (rw - w_hi.astype(F32)).astype(BF16)
    logits = _dot(h_hi, w_hi) + _dot(h_hi, w_lo) + _dot(h_lo, w_hi)
    lane = lax.broadcasted_iota(jnp.int32, logits.shape, 1)
    lane_f = lane.astype(F32)
    logits = jnp.where(lane < N_EXPERTS, logits, -jnp.inf)
    m0 = jnp.max(logits, axis=-1, keepdims=True)
    e0 = jnp.min(jnp.where(logits == m0, lane_f, float(LANES)), axis=-1, keepdims=True)
    rest = jnp.where(lane_f == e0, -jnp.inf, logits)
    m1 = jnp.max(rest, axis=-1, keepdims=True)
    e1 = jnp.min(jnp.where(rest == m1, lane_f, float(LANES)), axis=-1, keepdims=True)
    p1 = jnp.exp(m1 - m0)
    g0 = 1.0 / (1.0 + p1)
    idx_ref[...] = jnp.where(lane == 0, e0, jnp.where(lane == 1, e1, 0.0)).astype(jnp.int32)
    gates_ref[...] = jnp.where(lane == 0, g0, jnp.where(lane == 1, p1 * g0, 0.0))


def moe_router(x2, g, shift, scale, router_w, seq, tm=512):
    t, d = x2.shape
    tm = min(tm, seq)
    per_b = seq // tm
    bsz = shift.shape[0]
    rw = jnp.zeros((d, LANES), F32).at[:, :N_EXPERTS].set(router_w)
    row = lambda: pl.BlockSpec((None, 1, d), lambda i: (i // per_b, 0, 0))
    return pl.pallas_call(
        _router_kernel,
        grid=(t // tm,),
        in_specs=[pl.BlockSpec((tm, d), lambda i: (i, 0)),
                  pl.BlockSpec((1, d), lambda i: (0, 0)),
                  row(), row(),
                  pl.BlockSpec((d, LANES), lambda i: (0, 0))],
        out_specs=[pl.BlockSpec((tm, d), lambda i: (i, 0)),
                   pl.BlockSpec((tm, LANES), lambda i: (i, 0)),
                   pl.BlockSpec((tm, LANES), lambda i: (i, 0))],
        out_shape=[jax.ShapeDtypeStruct((t, d), F32),
                   jax.ShapeDtypeStruct((t, LANES), jnp.int32),
                   jax.ShapeDtypeStruct((t, LANES), F32)],
        compiler_params=_cparams(("parallel",)),
        name="moe_router",
    )(x2, g.reshape(1, d), shift.reshape(bsz, 1, d), scale.reshape(bsz, 1, d), rw)


GATHER_UNROLL = 8
RANK_CHUNK = 256
MOE_TILE_ROWS = 1056


def _row_copy(src_ref, dst_ref, sem, src_row, dst_row):
    return pltpu.make_async_copy(src_ref.at[pl.ds(src_row, 1), :],
                                 dst_ref.at[pl.ds(dst_row, 1), :], sem)


def _start_row_gather(src_ref, dst_ref, sem, row_of, n_rows):
    def start(r, _):
        _row_copy(src_ref, dst_ref, sem, row_of(r), r).start()
        return 0

    lax.fori_loop(0, n_rows, start, 0, unroll=GATHER_UNROLL)


def _wait_row_gather(dst_ref, sem):
    pltpu.make_async_copy(dst_ref, dst_ref, sem).wait()


def _moe_ffn_kernel(tile_e_ref, tile_ok_ref, buf_t_ref, h_hbm, w1_ref, w3_ref, w2_ref, o_ref,
                    hf_ref, hb_ref, a_ref, sem):
    i = pl.program_id(0)
    f = pl.program_id(1)
    n_tiles = pl.num_programs(0)
    nf = FFN_UP_STEPS
    tm = hb_ref.shape[0]
    rows_per_step = tm // nf
    ok = tile_ok_ref[i] > 0
    last = (i == n_tiles - 1) & (f == nf - 1)
    nxt = jnp.minimum(i + 1, n_tiles - 1)
    started = (i == 0) | (tile_ok_ref[jnp.maximum(i - 1, 0)] > 0)

    @pl.when(f == 0)
    def _():
        @pl.when(i == 0)
        def _():
            _start_row_gather(h_hbm, hf_ref, sem, lambda r: buf_t_ref[r], tm)

        @pl.when(started)
        def _():
            _wait_row_gather(hf_ref, sem)

        @pl.when(ok)
        def _():
            hb_ref[...] = hf_ref[...].astype(hb_ref.dtype)

    @pl.when(ok & (f < nf))
    def _():
        base = f * rows_per_step
        for r in range(rows_per_step):
            _row_copy(h_hbm, hf_ref, sem, buf_t_ref[nxt * tm + base + r], base + r).start()
        a_ref[f] = _swiglu_act(hb_ref[...], w1_ref, w3_ref)

    @pl.when(ok & last)
    def _():
        _wait_row_gather(hf_ref, sem)

    @pl.when(f >= nf)
    def _():
        @pl.when(ok)
        def _():
            o_ref[...] = _down_proj(a_ref, w2_ref)

        @pl.when(jnp.logical_not(ok))
        def _():
            o_ref[...] = jnp.zeros_like(o_ref)


def moe_expert_ffn(h2, tile_e, tile_ok, buf_t, w1, w3, w2, tm):
    t, d = h2.shape
    ffn = w1.shape[2]
    tf, tn = _ffn_chunks(ffn, d)
    w2 = w2.astype(BF16)
    n_tiles = tile_e.shape[0]

    def up(i, s, ok):
        return jnp.where(ok[i] > 0, _up_idx(s), FFN_UP_STEPS - 1)

    def down(i, s, ok):
        return jnp.where(ok[i] > 0, _down_idx(s), 0)

    grid_spec = pltpu.PrefetchScalarGridSpec(
        num_scalar_prefetch=3,
        grid=(n_tiles, FFN_UP_STEPS + FFN_DOWN_STEPS),
        in_specs=[pl.BlockSpec(memory_space=pl.ANY),
                  pl.BlockSpec((None, d, tf), lambda i, s, te, ok, bt: (te[i], 0, up(i, s, ok))),
                  pl.BlockSpec((None, d, tf), lambda i, s, te, ok, bt: (te[i], 0, up(i, s, ok))),
                  pl.BlockSpec((None, ffn, tn), lambda i, s, te, ok, bt: (te[i], 0, down(i, s, ok)))],
        out_specs=pl.BlockSpec((tm, tn), lambda i, s, te, ok, bt: (i, _down_idx(s))),
        scratch_shapes=[pltpu.VMEM((tm, d), F32), pltpu.VMEM((tm, d), BF16),
                        pltpu.VMEM((FFN_UP_STEPS, tm, tf), BF16), pltpu.SemaphoreType.DMA(())],
    )
    return pl.pallas_call(
        _moe_ffn_kernel,
        grid_spec=grid_spec,
        out_shape=jax.ShapeDtypeStruct((n_tiles * tm, d), F32),
        compiler_params=_cparams(("arbitrary", "arbitrary"), VMEM_LIMIT_MOE),
        name="moe_expert_ffn",
    )(tile_e, tile_ok, buf_t, h2, w1, w3, w2)


def _moe_combine_kernel(d0_ref, d1_ref, y_hbm, x_ref, gate_ref, gates_ref, o_ref,
                        y0_ref, y1_ref, sems0, sems1):
    i = pl.program_id(0)
    tm = x_ref.shape[0]
    slot = i % 2

    def start_gather(tile, into):
        base = tile * tm
        for r in range(tm):
            _row_copy(y_hbm, y0_ref.at[into], sems0.at[into], d0_ref[base + r], r).start(priority=0)
            _row_copy(y_hbm, y1_ref.at[into], sems1.at[into], d1_ref[base + r], r).start(priority=1)

    @pl.when(i == 0)
    def _():
        start_gather(0, 0)

    for s in (0, 1):
        @pl.when((slot == s) & (i + 1 < pl.num_programs(0)))
        def _():
            start_gather(i + 1, 1 - s)

    _wait_row_gather(y0_ref.at[slot], sems0.at[slot])
    _wait_row_gather(y1_ref.at[slot], sems1.at[slot])
    gates = gates_ref[...]
    y = gates[:, 0:1] * y0_ref[slot] + gates[:, 1:2] * y1_ref[slot]
    o_ref[...] = x_ref[...] + gate_ref[...] * y


def moe_combine_residual(dest0, dest1, y_buf, x2, gate, gates, seq, tm=256):
    t, d = x2.shape
    tm = min(tm, seq)
    per_b = seq // tm
    bsz = gate.shape[0]
    grid_spec = pltpu.PrefetchScalarGridSpec(
        num_scalar_prefetch=2,
        grid=(t // tm,),
        in_specs=[pl.BlockSpec(memory_space=pl.ANY),
                  pl.BlockSpec((tm, d), lambda i, d0, d1: (i, 0)),
                  pl.BlockSpec((None, 1, d), lambda i, d0, d1: (i // per_b, 0, 0)),
                  pl.BlockSpec((tm, LANES), lambda i, d0, d1: (i, 0))],
        out_specs=pl.BlockSpec((tm, d), lambda i, d0, d1: (i, 0)),
        scratch_shapes=[pltpu.VMEM((2, tm, d), F32), pltpu.VMEM((2, tm, d), F32),
                        pltpu.SemaphoreType.DMA((2,)), pltpu.SemaphoreType.DMA((2,))],
    )
    return pl.pallas_call(
        _moe_combine_kernel,
        grid_spec=grid_spec,
        out_shape=jax.ShapeDtypeStruct((t, d), F32),
        compiler_params=_cparams(("arbitrary",)),
        name="moe_combine_residual",
    )(dest0, dest1, y_buf, x2, gate.reshape(bsz, 1, d), gates)


def _moe_routing_tables(idx, tm):
    t = idx.shape[0]
    n_slots = t * MOE_TOPK
    n_tiles = n_slots // tm + N_EXPERTS
    flat_e = idx[:, :MOE_TOPK].reshape(-1)
    onehot = (flat_e[:, None] == jnp.arange(N_EXPERTS)[None, :]).astype(jnp.int32)
    counts = jnp.sum(onehot, axis=0)
    chunk = RANK_CHUNK
    oh = onehot.astype(F32).reshape(n_slots // chunk, chunk, N_EXPERTS)
    earlier = jnp.tril(jnp.ones((chunk, chunk), F32), -1)
    within = jnp.einsum("ij,cjk->cik", earlier, oh)
    totals = jnp.sum(oh, axis=1)
    before = jnp.cumsum(totals, axis=0) - totals
    prefix = (within + before[:, None, :]).reshape(n_slots, N_EXPERTS).astype(jnp.int32)
    rank = jnp.sum(prefix * onehot, axis=1)
    padded = (counts + tm - 1) // tm * tm
    pad_ends = jnp.cumsum(padded)
    pad_starts = pad_ends - padded
    dest = (pad_starts[flat_e] + rank).astype(jnp.int32)
    buf_t = jnp.zeros((n_tiles * tm,), jnp.int32).at[dest].set(
        (jnp.arange(n_slots) // MOE_TOPK).astype(jnp.int32))
    tile_start = jnp.arange(n_tiles) * tm
    tile_e = jnp.minimum(jnp.searchsorted(pad_ends, tile_start, side="right"),
                         N_EXPERTS - 1).astype(jnp.int32)
    tile_ok = (tile_start < pad_ends[-1]).astype(jnp.int32)
    dest2 = dest.reshape(t, MOE_TOPK)
    return tile_e, tile_ok, buf_t, dest2[:, 0], dest2[:, 1]


def moe_ffn_residual(x2, g, shift, scale, gate, router_w, w1, w3, w2, seq, tile_rows=MOE_TILE_ROWS):
    h2, idx, gates = moe_router(x2, g, shift, scale, router_w, seq)
    tile_e, tile_ok, buf_t, dest0, dest1 = _moe_routing_tables(idx, tile_rows)
    y_buf = moe_expert_ffn(h2, tile_e, tile_ok, buf_t, w1, w3, w2, tile_rows)
    return moe_combine_residual(dest0, dest1, y_buf, x2, gate, gates, seq)


def _pad_w_in_kernel(w_ref, o_ref):
    rows = w_ref.shape[0]
    kpe_end = MLA_Q_RANK + MLA_KV_RANK + MLA_ROPE_DIM
    tail = PROJ_WIDTH - (COL_MOBA + 3 * GROUP_WIDTH)
    w = w_ref[...]
    o_ref[...] = jnp.concatenate([
        w[:, :kpe_end], jnp.zeros((rows, LANES - MLA_ROPE_DIM), F32),
        w[:, kpe_end:], jnp.zeros((rows, tail), F32)], axis=1).astype(o_ref.dtype)


def _pad_w_in(w_in, rows=256):
    d, n = w_in.shape
    return pl.pallas_call(
        _pad_w_in_kernel,
        grid=(d // rows,),
        in_specs=[pl.BlockSpec((rows, n), lambda i: (i, 0))],
        out_specs=pl.BlockSpec((rows, PROJ_WIDTH), lambda i: (i, 0)),
        out_shape=jax.ShapeDtypeStruct((d, PROJ_WIDTH), BF16),
        compiler_params=_cparams(("parallel",)),
        name="pad_w_in",
    )(w_in)


def _pad_heads_192(a):
    lead = a.shape[:-1]
    a = a.reshape(lead + (GROUP_HEADS, MLA_QK_DIM))
    a = jnp.pad(a, [(0, 0)] * len(lead) + [(0, 0), (0, MLA_QK_PAD - MLA_QK_DIM)])
    return a.reshape(lead + (GROUP_HEADS * MLA_QK_PAD,))


def kernel(x, c, positions, ada_w, ada_b, norm_mix_g, norm_ffn_g, w_in, mla_q_norm_g, mla_kv_norm_g, mla_w_uq, mla_w_ukv, mla_q_head_g, mla_k_head_g, moba_q_head_g, moba_k_head_g, group_norm_g, w_out, ffn_w1, ffn_w3, ffn_w2, router_w, moe_w1, moe_w3, moe_w2):
    bsz, seq, d = x.shape
    depth = ada_w.shape[0]
    cos_pe, sin_pe = _rope_tables(positions, MLA_ROPE_DIM)
    cos_full, sin_full = _rope_tables(positions, HEAD_DIM)
    mod = ada_modulation(c, ada_w, ada_b)
    x2 = x.reshape(bsz * seq, d)
    for l in range(depth):
        shift_m, scale_m, gate_m, shift_f, scale_f, gate_f = jnp.split(mod[l], 6, axis=-1)
        proj = norm_mod_proj(x2, norm_mix_g[l], shift_m, scale_m, _pad_w_in(w_in[l]), seq)
        proj3 = proj.reshape(bsz, seq, PROJ_WIDTH)
        pad_g = lambda g: jnp.pad(g, (0, MLA_QK_PAD - MLA_QK_DIM)).reshape(1, MLA_QK_PAD)
        q_mla, k_mla, v_mla = mla_prep(
            proj3, mla_q_norm_g[l], mla_kv_norm_g[l],
            _pad_heads_192(mla_w_uq[l]).astype(BF16), mla_w_ukv[l].astype(BF16),
            pad_g(mla_q_head_g[l]), pad_g(mla_k_head_g[l]), cos_pe, sin_pe)
        y_mla = mla_attention(q_mla, k_mla, v_mla)
        y_ret = retention(proj3, cos_full, sin_full)
        y_sb = sb_attention(proj3)
        y_moba = moba_attention(proj3, moba_q_head_g[l], moba_k_head_g[l], cos_full, sin_full)
        x2 = out_proj_residual((y_mla, y_ret, y_sb, y_moba), group_norm_g[l],
                               w_out[l].astype(BF16), x2, gate_m, seq)
        j = l // 2
        if l % 2 == 0:
            x2 = dense_ffn_residual(x2, norm_ffn_g[l], shift_f, scale_f, gate_f,
                                    ffn_w1[j], ffn_w3[j], ffn_w2[j], seq)
        else:
            x2 = moe_ffn_residual(x2, norm_ffn_g[l], shift_f, scale_f, gate_f, router_w[j],
                                  moe_w1[j], moe_w3[j], moe_w2[j], seq)
    return x2.reshape(bsz, seq, d)
```

```python
import jax
import jax.numpy as jnp
from jax import lax
from jax.experimental import pallas as pl
from jax.experimental.pallas import tpu as pltpu

F32 = jnp.float32
BF16 = jnp.bfloat16

HEAD_DIM = 128
GROUP_HEADS = 4
GROUP_WIDTH = 512
MLA_Q_RANK = 512
MLA_KV_RANK = 256
MLA_NOPE_DIM = 128
MLA_ROPE_DIM = 64
MLA_QK_DIM = MLA_NOPE_DIM + MLA_ROPE_DIM
MLA_QK_PAD = 256
RET_CHUNK = 128
MOBA_BLOCK = 256
MOBA_TOPK = 3
ROPE_THETA = 10000.0
NORM_EPS = 1e-6
NEG = -1e30
N_EXPERTS = 8
MOE_TOPK = 2

LANES = 128
ATTN_BLOCK = 256
MLA_ATTN_BLOCK = 1024
VMEM_LIMIT = 56 * 1024 * 1024
VMEM_LIMIT_MOE = 62 * 1024 * 1024

COL_CQ = 0
COL_CKV = 512
COL_KPE = 768
COL_RET = 896
COL_SB = COL_RET + 4 * GROUP_WIDTH
COL_MOBA = COL_SB + 3 * GROUP_WIDTH
PROJ_WIDTH = 6144


def _cparams(sem, vmem=VMEM_LIMIT):
    return pltpu.CompilerParams(dimension_semantics=sem, vmem_limit_bytes=vmem)


def _dot(a, b):
    return jnp.dot(a, b, preferred_element_type=F32)


def _dot_nt(a, b):
    return lax.dot_general(a, b, (((1,), (1,)), ((), ())), preferred_element_type=F32)


def _sigmoid(x):
    return 1.0 / (1.0 + jnp.exp(-x))


def _softplus(z):
    return jnp.maximum(z, 0.0) + jnp.log(1.0 + jnp.exp(-jnp.abs(z)))


def _rms(xf, width=None):
    width = xf.shape[-1] if width is None else width
    ss = jnp.sum(xf * xf, axis=-1, keepdims=True) * (1.0 / width)
    return xf * lax.rsqrt(ss + NORM_EPS)


def _ada_kernel(c_ref, w_ref, b_ref, o_ref):
    c = c_ref[...]
    cond = c * _sigmoid(c)
    o_ref[...] = _dot(cond.astype(BF16), w_ref[...].astype(BF16)) + b_ref[...]


def ada_modulation(c, ada_w, ada_b, tn=2048):
    depth, d, n = ada_w.shape
    b = c.shape[0]
    return pl.pallas_call(
        _ada_kernel,
        grid=(depth, n // tn),
        in_specs=[
            pl.BlockSpec((b, d), lambda l, j: (0, 0)),
            pl.BlockSpec((None, d, tn), lambda l, j: (l, 0, j)),
            pl.BlockSpec((None, 1, tn), lambda l, j: (l, 0, j)),
        ],
        out_specs=pl.BlockSpec((None, b, tn), lambda l, j: (l, 0, j)),
        out_shape=jax.ShapeDtypeStruct((depth, b, n), F32),
        compiler_params=_cparams(("parallel", "parallel")),
        name="ada_modulation",
    )(c, ada_w, ada_b.reshape(depth, 1, n))


def _norm_mod_rows(x_ref, g_ref, sh_ref, sc_ref, dst_ref, rows=32):
    tm = x_ref.shape[0]
    mul = g_ref[...] * (1.0 + sc_ref[...])
    sh = sh_ref[...]

    def body(r, _):
        sl = pl.ds(pl.multiple_of(r * rows, rows), rows)
        dst_ref[sl, :] = (_rms(x_ref[sl, :]) * mul + sh).astype(dst_ref.dtype)
        return 0

    lax.fori_loop(0, tm // rows, body, 0, unroll=2)


def _proj_kernel(x_ref, g_ref, sh_ref, sc_ref, w_ref, o_ref, h_ref):
    @pl.when(pl.program_id(1) == 0)
    def _():
        _norm_mod_rows(x_ref, g_ref, sh_ref, sc_ref, h_ref)

    o_ref[...] = _dot(h_ref[...], w_ref[...]).astype(o_ref.dtype)


def norm_mod_proj(x2, g, shift, scale, w, seq, tm=1024, tn=2048):
    t, d = x2.shape
    n = w.shape[1]
    tm = min(tm, seq)
    per_b = seq // tm
    bsz = shift.shape[0]
    return pl.pallas_call(
        _proj_kernel,
        grid=(t // tm, n // tn),
        in_specs=[
            pl.BlockSpec((tm, d), lambda i, j: (i, 0)),
            pl.BlockSpec((1, d), lambda i, j: (0, 0)),
            pl.BlockSpec((None, 1, d), lambda i, j: (i // per_b, 0, 0)),
            pl.BlockSpec((None, 1, d), lambda i, j: (i // per_b, 0, 0)),
            pl.BlockSpec((d, tn), lambda i, j: (0, j)),
        ],
        out_specs=pl.BlockSpec((tm, tn), lambda i, j: (i, j)),
        out_shape=jax.ShapeDtypeStruct((t, n), BF16),
        scratch_shapes=[pltpu.VMEM((tm, d), BF16)],
        compiler_params=_cparams(("parallel", "arbitrary")),
        name="norm_mod_proj",
    )(x2, g.reshape(1, d), shift.reshape(bsz, 1, d), scale.reshape(bsz, 1, d), w)


def _rope_tables(positions, dim):
    inv_freq = ROPE_THETA ** (-jnp.arange(0, dim, 2, dtype=F32) / dim)
    ang = positions.astype(F32)[:, None] * inv_freq[None, :]
    cos, sin = jnp.cos(ang), jnp.sin(ang)
    pad = jnp.zeros((positions.shape[0], LANES - dim), F32)
    return (jnp.concatenate([cos, cos, pad], axis=-1),
            jnp.concatenate([-sin, sin, pad], axis=-1))


def _rope_full(z, cos2, sin_s):
    return z * cos2 + pltpu.roll(z, 64, 1) * sin_s


def _rope_64(z, cos2, sin_s):
    lane = lax.broadcasted_iota(jnp.int32, z.shape, 1)
    partner = jnp.where(lane < 32, pltpu.roll(z, 96, 1), pltpu.roll(z, 32, 1))
    return z * cos2 + partner * sin_s


def _mla_prep_kernel(p_ref, qg_ref, kvg_ref, wuq_ref, wukv_ref, qhg_ref, khg_ref,
                     cos_ref, sin_ref, q_ref, k_ref, v_ref):
    p = p_ref[...].astype(F32)
    c_q = p[:, COL_CQ:COL_CQ + MLA_Q_RANK]
    c_kv = p[:, COL_CKV:COL_CKV + MLA_KV_RANK]
    k_pe = p[:, COL_KPE:COL_KPE + LANES]
    q = _dot((_rms(c_q) * qg_ref[...]).astype(BF16), wuq_ref[...])
    kv = _dot((_rms(c_kv) * kvg_ref[...]).astype(BF16), wukv_ref[...])
    cos2, sin_s = cos_ref[...], sin_ref[...]
    qhg, khg = qhg_ref[...], khg_ref[...]
    scale = MLA_QK_DIM ** -0.5
    pe_ss = jnp.sum(k_pe * k_pe, axis=-1, keepdims=True)
    k_rope = _rope_64(k_pe * khg[:, LANES:], cos2, sin_s)
    for h in range(GROUP_HEADS):
        qh = q[:, h * MLA_QK_PAD:(h + 1) * MLA_QK_PAD]
        qh = _rms(qh, MLA_QK_DIM) * qhg * scale
        q_ref[h, :, :LANES] = qh[:, :LANES].astype(q_ref.dtype)
        q_ref[h, :, LANES:] = _rope_64(qh[:, LANES:], cos2, sin_s).astype(q_ref.dtype)
        k_nope = kv[:, 2 * h * LANES:(2 * h + 1) * LANES]
        ss = (jnp.sum(k_nope * k_nope, axis=-1, keepdims=True) + pe_ss) * (1.0 / MLA_QK_DIM)
        r = lax.rsqrt(ss + NORM_EPS)
        k_ref[h, :, :LANES] = (k_nope * r * khg[:, :LANES]).astype(k_ref.dtype)
        k_ref[h, :, LANES:] = (k_rope * r).astype(k_ref.dtype)
        v_ref[h] = kv[:, (2 * h + 1) * LANES:(2 * h + 2) * LANES].astype(v_ref.dtype)


def mla_prep(proj3, q_norm_g, kv_norm_g, w_uq_pad, w_ukv, q_head_g_pad, k_head_g_pad,
             cos_pe, sin_pe, tm=512):
    bsz, seq, _ = proj3.shape
    h = GROUP_HEADS
    n_in = COL_KPE + LANES
    const = lambda shape: pl.BlockSpec(shape, lambda b, i: (0,) * len(shape))
    return pl.pallas_call(
        _mla_prep_kernel,
        grid=(bsz, seq // tm),
        in_specs=[
            pl.BlockSpec((None, tm, n_in), lambda b, i: (b, i, 0)),
            const((1, MLA_Q_RANK)), const((1, MLA_KV_RANK)),
            const((MLA_Q_RANK, h * MLA_QK_PAD)), const((MLA_KV_RANK, h * 2 * LANES)),
            const((1, MLA_QK_PAD)), const((1, MLA_QK_PAD)),
            pl.BlockSpec((tm, LANES), lambda b, i: (i, 0)),
            pl.BlockSpec((tm, LANES), lambda b, i: (i, 0)),
        ],
        out_specs=[
            pl.BlockSpec((None, h, tm, MLA_QK_PAD), lambda b, i: (b, 0, i, 0)),
            pl.BlockSpec((None, h, tm, MLA_QK_PAD), lambda b, i: (b, 0, i, 0)),
            pl.BlockSpec((None, h, tm, LANES), lambda b, i: (b, 0, i, 0)),
        ],
        out_shape=[
            jax.ShapeDtypeStruct((bsz, h, seq, MLA_QK_PAD), BF16),
            jax.ShapeDtypeStruct((bsz, h, seq, MLA_QK_PAD), BF16),
            jax.ShapeDtypeStruct((bsz, h, seq, LANES), BF16),
        ],
        compiler_params=_cparams(("parallel", "parallel")),
        name="mla_prep",
    )(proj3, q_norm_g.reshape(1, -1), kv_norm_g.reshape(1, -1), w_uq_pad, w_ukv,
      q_head_g_pad, k_head_g_pad, cos_pe, sin_pe)


def _moba_gates(q_ref, kmean_ref):
    km = kmean_ref[...]
    nb = km.shape[0]
    hi = km.astype(BF16).astype(F32)
    mid = (km - hi).astype(BF16).astype(F32)
    lo = (km - hi - mid).astype(BF16).astype(F32)
    pieces = jnp.concatenate([hi, mid, lo, jnp.zeros_like(km)], axis=0).astype(BF16)
    r = _dot_nt(pieces, q_ref[...])
    return r[:nb] + r[nb:2 * nb] + r[2 * nb:3 * nb]


def _moba_block_choice(gate, n_past):
    blk = lax.broadcasted_iota(jnp.int32, gate.shape, 0)
    rank = jnp.zeros(gate.shape, jnp.int32)
    for o in range(n_past):
        g_o = gate[o:o + 1, :]
        tie = jnp.where(o < blk, 1, 0)
        rank = rank + jnp.where(g_o > gate, 1, jnp.where(g_o == gate, tie, 0))
    return jnp.where(rank < MOBA_TOPK, 1.0, 0.0)


def _attend(q_ref, k_ref, v_ref, kmean_ref, o_ref, t=ATTN_BLOCK):
    moba = kmean_ref is not None
    if moba:
        gates = _moba_gates(q_ref, kmean_ref)
    seq = q_ref.shape[0]
    v_t = jnp.transpose(v_ref[...].astype(F32)).astype(BF16)
    key = lax.broadcasted_iota(jnp.int32, (t, t), 0)
    qry = lax.broadcasted_iota(jnp.int32, (t, t), 1)
    causal = key <= qry
    for i in range(seq // t):
        own = slice(i * t, (i + 1) * t)
        q = q_ref[own, :]
        s_own = jnp.where(causal, _dot_nt(k_ref[own, :], q), NEG)
        m = jnp.max(s_own, axis=0, keepdims=True)
        s_past = []
        if i > 0:
            s_all = _dot_nt(k_ref[:i * t, :], q)
            if moba and i > MOBA_TOPK:
                sel = _moba_block_choice(gates[:, own], i)
                s_past = [jnp.where(sel[n:n + 1, :] > 0.5, s_all[n * t:(n + 1) * t, :], NEG)
                          for n in range(i)]
            else:
                s_past = [s_all]
            for s in s_past:
                m = jnp.maximum(m, jnp.max(s, axis=0, keepdims=True))
        p = jnp.exp(s_own - m)
        l = jnp.sum(p, axis=0, keepdims=True)
        acc_t = _dot(v_t[:, own], p.astype(BF16))
        start = 0
        for s in s_past:
            p = jnp.exp(s - m)
            l = l + jnp.sum(p, axis=0, keepdims=True)
            acc_t = acc_t + _dot(v_t[:, start:start + s.shape[0]], p.astype(BF16))
            start += s.shape[0]
        o_ref[own, :] = jnp.transpose(acc_t / l).astype(o_ref.dtype)


def _head_spec(seq, d):
    return pl.BlockSpec((None, None, seq, d), lambda b, hh: (b, hh, 0, 0))


def _proj_col_spec(seq, col):
    cb = col // LANES
    return pl.BlockSpec((None, seq, LANES), lambda b, hh: (b, 0, cb + hh))


def _mla_attn_kernel(q_ref, k_ref, v_ref, o_ref):
    _attend(q_ref, k_ref, v_ref, None, o_ref, t=MLA_ATTN_BLOCK)


def mla_attention(q, k, v):
    bsz, h, seq, dk = q.shape
    return pl.pallas_call(
        _mla_attn_kernel,
        grid=(bsz, h),
        in_specs=[_head_spec(seq, dk), _head_spec(seq, dk), _head_spec(seq, LANES)],
        out_specs=pl.BlockSpec((None, seq, LANES), lambda b, hh: (b, 0, hh)),
        out_shape=jax.ShapeDtypeStruct((bsz, seq, h * LANES), BF16),
        compiler_params=_cparams(("parallel", "parallel")),
        name="mla_attention",
    )(q, k, v)


def _moba_prep(q_ref, k_ref, qg_ref, kg_ref, cos_ref, sin_ref, qo_ref, ko_ref, km_ref):
    seq = q_ref.shape[0]
    scale = HEAD_DIM ** -0.5
    for blk in range(seq // MOBA_BLOCK):
        sl = slice(blk * MOBA_BLOCK, (blk + 1) * MOBA_BLOCK)
        cos2, sin_s = cos_ref[sl, :], sin_ref[sl, :]
        qn = _rms(q_ref[sl, :].astype(F32)) * qg_ref[...]
        qo_ref[sl, :] = (_rope_full(qn, cos2, sin_s) * scale).astype(qo_ref.dtype)
        kn = _rope_full(_rms(k_ref[sl, :].astype(F32)) * kg_ref[...], cos2, sin_s)
        ko_ref[sl, :] = kn.astype(ko_ref.dtype)
        km_ref[blk:blk + 1, :] = jnp.mean(kn, axis=0, keepdims=True)


def _moba_attn_kernel(q_ref, k_ref, v_ref, qg_ref, kg_ref, cos_ref, sin_ref, o_ref,
                      qn_ref, kn_ref, km_ref):
    _moba_prep(q_ref, k_ref, qg_ref, kg_ref, cos_ref, sin_ref, qn_ref, kn_ref, km_ref)
    _attend(qn_ref, kn_ref, v_ref, km_ref, o_ref)


def moba_attention(proj3, q_head_g, k_head_g, cos_full, sin_full):
    bsz, seq, _ = proj3.shape
    h = GROUP_HEADS
    const = lambda shape: pl.BlockSpec(shape, lambda b, hh: (0,) * len(shape))
    return pl.pallas_call(
        _moba_attn_kernel,
        grid=(bsz, h),
        in_specs=[
            _proj_col_spec(seq, COL_MOBA),
            _proj_col_spec(seq, COL_MOBA + GROUP_WIDTH),
            _proj_col_spec(seq, COL_MOBA + 2 * GROUP_WIDTH),
            const((1, LANES)), const((1, LANES)),
            const((seq, LANES)), const((seq, LANES)),
        ],
        out_specs=pl.BlockSpec((None, seq, LANES), lambda b, hh: (b, 0, hh)),
        out_shape=jax.ShapeDtypeStruct((bsz, seq, h * LANES), BF16),
        scratch_shapes=[pltpu.VMEM((seq, LANES), BF16), pltpu.VMEM((seq, LANES), BF16),
                        pltpu.VMEM((seq // MOBA_BLOCK, LANES), F32)],
        compiler_params=_cparams(("parallel", "parallel")),
        name="moba_attention",
    )(proj3, proj3, proj3, q_head_g.reshape(1, -1), k_head_g.reshape(1, -1), cos_full, sin_full)


SB_DEAD_TAIL = -104.0


def _sb_tile(q, k_blk, v_blk, tail, later, strict):
    z = _dot_nt(q, k_blk) * (HEAD_DIM ** -0.5)
    log_1m = -_softplus(z)
    if strict is not None:
        log_1m = jnp.where(strict, log_1m, 0.0)
    hi = log_1m.astype(BF16)
    lo = (log_1m - hi.astype(F32)).astype(BF16)
    t = hi.shape[0]
    both = _dot(jnp.concatenate([hi, lo], axis=0), later)
    suffix = both[:t] + both[t:]
    a = jnp.exp(z + log_1m + suffix + tail)
    if strict is not None:
        a = jnp.where(strict, a, 0.0)
    return _dot(a.astype(v_blk.dtype), v_blk), jnp.sum(log_1m, axis=-1, keepdims=True)


def _sb_kernel(q_ref, k_ref, v_ref, o_ref, acc_ref, tail_ref):
    seq = q_ref.shape[0]
    t = ATTN_BLOCK
    row = lax.broadcasted_iota(jnp.int32, (t, t), 0)
    col = lax.broadcasted_iota(jnp.int32, (t, t), 1)
    strict = col < row
    later = jnp.where(row > col, 1.0, 0.0).astype(BF16)
    blk = lambda n: slice(n * t, (n + 1) * t)

    for i in range(seq // t):
        q = q_ref[blk(i), :]
        acc, tail = _sb_tile(q, k_ref[blk(i), :], v_ref[blk(i), :], 0.0, later, strict)
        if i > 0:
            y, s = _sb_tile(q, k_ref[blk(i - 1), :], v_ref[blk(i - 1), :], tail, later, None)
            acc, tail = acc + y, tail + s
        acc_ref[blk(i), :] = acc
        tail_ref[blk(i), :] = tail

    for i in range(2, seq // t):
        alive = jnp.max(tail_ref[blk(i), :], axis=0, keepdims=True)[0, 0] > SB_DEAD_TAIL

        @pl.when(alive)
        def _():
            q = q_ref[blk(i), :]
            acc, tail = acc_ref[blk(i), :], tail_ref[blk(i), :]
            for n in range(i - 2, -1, -1):
                y, s = _sb_tile(q, k_ref[blk(n), :], v_ref[blk(n), :], tail, later, None)
                acc, tail = acc + y, tail + s
            acc_ref[blk(i), :] = acc

    o_ref[...] = acc_ref[...].astype(o_ref.dtype)


def sb_attention(proj3):
    bsz, seq, _ = proj3.shape
    return pl.pallas_call(
        _sb_kernel,
        grid=(bsz, GROUP_HEADS),
        in_specs=[
            _proj_col_spec(seq, COL_SB),
            _proj_col_spec(seq, COL_SB + GROUP_WIDTH),
            _proj_col_spec(seq, COL_SB + 2 * GROUP_WIDTH),
        ],
        out_specs=pl.BlockSpec((None, seq, LANES), lambda b, hh: (b, 0, hh)),
        out_shape=jax.ShapeDtypeStruct((bsz, seq, GROUP_WIDTH), BF16),
        scratch_shapes=[pltpu.VMEM((seq, LANES), F32), pltpu.VMEM((seq, 1), F32)],
        compiler_params=_cparams(("parallel", "parallel")),
        name="sb_attention",
    )(proj3, proj3, proj3)


def _ret_kernel(*refs):
    nh = GROUP_HEADS
    q_refs, k_refs, v_refs, g_refs = (refs[i * nh:(i + 1) * nh] for i in range(4))
    cos_ref, sin_ref, lg_ref, o_ref = refs[4 * nh:]
    seq = o_ref.shape[0]
    c = RET_CHUNK
    ri = lax.broadcasted_iota(jnp.int32, (c, c), 0).astype(F32)
    ci = lax.broadcasted_iota(jnp.int32, (c, c), 1).astype(F32)
    rel = ri - ci
    idx = lax.broadcasted_iota(jnp.int32, (c, 1), 0).astype(F32)
    k_scale = HEAD_DIM ** -0.5
    decays = []
    for h in range(nh):
        log_gamma = lg_ref[h]
        decays.append((
            jnp.where(rel >= 0, jnp.exp(jnp.maximum(rel, 0.0) * log_gamma), 0.0),
            jnp.exp((idx + 1.0) * log_gamma),
            jnp.exp((c - 1.0 - idx) * log_gamma),
            jnp.exp(c * log_gamma)))

    states = [jnp.zeros((HEAD_DIM, HEAD_DIM), F32) for _ in range(nh)]
    for n in range(seq // c):
        sl = slice(n * c, (n + 1) * c)
        cos2, sin_s = cos_ref[sl, :], sin_ref[sl, :]
        for h in range(nh):
            intra_decay, query_decay, key_decay, chunk_decay = decays[h]
            q = _rope_full(q_refs[h][sl, :].astype(F32), cos2, sin_s)
            k = _rope_full(k_refs[h][sl, :].astype(F32), cos2, sin_s) * k_scale
            vb = v_refs[h][sl, :]
            qb = q.astype(BF16)
            scores = _dot_nt(qb, k.astype(BF16)) * intra_decay
            y = _dot(scores.astype(BF16), vb)
            y = y + _dot(qb, states[h].astype(BF16)) * query_decay
            kd_t = jnp.transpose(k * key_decay).astype(BF16)
            states[h] = states[h] * chunk_decay + _dot(kd_t, vb)
            mu = jnp.mean(y, axis=-1, keepdims=True)
            yc = y - mu
            var = jnp.mean(yc * yc, axis=-1, keepdims=True)
            yn = yc * lax.rsqrt(var + NORM_EPS)
            g = g_refs[h][sl, :].astype(F32)
            o_ref[sl, h * LANES:(h + 1) * LANES] = (g * _sigmoid(g) * yn).astype(o_ref.dtype)


def retention(proj3, cos_full, sin_full):
    bsz, seq, _ = proj3.shape
    h = GROUP_HEADS
    log_gamma = jnp.log(1.0 - 2.0 ** (-5.0 - jnp.arange(h, dtype=F32))).reshape(h, 1, 1)

    def col(k, hh):
        cb = (COL_RET + k * GROUP_WIDTH) // LANES + hh
        return pl.BlockSpec((None, seq, LANES), lambda b: (b, 0, cb))

    const = lambda shape: pl.BlockSpec(shape, lambda b: (0,) * len(shape))
    heads = [col(k, hh) for k in range(4) for hh in range(h)]
    return pl.pallas_call(
        _ret_kernel,
        grid=(bsz,),
        in_specs=heads + [const((seq, LANES)), const((seq, LANES)), const((h, 1, 1))],
        out_specs=pl.BlockSpec((None, seq, h * LANES), lambda b: (b, 0, 0)),
        out_shape=jax.ShapeDtypeStruct((bsz, seq, GROUP_WIDTH), BF16),
        compiler_params=_cparams(("parallel",)),
        name="retention",
    )(*([proj3] * (4 * h)), cos_full, sin_full, log_gamma)


def _out_proj_kernel(y0_ref, y1_ref, y2_ref, y3_ref, gg_ref, w_ref, x_ref, gate_ref, o_ref, h_ref):
    @pl.when(pl.program_id(1) == 0)
    def _():
        for grp, y_ref in enumerate((y0_ref, y1_ref, y2_ref, y3_ref)):
            yn = _rms(y_ref[...].astype(F32)) * gg_ref[grp:grp + 1, :]
            h_ref[:, grp * GROUP_WIDTH:(grp + 1) * GROUP_WIDTH] = yn.astype(h_ref.dtype)

    o_ref[...] = x_ref[...] + gate_ref[...] * _dot(h_ref[...], w_ref[...])


def out_proj_residual(ys, group_g, w_out, x2, gate, seq, tm=1024, tn=1024):
    t, d = x2.shape
    tm = min(tm, seq)
    per_b = seq // tm
    bsz = gate.shape[0]
    y_spec = pl.BlockSpec((tm, GROUP_WIDTH), lambda i, j: (i, 0))
    return pl.pallas_call(
        _out_proj_kernel,
        grid=(t // tm, d // tn),
        in_specs=[y_spec, y_spec, y_spec, y_spec,
                  pl.BlockSpec((4, GROUP_WIDTH), lambda i, j: (0, 0)),
                  pl.BlockSpec((4 * GROUP_WIDTH, tn), lambda i, j: (0, j)),
                  pl.BlockSpec((tm, tn), lambda i, j: (i, j)),
                  pl.BlockSpec((None, 1, tn), lambda i, j: (i // per_b, 0, j))],
        out_specs=pl.BlockSpec((tm, tn), lambda i, j: (i, j)),
        out_shape=jax.ShapeDtypeStruct((t, d), F32),
        scratch_shapes=[pltpu.VMEM((tm, 4 * GROUP_WIDTH), BF16)],
        compiler_params=_cparams(("parallel", "arbitrary")),
        name="out_proj_residual",
    )(*[y.reshape(t, GROUP_WIDTH) for y in ys], group_g, w_out, x2, gate.reshape(bsz, 1, d))


def _swiglu_act(h, w1_ref, w3_ref):
    a = _dot(h, w1_ref[...].astype(BF16))
    return (a * _sigmoid(a) * _dot(h, w3_ref[...].astype(BF16))).astype(BF16)


FFN_UP_STEPS = 11
FFN_DOWN_STEPS = 4


def _down_proj(a_ref, w2_ref):
    n, _, tf = a_ref.shape
    acc = _dot(a_ref[0], w2_ref[:tf, :])
    for c in range(1, n):
        acc = acc + _dot(a_ref[c], w2_ref[c * tf:(c + 1) * tf, :])
    return acc


def _ffn_kernel(x_ref, g_ref, sh_ref, sc_ref, w1_ref, w3_ref, w2_ref, xc_ref, gate_ref, o_ref,
                h_ref, a_ref):
    s = pl.program_id(1)

    @pl.when(s == 0)
    def _():
        _norm_mod_rows(x_ref, g_ref, sh_ref, sc_ref, h_ref)

    @pl.when(s < FFN_UP_STEPS)
    def _():
        a_ref[s] = _swiglu_act(h_ref[...], w1_ref, w3_ref)

    @pl.when(s >= FFN_UP_STEPS)
    def _():
        o_ref[...] = xc_ref[...] + gate_ref[...] * _down_proj(a_ref, w2_ref)


def _ffn_chunks(ffn, d):
    return ffn // FFN_UP_STEPS, d // FFN_DOWN_STEPS


def _up_idx(s):
    return jnp.minimum(s, FFN_UP_STEPS - 1)


def _down_idx(s):
    return jnp.maximum(s - FFN_UP_STEPS, 0)


def dense_ffn_residual(x2, g, shift, scale, gate, w1, w3, w2, seq, tm=1024):
    t, d = x2.shape
    ffn = w1.shape[1]
    tf, tn = _ffn_chunks(ffn, d)
    w3, w2 = w3.astype(BF16), w2.astype(BF16)
    tm = min(tm, seq)
    per_b = seq // tm
    bsz = gate.shape[0]
    row = lambda: pl.BlockSpec((None, 1, d), lambda i, s: (i // per_b, 0, 0))
    return pl.pallas_call(
        _ffn_kernel,
        grid=(t // tm, FFN_UP_STEPS + FFN_DOWN_STEPS),
        in_specs=[pl.BlockSpec((tm, d), lambda i, s: (i, 0), pipeline_mode=pl.Buffered(1)),
                  pl.BlockSpec((1, d), lambda i, s: (0, 0)),
                  row(), row(),
                  pl.BlockSpec((d, tf), lambda i, s: (0, _up_idx(s))),
                  pl.BlockSpec((d, tf), lambda i, s: (0, _up_idx(s))),
                  pl.BlockSpec((ffn, tn), lambda i, s: (0, _down_idx(s))),
                  pl.BlockSpec((tm, tn), lambda i, s: (i, _down_idx(s))),
                  pl.BlockSpec((None, 1, tn), lambda i, s: (i // per_b, 0, _down_idx(s)))],
        out_specs=pl.BlockSpec((tm, tn), lambda i, s: (i, _down_idx(s))),
        out_shape=jax.ShapeDtypeStruct((t, d), F32),
        scratch_shapes=[pltpu.VMEM((tm, d), BF16), pltpu.VMEM((FFN_UP_STEPS, tm, tf), BF16)],
        compiler_params=_cparams(("parallel", "arbitrary"), VMEM_LIMIT_MOE),
        name="dense_ffn_residual",
    )(x2, g.reshape(1, d), shift.reshape(bsz, 1, d), scale.reshape(bsz, 1, d),
      w1, w3, w2, x2, gate.reshape(bsz, 1, d))


def _router_kernel(x_ref, g_ref, sh_ref, sc_ref, rw_ref, h_ref, idx_ref, gates_ref):
    _norm_mod_rows(x_ref, g_ref, sh_ref, sc_ref, h_ref)
    hf, rw = h_ref[...], rw_ref[...]
    h_hi, w_hi = hf.astype(BF16), rw.astype(BF16)
    h_lo = (hf - h_hi.astype(F32)).astype(BF16)
    w_lo = (rw - w_hi.astype(F32)).astype(BF16)
    logits = _dot(h_hi, w_hi) + _dot(h_hi, w_lo) + _dot(h_lo, w_hi)
    lane = lax.broadcasted_iota(jnp.int32, logits.shape, 1)
    lane_f = lane.astype(F32)
    logits = jnp.where(lane < N_EXPERTS, logits, -jnp.inf)
    m0 = jnp.max(logits, axis=-1, keepdims=True)
    e0 = jnp.min(jnp.where(logits == m0, lane_f, float(LANES)), axis=-1, keepdims=True)
    rest = jnp.where(lane_f == e0, -jnp.inf, logits)
    m1 = jnp.max(rest, axis=-1, keepdims=True)
    e1 = jnp.min(jnp.where(rest == m1, lane_f, float(LANES)), axis=-1, keepdims=True)
    p1 = jnp.exp(m1 - m0)
    g0 = 1.0 / (1.0 + p1)
    idx_ref[...] = jnp.where(lane == 0, e0, jnp.where(lane == 1, e1, 0.0)).astype(jnp.int32)
    gates_ref[...] = jnp.where(lane == 0, g0, jnp.where(lane == 1, p1 * g0, 0.0))


def moe_router(x2, g, shift, scale, router_w, seq, tm=512):
    t, d = x2.shape
    tm = min(tm, seq)
    per_b = seq // tm
    bsz = shift.shape[0]
    rw = jnp.zeros((d, LANES), F32).at[:, :N_EXPERTS].set(router_w)
    row = lambda: pl.BlockSpec((None, 1, d), lambda i: (i // per_b, 0, 0))
    return pl.pallas_call(
        _router_kernel,
        grid=(t // tm,),
        in_specs=[pl.BlockSpec((tm, d), lambda i: (i, 0)),
                  pl.BlockSpec((1, d), lambda i: (0, 0)),
                  row(), row(),
                  pl.BlockSpec((d, LANES), lambda i: (0, 0))],
        out_specs=[pl.BlockSpec((tm, d), lambda i: (i, 0)),
                   pl.BlockSpec((tm, LANES), lambda i: (i, 0)),
                   pl.BlockSpec((tm, LANES), lambda i: (i, 0))],
        out_shape=[jax.ShapeDtypeStruct((t, d), F32),
                   jax.ShapeDtypeStruct((t, LANES), jnp.int32),
                   jax.ShapeDtypeStruct((t, LANES), F32)],
        compiler_params=_cparams(("parallel",)),
        name="moe_router",
    )(x2, g.reshape(1, d), shift.reshape(bsz, 1, d), scale.reshape(bsz, 1, d), rw)


GATHER_UNROLL = 8
RANK_CHUNK = 256
MOE_TILE_ROWS = 1056


def _row_copy(src_ref, dst_ref, sem, src_row, dst_row):
    return pltpu.make_async_copy(src_ref.at[pl.ds(src_row, 1), :],
                                 dst_ref.at[pl.ds(dst_row, 1), :], sem)


def _start_row_gather(src_ref, dst_ref, sem, row_of, n_rows):
    def start(r, _):
        _row_copy(src_ref, dst_ref, sem, row_of(r), r).start()
        return 0

    lax.fori_loop(0, n_rows, start, 0, unroll=GATHER_UNROLL)


def _wait_row_gather(dst_ref, sem):
    pltpu.make_async_copy(dst_ref, dst_ref, sem).wait()


def _moe_ffn_kernel(tile_e_ref, tile_ok_ref, buf_t_ref, h_hbm, w1_ref, w3_ref, w2_ref, o_ref,
                    hf_ref, hb_ref, a_ref, sem):
    i = pl.program_id(0)
    f = pl.program_id(1)
    n_tiles = pl.num_programs(0)
    nf = FFN_UP_STEPS
    tm = hb_ref.shape[0]
    rows_per_step = tm // nf
    ok = tile_ok_ref[i] > 0
    last = (i == n_tiles - 1) & (f == nf - 1)
    nxt = jnp.minimum(i + 1, n_tiles - 1)
    started = (i == 0) | (tile_ok_ref[jnp.maximum(i - 1, 0)] > 0)

    @pl.when(f == 0)
    def _():
        @pl.when(i == 0)
        def _():
            _start_row_gather(h_hbm, hf_ref, sem, lambda r: buf_t_ref[r], tm)

        @pl.when(started)
        def _():
            _wait_row_gather(hf_ref, sem)

        @pl.when(ok)
        def _():
            hb_ref[...] = hf_ref[...].astype(hb_ref.dtype)

    @pl.when(ok & (f < nf))
    def _():
        base = f * rows_per_step
        for r in range(rows_per_step):
            _row_copy(h_hbm, hf_ref, sem, buf_t_ref[nxt * tm + base + r], base + r).start()
        a_ref[f] = _swiglu_act(hb_ref[...], w1_ref, w3_ref)

    @pl.when(ok & last)
    def _():
        _wait_row_gather(hf_ref, sem)

    @pl.when(f >= nf)
    def _():
        @pl.when(ok)
        def _():
            o_ref[...] = _down_proj(a_ref, w2_ref)

        @pl.when(jnp.logical_not(ok))
        def _():
            o_ref[...] = jnp.zeros_like(o_ref)


def moe_expert_ffn(h2, tile_e, tile_ok, buf_t, w1, w3, w2, tm):
    t, d = h2.shape
    ffn = w1.shape[2]
    tf, tn = _ffn_chunks(ffn, d)
    w2 = w2.astype(BF16)
    n_tiles = tile_e.shape[0]

    def up(i, s, ok):
        return jnp.where(ok[i] > 0, _up_idx(s), FFN_UP_STEPS - 1)

    def down(i, s, ok):
        return jnp.where(ok[i] > 0, _down_idx(s), 0)

    grid_spec = pltpu.PrefetchScalarGridSpec(
        num_scalar_prefetch=3,
        grid=(n_tiles, FFN_UP_STEPS + FFN_DOWN_STEPS),
        in_specs=[pl.BlockSpec(memory_space=pl.ANY),
                  pl.BlockSpec((None, d, tf), lambda i, s, te, ok, bt: (te[i], 0, up(i, s, ok))),
                  pl.BlockSpec((None, d, tf), lambda i, s, te, ok, bt: (te[i], 0, up(i, s, ok))),
                  pl.BlockSpec((None, ffn, tn), lambda i, s, te, ok, bt: (te[i], 0, down(i, s, ok)))],
        out_specs=pl.BlockSpec((tm, tn), lambda i, s, te, ok, bt: (i, _down_idx(s))),
        scratch_shapes=[pltpu.VMEM((tm, d), F32), pltpu.VMEM((tm, d), BF16),
                        pltpu.VMEM((FFN_UP_STEPS, tm, tf), BF16), pltpu.SemaphoreType.DMA(())],
    )
    return pl.pallas_call(
        _moe_ffn_kernel,
        grid_spec=grid_spec,
        out_shape=jax.ShapeDtypeStruct((n_tiles * tm, d), F32),
        compiler_params=_cparams(("arbitrary", "arbitrary"), VMEM_LIMIT_MOE),
        name="moe_expert_ffn",
    )(tile_e, tile_ok, buf_t, h2, w1, w3, w2)


def _moe_combine_kernel(d0_ref, d1_ref, y_hbm, x_ref, gate_ref, gates_ref, o_ref,
                        y0_ref, y1_ref, sems0, sems1):
    i = pl.program_id(0)
    tm = x_ref.shape[0]
    slot = i % 2

    def start_gather(tile, into):
        base = tile * tm
        for r in range(tm):
            _row_copy(y_hbm, y0_ref.at[into], sems0.at[into], d0_ref[base + r], r).start(priority=0)
            _row_copy(y_hbm, y1_ref.at[into], sems1.at[into], d1_ref[base + r], r).start(priority=1)

    @pl.when(i == 0)
    def _():
        start_gather(0, 0)

    for s in (0, 1):
        @pl.when((slot == s) & (i + 1 < pl.num_programs(0)))
        def _():
            start_gather(i + 1, 1 - s)

    _wait_row_gather(y0_ref.at[slot], sems0.at[slot])
    _wait_row_gather(y1_ref.at[slot], sems1.at[slot])
    gates = gates_ref[...]
    y = gates[:, 0:1] * y0_ref[slot] + gates[:, 1:2] * y1_ref[slot]
    o_ref[...] = x_ref[...] + gate_ref[...] * y


def moe_combine_residual(dest0, dest1, y_buf, x2, gate, gates, seq, tm=256):
    t, d = x2.shape
    tm = min(tm, seq)
    per_b = seq // tm
    bsz = gate.shape[0]
    grid_spec = pltpu.PrefetchScalarGridSpec(
        num_scalar_prefetch=2,
        grid=(t // tm,),
        in_specs=[pl.BlockSpec(memory_space=pl.ANY),
                  pl.BlockSpec((tm, d), lambda i, d0, d1: (i, 0)),
                  pl.BlockSpec((None, 1, d), lambda i, d0, d1: (i // per_b, 0, 0)),
                  pl.BlockSpec((tm, LANES), lambda i, d0, d1: (i, 0))],
        out_specs=pl.BlockSpec((tm, d), lambda i, d0, d1: (i, 0)),
        scratch_shapes=[pltpu.VMEM((2, tm, d), F32), pltpu.VMEM((2, tm, d), F32),
                        pltpu.SemaphoreType.DMA((2,)), pltpu.SemaphoreType.DMA((2,))],
    )
    return pl.pallas_call(
        _moe_combine_kernel,
        grid_spec=grid_spec,
        out_shape=jax.ShapeDtypeStruct((t, d), F32),
        compiler_params=_cparams(("arbitrary",)),
        name="moe_combine_residual",
    )(dest0, dest1, y_buf, x2, gate.reshape(bsz, 1, d), gates)


def _moe_routing_tables(idx, tm):
    t = idx.shape[0]
    n_slots = t * MOE_TOPK
    n_tiles = n_slots // tm + N_EXPERTS
    flat_e = idx[:, :MOE_TOPK].reshape(-1)
    onehot = (flat_e[:, None] == jnp.arange(N_EXPERTS)[None, :]).astype(jnp.int32)
    counts = jnp.sum(onehot, axis=0)
    chunk = RANK_CHUNK
    oh = onehot.astype(F32).reshape(n_slots // chunk, chunk, N_EXPERTS)
    earlier = jnp.tril(jnp.ones((chunk, chunk), F32), -1)
    within = jnp.einsum("ij,cjk->cik", earlier, oh)
    totals = jnp.sum(oh, axis=1)
    before = jnp.cumsum(totals, axis=0) - totals
    prefix = (within + before[:, None, :]).reshape(n_slots, N_EXPERTS).astype(jnp.int32)
    rank = jnp.sum(prefix * onehot, axis=1)
    padded = (counts + tm - 1) // tm * tm
    pad_ends = jnp.cumsum(padded)
    pad_starts = pad_ends - padded
    dest = (pad_starts[flat_e] + rank).astype(jnp.int32)
    buf_t = jnp.zeros((n_tiles * tm,), jnp.int32).at[dest].set(
        (jnp.arange(n_slots) // MOE_TOPK).astype(jnp.int32))
    tile_start = jnp.arange(n_tiles) * tm
    tile_e = jnp.minimum(jnp.searchsorted(pad_ends, tile_start, side="right"),
                         N_EXPERTS - 1).astype(jnp.int32)
    tile_ok = (tile_start < pad_ends[-1]).astype(jnp.int32)
    dest2 = dest.reshape(t, MOE_TOPK)
    return tile_e, tile_ok, buf_t, dest2[:, 0], dest2[:, 1]


def moe_ffn_residual(x2, g, shift, scale, gate, router_w, w1, w3, w2, seq, tile_rows=MOE_TILE_ROWS):
    h2, idx, gates = moe_router(x2, g, shift, scale, router_w, seq)
    tile_e, tile_ok, buf_t, dest0, dest1 = _moe_routing_tables(idx, tile_rows)
    y_buf = moe_expert_ffn(h2, tile_e, tile_ok, buf_t, w1, w3, w2, tile_rows)
    return moe_combine_residual(dest0, dest1, y_buf, x2, gate, gates, seq)


def _pad_w_in_kernel(w_ref, o_ref):
    rows = w_ref.shape[0]
    kpe_end = MLA_Q_RANK + MLA_KV_RANK + MLA_ROPE_DIM
    tail = PROJ_WIDTH - (COL_MOBA + 3 * GROUP_WIDTH)
    w = w_ref[...]
    o_ref[...] = jnp.concatenate([
        w[:, :kpe_end], jnp.zeros((rows, LANES - MLA_ROPE_DIM), F32),
        w[:, kpe_end:], jnp.zeros((rows, tail), F32)], axis=1).astype(o_ref.dtype)


def _pad_w_in(w_in, rows=256):
    d, n = w_in.shape
    return pl.pallas_call(
        _pad_w_in_kernel,
        grid=(d // rows,),
        in_specs=[pl.BlockSpec((rows, n), lambda i: (i, 0))],
        out_specs=pl.BlockSpec((rows, PROJ_WIDTH), lambda i: (i, 0)),
        out_shape=jax.ShapeDtypeStruct((d, PROJ_WIDTH), BF16),
        compiler_params=_cparams(("parallel",)),
        name="pad_w_in",
    )(w_in)


def _pad_heads_192(a):
    lead = a.shape[:-1]
    a = a.reshape(lead + (GROUP_HEADS, MLA_QK_DIM))
    a = jnp.pad(a, [(0, 0)] * len(lead) + [(0, 0), (0, MLA_QK_PAD - MLA_QK_DIM)])
    return a.reshape(lead + (GROUP_HEADS * MLA_QK_PAD,))


def kernel(x, c, positions, ada_w, ada_b, norm_mix_g, norm_ffn_g, w_in, mla_q_norm_g, mla_kv_norm_g, mla_w_uq, mla_w_ukv, mla_q_head_g, mla_k_head_g, moba_q_head_g, moba_k_head_g, group_norm_g, w_out, ffn_w1, ffn_w3, ffn_w2, router_w, moe_w1, moe_w3, moe_w2):
    bsz, seq, d = x.shape
    depth = ada_w.shape[0]
    cos_pe, sin_pe = _rope_tables(positions, MLA_ROPE_DIM)
    cos_full, sin_full = _rope_tables(positions, HEAD_DIM)
    mod = ada_modulation(c, ada_w, ada_b)
    x2 = x.reshape(bsz * seq, d)
    for l in range(depth):
        shift_m, scale_m, gate_m, shift_f, scale_f, gate_f = jnp.split(mod[l], 6, axis=-1)
        proj = norm_mod_proj(x2, norm_mix_g[l], shift_m, scale_m, _pad_w_in(w_in[l]), seq)
        proj3 = proj.reshape(bsz, seq, PROJ_WIDTH)
        pad_g = lambda g: jnp.pad(g, (0, MLA_QK_PAD - MLA_QK_DIM)).reshape(1, MLA_QK_PAD)
        q_mla, k_mla, v_mla = mla_prep(
            proj3, mla_q_norm_g[l], mla_kv_norm_g[l],
            _pad_heads_192(mla_w_uq[l]).astype(BF16), mla_w_ukv[l].astype(BF16),
            pad_g(mla_q_head_g[l]), pad_g(mla_k_head_g[l]), cos_pe, sin_pe)
        y_mla = mla_attention(q_mla, k_mla, v_mla)
        y_ret = retention(proj3, cos_full, sin_full)
        y_sb = sb_attention(proj3)
        y_moba = moba_attention(proj3, moba_q_head_g[l], moba_k_head_g[l], cos_full, sin_full)
        x2 = out_proj_residual((y_mla, y_ret, y_sb, y_moba), group_norm_g[l],
                               w_out[l].astype(BF16), x2, gate_m, seq)
        j = l // 2
        if l % 2 == 0:
            x2 = dense_ffn_residual(x2, norm_ffn_g[l], shift_f, scale_f, gate_f,
                                    ffn_w1[j], ffn_w3[j], ffn_w2[j], seq)
        else:
            x2 = moe_ffn_residual(x2, norm_ffn_g[l], shift_f, scale_f, gate_f, router_w[j],
                                  moe_w1[j], moe_w3[j], moe_w2[j], seq)
    return x2.reshape(bsz, seq, d)
```
